```python
import math
import jax, jax.numpy as jnp
from jax import lax
import numpy as np

D_MODEL = 1024
BATCH = 1
SEQ = 16384
DEPTH = 1

GRID_W = 64
CTX_LEN = 256
SSD_HEAD_DIM = 64
D_SSD = D_MODEL
SSD_HEADS = D_SSD // SSD_HEAD_DIM
D_STATE = 128
D_CONV = 5
CHUNK = 128
D_POOL = D_MODEL
POOL_WINDOWS = (2, 4, 8, 16)
N_POOL_GROUPS = len(POOL_WINDOWS)
POOL_GROUP_DIM = D_POOL // N_POOL_GROUPS
D_MIX = D_SSD + D_POOL
D_XBC = D_SSD + 2 * D_STATE
D_IN_PROJ = D_SSD + D_XBC + 2 * SSD_HEADS + D_POOL
D_FF = ((8 * D_MODEL // 3 + 255) // 256) * 256
DEEPNORM_ALPHA = (2 * DEPTH) ** 0.25
DEEPNORM_BETA = (8 * DEPTH) ** -0.25
LN_EPS = 1e-5

kernel_name = 'hybrid_ssd_pool_deepnorm_dit_block'


def layer_norm(x, g, b):
    xf = x.astype(jnp.float32)
    mu = jnp.mean(xf, axis=-1, keepdims=True)
    var = jnp.mean(jnp.square(xf - mu), axis=-1, keepdims=True)
    return ((xf - mu) * lax.rsqrt(var + LN_EPS) * g.astype(jnp.float32) + b.astype(jnp.float32)).astype(x.dtype)


def rms_norm(x, g):
    xf = x.astype(jnp.float32)
    return (xf * lax.rsqrt(jnp.mean(xf * xf, axis=-1, keepdims=True) + LN_EPS) * g.astype(jnp.float32)).astype(x.dtype)


def modulate(x, shift, scale):
    return x * (1 + scale) + shift


def split_projection(h, w_in):
    proj = h @ w_in
    o1 = D_SSD
    o2 = o1 + D_XBC
    o3 = o2 + 2 * SSD_HEADS
    return proj[..., :o1], proj[..., o1:o2], proj[..., o2:o3], proj[..., o3:]


def depthwise_conv_centred(u, w, b):
    out = lax.conv_general_dilated(
        u, w[:, None, :].astype(u.dtype), window_strides=(1,),
        padding=[(D_CONV // 2, D_CONV // 2)],
        dimension_numbers=('NWC', 'WIO', 'NWC'),
        feature_group_count=u.shape[-1])
    return out + b


def ssd_chunked_scan(x, dt, a_neg, B, C, h0):
    f32 = jnp.float32
    b, L, H, P = x.shape
    N = B.shape[-1]
    nc = L // CHUNK
    xc = x.astype(f32).reshape(b, nc, CHUNK, H, P)
    dtc = dt.astype(f32).reshape(b, nc, CHUNK, H)
    Bc = B.astype(f32).reshape(b, nc, CHUNK, N)
    Cc = C.astype(f32).reshape(b, nc, CHUNK, N)
    a_cum = jnp.cumsum(dtc * a_neg.astype(f32), axis=2)
    a_cum_h = jnp.moveaxis(a_cum, -1, 2)
    seg = a_cum_h[..., :, None] - a_cum_h[..., None, :]
    lower = jnp.tril(jnp.ones((CHUNK, CHUNK), dtype=bool))
    decay = jnp.exp(jnp.where(lower, seg, -jnp.inf))
    cb = jnp.einsum('bcin,bcjn->bcij', Cc, Bc)
    scores = cb[:, :, None] * decay * jnp.moveaxis(dtc, -1, 2)[..., None, :]
    y_diag = jnp.einsum('bchij,bcjhp->bcihp', scores, xc)
    w_end = jnp.exp(a_cum[:, :, -1:, :] - a_cum) * dtc
    states = jnp.einsum('bcjn,bcjhp->bchpn', Bc, xc * w_end[..., None])
    chunk_decay = jnp.exp(a_cum[:, :, -1, :])

    def step(h, inp):
        dec, st = inp
        return h * dec[:, :, None, None] + st, h

    h_final, h_prev = lax.scan(step, h0.astype(f32),
                               (jnp.moveaxis(chunk_decay, 1, 0), jnp.moveaxis(states, 1, 0)))
    h_prev = jnp.moveaxis(h_prev, 0, 1)
    y_off = jnp.einsum('bcin,bchpn->bcihp', Cc, h_prev) * jnp.exp(a_cum)[..., None]
    y = (y_diag + y_off).reshape(b, L, H, P)
    return y.astype(x.dtype), h_final


def ssd_bidirectional(xbc, dt_raw, dt_bias, a_log, h0_fwd, h0_bwd):
    b, L, _ = xbc.shape
    xs = xbc[..., :D_SSD].reshape(b, L, SSD_HEADS, SSD_HEAD_DIM)
    Bm = xbc[..., D_SSD:D_SSD + D_STATE]
    Cm = xbc[..., D_SSD + D_STATE:]
    dt = jax.nn.softplus(dt_raw.reshape(b, L, 2, SSD_HEADS) + dt_bias)
    a_neg = -jnp.exp(a_log.astype(jnp.float32))
    y_f, h_f = ssd_chunked_scan(xs, dt[:, :, 0], a_neg[0], Bm, Cm, h0_fwd)
    flip = lambda t: jnp.flip(t, axis=1)
    y_b, h_b = ssd_chunked_scan(flip(xs), flip(dt[:, :, 1]), a_neg[1], flip(Bm), flip(Cm), h0_bwd)
    return y_f + flip(y_b), xs, h_f, h_b


def box_mean(u, w, axis):
    n = u.shape[axis]
    pad = [(0, 0)] * u.ndim
    pad[axis] = (1, 0)
    cs = jnp.pad(jnp.cumsum(u, axis=axis), pad)
    pos = jnp.arange(n)
    lo = jnp.clip(pos - w // 2, 0, n)
    hi = jnp.clip(pos + (w - w // 2), 0, n)
    total = jnp.take(cs, hi, axis=axis) - jnp.take(cs, lo, axis=axis)
    shape = [1] * u.ndim
    shape[axis] = n
    return total / (hi - lo).astype(u.dtype).reshape(shape)


def pool_mixer(u, rows, pool_w, pool_scale):
    b, L, _ = u.shape
    uf = u.astype(jnp.float32).reshape(b, L, N_POOL_GROUPS, POOL_GROUP_DIM)
    outs = []
    for g, w in enumerate(POOL_WINDOWS):
        ug = uf[:, :, g]
        if rows is None:
            m = box_mean(ug, w, 1)
        else:
            grid = ug.reshape(b, rows, GRID_W, POOL_GROUP_DIM)
            m = box_mean(box_mean(grid, w, 1), w, 2).reshape(b, L, POOL_GROUP_DIM)
        outs.append(m - ug)
    d = jnp.stack(outs, axis=2)
    y = jnp.einsum('blgi,gio->blgo', d, pool_w.astype(jnp.float32)).reshape(b, L, D_POOL)
    return (y * pool_scale.astype(jnp.float32)).astype(u.dtype)


def merge_head_groups(y_ssd, xs, z, u_pool, rows, d_skip, ssd_norm_g, pool_w, pool_scale, w_out):
    b, L = z.shape[:2]
    y = (y_ssd + d_skip[:, None] * xs).reshape(b, L, D_SSD)
    y = rms_norm(y * jax.nn.silu(z), ssd_norm_g)
    p = pool_mixer(u_pool, rows, pool_w, pool_scale)
    return jnp.concatenate([y, p], axis=-1) @ w_out


def swiglu(h, w_gate, w_up, w_down):
    return (jax.nn.silu(h @ w_gate) * (h @ w_up)) @ w_down


def setup_inputs(seed: int = 0) -> dict:
    key = jax.random.key(seed)
    ks = jax.random.split(key, 26)
    f32 = jnp.float32
    nrm = lambda k, shape, s: jax.random.normal(k, shape, f32) * s
    dt0 = jnp.exp(jax.random.uniform(ks[10], (DEPTH, 2, SSD_HEADS), f32,
                                     minval=math.log(1e-3), maxval=math.log(1e-1)))
    return {
        'x': nrm(ks[0], (BATCH, SEQ, D_MODEL), 1.0),
        'c': nrm(ks[1], (BATCH, D_MODEL), 1.0),
        'ctx': nrm(ks[2], (BATCH, CTX_LEN, D_MODEL), 1.0),
        'c_ctx': nrm(ks[3], (D_MODEL,), 1.0),
        'emb_ln_g': 1.0 + nrm(ks[4], (D_MODEL,), 0.02),
        'emb_ln_b': nrm(ks[5], (D_MODEL,), 0.02),
        'w_ada': nrm(ks[6], (DEPTH, D_MODEL, 6 * D_MODEL), 0.5 * D_MODEL ** -0.5),
        'b_ada': nrm(ks[7], (DEPTH, 6 * D_MODEL), 0.01),
        'in_proj': nrm(ks[8], (DEPTH, D_MODEL, D_IN_PROJ), D_MODEL ** -0.5),
        'conv_w': nrm(ks[9], (DEPTH, D_CONV, D_XBC), D_CONV ** -0.5),
        'conv_b': nrm(ks[11], (DEPTH, D_XBC), 0.01),
        'dt_bias': dt0 + jnp.log(-jnp.expm1(-dt0)),
        'a_log': jnp.log(jax.random.uniform(ks[12], (DEPTH, 2, SSD_HEADS), f32, minval=1.0, maxval=16.0)),
        'd_skip': 1.0 + nrm(ks[13], (DEPTH, SSD_HEADS), 0.1),
        'ssd_norm_g': 1.0 + nrm(ks[14], (DEPTH, D_SSD), 0.02),
        'pool_w': nrm(ks[15], (DEPTH, N_POOL_GROUPS, POOL_GROUP_DIM, POOL_GROUP_DIM), POOL_GROUP_DIM ** -0.5),
        'pool_scale': 1.0 + nrm(ks[16], (DEPTH, D_POOL), 0.02),
        'w_out': nrm(ks[17], (DEPTH, D_MIX, D_MODEL), DEEPNORM_BETA * D_MIX ** -0.5),
        'ln1_g': 1.0 + nrm(ks[18], (DEPTH, D_MODEL), 0.02),
        'ln1_b': nrm(ks[19], (DEPTH, D_MODEL), 0.02),
        'w_gate': nrm(ks[20], (DEPTH, D_MODEL, D_FF), D_MODEL ** -0.5),
        'w_up': nrm(ks[21], (DEPTH, D_MODEL, D_FF), D_MODEL ** -0.5),
        'w_down': nrm(ks[22], (DEPTH, D_FF, D_MODEL), DEEPNORM_BETA * D_FF ** -0.5),
        'ln2_g': 1.0 + nrm(ks[23], (DEPTH, D_MODEL), 0.02),
        'ln2_b': nrm(ks[24], (DEPTH, D_MODEL), 0.02),
    }


def reference(x, c, ctx, c_ctx, emb_ln_g, emb_ln_b, w_ada, b_ada, in_proj, conv_w, conv_b,
              dt_bias, a_log, d_skip, ssd_norm_g, pool_w, pool_scale, w_out, ln1_g, ln1_b,
              w_gate, w_up, w_down, ln2_g, ln2_b):
    b = x.shape[0]
    rows = x.shape[1] // GRID_W
    x = layer_norm(x, emb_ln_g, emb_ln_b)
    xc = layer_norm(ctx, emb_ln_g, emb_ln_b)
    silu_c = jax.nn.silu(c)
    silu_cc = jax.nn.silu(c_ctx)
    h_zero = jnp.zeros((b, SSD_HEADS, SSD_HEAD_DIM, D_STATE), jnp.float32)
    for l in range(DEPTH):
        mod = (silu_c @ w_ada[l] + b_ada[l])[:, None, :]
        sh1, sc1, g1, sh2, sc2, g2 = jnp.split(mod, 6, axis=-1)
        modc = silu_cc @ w_ada[l] + b_ada[l]
        sh1c, sc1c, g1c, sh2c, sc2c, g2c = jnp.split(modc, 6, axis=-1)

        zc, xbcc, dtc, upc = split_projection(modulate(xc, sh1c, sc1c), in_proj[l])
        xbcc = jax.nn.silu(depthwise_conv_centred(xbcc, conv_w[l], conv_b[l]))
        yc, xsc, hf_ctx, hb_ctx = ssd_bidirectional(xbcc, dtc, dt_bias[l], a_log[l], h_zero, h_zero)

        z, xbc, dt_raw, up = split_projection(modulate(x, sh1, sc1), in_proj[l])
        xbc = jax.nn.silu(depthwise_conv_centred(xbc, conv_w[l], conv_b[l]))
        y, xs, _, _ = ssd_bidirectional(xbc, dt_raw, dt_bias[l], a_log[l], hf_ctx, hb_ctx)
        mix = merge_head_groups(y, xs, z, up, rows, d_skip[l], ssd_norm_g[l], pool_w[l],
                                pool_scale[l], w_out[l])
        x = layer_norm(DEEPNORM_ALPHA * x + g1 * mix, ln1_g[l], ln1_b[l])
        ffn = swiglu(modulate(x, sh2, sc2), w_gate[l], w_up[l], w_down[l])
        x = layer_norm(DEEPNORM_ALPHA * x + g2 * ffn, ln2_g[l], ln2_b[l])

        if l + 1 < DEPTH:
            mixc = merge_head_groups(yc, xsc, zc, upc, None, d_skip[l], ssd_norm_g[l], pool_w[l],
                                     pool_scale[l], w_out[l])
            xc = layer_norm(DEEPNORM_ALPHA * xc + g1c * mixc, ln1_g[l], ln1_b[l])
            ffnc = swiglu(modulate(xc, sh2c, sc2c), w_gate[l], w_up[l], w_down[l])
            xc = layer_norm(DEEPNORM_ALPHA * xc + g2c * ffnc, ln2_g[l], ln2_b[l])
    return x
```

```python
import functools

import jax
import jax.numpy as jnp
import numpy as np
from jax import lax
from jax.experimental import pallas as pl
from jax.experimental.pallas import tpu as pltpu

F32 = jnp.float32
BF16 = jnp.bfloat16

D_MODEL = 1024
SSD_HEADS = 16
SSD_HEAD_DIM = 64
D_SSD = SSD_HEADS * SSD_HEAD_DIM
D_STATE = 128
D_CONV = 5
CHUNK = 128
D_POOL = 1024
POOL_WINDOWS = (2, 4, 8, 16)
POOL_GROUP_DIM = D_POOL // len(POOL_WINDOWS)
GRID_W = 64
D_XBC = D_SSD + 2 * D_STATE
D_XD = D_XBC + 128
D_FF = 2816
DEPTH = 1
DEEPNORM_ALPHA = (2 * DEPTH) ** 0.25
LN_EPS = 1e-5

HALO = 16
POOL_HALO = 512
N_DH = 2 * SSD_HEADS
VMEM_LIMIT = 56 * 1024 * 1024


def _dot(a, b):
    return jnp.dot(a, b, preferred_element_type=F32)


def _silu(x):
    return x / (1.0 + jnp.exp(-x))


def _layer_norm(x, g, b):
    mu = jnp.mean(x, axis=-1, keepdims=True)
    xc = x - mu
    var = jnp.mean(xc * xc, axis=-1, keepdims=True)
    return xc * lax.rsqrt(var + LN_EPS) * g + b


def _bf16_part(v):
    return v.astype(BF16).astype(F32)


def _mod_kernel(cc_ref, w_ref, b_ref, out_ref):
    s = _silu(cc_ref[...])
    w = w_ref[...]
    r0 = jnp.sum(w * s[:, 0:1], axis=0, keepdims=True) + b_ref[...]
    r1 = jnp.sum(w * s[:, 1:2], axis=0, keepdims=True) + b_ref[...]
    out_ref[...] = jnp.concatenate([r0, r1, jnp.zeros((6, w.shape[1]), F32)], axis=0)


def _mod_call(cc, w_ada, b_ada):
    tn = 512
    n = w_ada.shape[1]
    return pl.pallas_call(
        _mod_kernel,
        grid=(n // tn,),
        in_specs=[pl.BlockSpec((D_MODEL, 2), lambda j: (0, 0)),
                  pl.BlockSpec((D_MODEL, tn), lambda j: (0, j)),
                  pl.BlockSpec((1, tn), lambda j: (0, j))],
        out_specs=pl.BlockSpec((8, tn), lambda j: (0, j)),
        out_shape=jax.ShapeDtypeStruct((8, n), F32),
        name="mod",
    )(cc, w_ada, b_ada)


def _inproj_kernel(x_ref, xp_ref, xn_ref, lng_ref, lnb_ref, sh_ref, sc_ref, wz_ref, wxd_ref, wp_ref,
                   cw_ref, cb_ref, dtb_ref, alog_ref,
                   z_ref, xbc_ref, up_ref, rowpack_ref, colsplit_ref, ext_scr, *, tm):
    i = pl.program_id(0)
    n = pl.num_programs(0)
    lng, lnb, sh, sc = lng_ref[...], lnb_ref[...], sh_ref[...], sc_ref[...]

    def prep(xv):
        return _layer_norm(xv, lng, lnb) * (1.0 + sc) + sh

    h = prep(x_ref[...]).astype(BF16)
    hp = (prep(xp_ref[...]) * (i > 0).astype(F32)).astype(BF16)
    hn = (prep(xn_ref[...]) * (i < n - 1).astype(F32)).astype(BF16)

    z_ref[...] = _dot(h, wz_ref[...]).astype(BF16)
    up_ref[...] = _dot(h, wp_ref[...]).astype(BF16)
    ext_scr[...] = _dot(jnp.concatenate([hp, h, hn], axis=0), wxd_ref[...])

    acc = jnp.broadcast_to(cb_ref[...], (tm, D_XBC))
    for k in range(D_CONV):
        acc = acc + cw_ref[k:k + 1, :] * ext_scr[pl.ds(HALO - D_CONV // 2 + k, tm), 0:D_XBC]
    xbc_ref[...] = _silu(acc).astype(BF16)

    dt_raw = ext_scr[pl.ds(HALO, tm), D_XBC:D_XD].T[0:N_DH, :] + dtb_ref[...]
    dt = jnp.maximum(dt_raw, 0.0) + jnp.log(1.0 + jnp.exp(-jnp.abs(dt_raw)))
    a = dt * (-jnp.exp(alog_ref[...]))
    lane = lax.broadcasted_iota(jnp.int32, (N_DH, tm), 1) & (CHUNK - 1)
    row = lax.broadcasted_iota(jnp.int32, (N_DH, tm), 0)
    cf, cr = a, a
    k = 1
    while k < CHUNK:
        cf = cf + jnp.where(lane >= k, pltpu.roll(cf, k, 1), 0.0)
        cr = cr + jnp.where(lane < CHUNK - k, pltpu.roll(cr, tm - k, 1), 0.0)
        k *= 2
    is_fwd = row < SSD_HEADS
    acum = jnp.where(is_fwd, cf, cr)
    e = jnp.exp(acum)
    w_end = jnp.exp(jnp.where(is_fwd, cr, cf) - a) * dt
    rowpack_ref[...] = jnp.concatenate([acum, dt], axis=0)

    a1 = _bf16_part(acum); a2 = _bf16_part(acum - a1); a3 = _bf16_part(acum - a1 - a2)
    e1 = _bf16_part(e); e2 = _bf16_part(e - e1); e3 = _bf16_part(e - e1 - e2)
    w1 = _bf16_part(w_end); w2 = _bf16_part(w_end - w1)
    top = jnp.concatenate([a1, a2, a3, e1], axis=0)
    bot = jnp.concatenate([e2, e3, w1, w2], axis=0)
    colsplit_ref[...] = jnp.concatenate([top.T, bot.T], axis=1).astype(BF16)


def _inproj_call(x, ln_g, ln_b, sh, sc, wz, wxd, wp, conv_w, conv_b, dt_bias, a_log, *, tm):
    L = x.shape[0]
    nt = L // tm
    hb = tm // HALO
    const = lambda i: (0, 0)
    row = lambda i: (i, 0)
    return pl.pallas_call(
        functools.partial(_inproj_kernel, tm=tm),
        grid=(nt,),
        in_specs=[pl.BlockSpec((tm, D_MODEL), row),
                  pl.BlockSpec((HALO, D_MODEL), lambda i: (jnp.maximum(i * hb - 1, 0), 0)),
                  pl.BlockSpec((HALO, D_MODEL), lambda i: (jnp.minimum((i + 1) * hb, L // HALO - 1), 0)),
                  pl.BlockSpec((1, D_MODEL), const), pl.BlockSpec((1, D_MODEL), const),
                  pl.BlockSpec((1, D_MODEL), const), pl.BlockSpec((1, D_MODEL), const),
                  pl.BlockSpec((D_MODEL, D_SSD), const),
                  pl.BlockSpec((D_MODEL, D_XD), const),
                  pl.BlockSpec((D_MODEL, D_POOL), const),
                  pl.BlockSpec((D_CONV, D_XBC), const), pl.BlockSpec((1, D_XBC), const),
                  pl.BlockSpec((N_DH, 1), const), pl.BlockSpec((N_DH, 1), const)],
        out_specs=[pl.BlockSpec((tm, D_SSD), row),
                   pl.BlockSpec((tm, D_XBC), row),
                   pl.BlockSpec((tm, D_POOL), row),
                   pl.BlockSpec((2 * N_DH, tm), lambda i: (0, i)),
                   pl.BlockSpec((tm, 256), row)],
        out_shape=[jax.ShapeDtypeStruct((L, D_SSD), BF16),
                   jax.ShapeDtypeStruct((L, D_XBC), BF16),
                   jax.ShapeDtypeStruct((L, D_POOL), BF16),
                   jax.ShapeDtypeStruct((2 * N_DH, L), F32),
                   jax.ShapeDtypeStruct((L, 256), BF16)],
        scratch_shapes=[pltpu.VMEM((tm + 2 * HALO, D_XD), F32)],
        compiler_params=pltpu.CompilerParams(dimension_semantics=("arbitrary",),
                                             vmem_limit_bytes=VMEM_LIMIT),
        name="inproj",
    )(x, x, x, ln_g, ln_b, sh, sc, wz, wxd, wp, conv_w, conv_b, dt_bias, a_log)


def _expand_matrices():
    xa = np.zeros((256, N_DH * CHUNK), np.float32)
    for dh in range(N_DH):
        for part in range(3):
            xa[part * N_DH + dh, dh * CHUNK:(dh + 1) * CHUNK] = 1.0
    xew = np.zeros((2, 256, 2 * D_SSD), np.float32)
    for d in range(2):
        for hh in range(SSD_HEADS):
            dh = d * SSD_HEADS + hh
            cols = slice(hh * SSD_HEAD_DIM, (hh + 1) * SSD_HEAD_DIM)
            for part in (3, 4, 5):
                xew[d, part * N_DH + dh, cols] = 1.0
            cols = slice(D_SSD + hh * SSD_HEAD_DIM, D_SSD + (hh + 1) * SSD_HEAD_DIM)
            for part in (6, 7):
                xew[d, part * N_DH + dh, cols] = 1.0
    return jnp.asarray(xa, BF16), jnp.asarray(xew[0], BF16), jnp.asarray(xew[1], BF16)


def _ssd_kernel(xbcf_ref, xbcb_ref, rp_ref, csf_ref, csb_ref, xa_ref, xewf_ref, xewb_ref, dskip_ref,
                h0f_ref, h0b_ref, yf_ref, yb_ref, hf_ref, hb_ref):
    q = CHUNK

    @pl.when(pl.program_id(0) == 0)
    def _():
        hf_ref[...] = h0f_ref[...]
        hb_ref[...] = h0b_ref[...]

    def stream(xbc_ref, cs, xew_ref, h_ref, exit_row):
        xs = xbc_ref[:, 0:D_SSD]
        bm = xbc_ref[:, D_SSD:D_SSD + D_STATE]
        cm = xbc_ref[:, D_SSD + D_STATE:D_XBC]
        ew = _dot(cs, xew_ref[...])
        e, w_end = ew[:, 0:D_SSD], ew[:, D_SSD:]
        st = h_ref[...]
        y_off = _dot(cm, st.astype(BF16)) * e
        xw = (xs.astype(F32) * w_end).astype(BF16)
        upd = lax.dot_general(bm, xw, (((0,), (0,)), ((), ())), preferred_element_type=F32)
        h_ref[...] = st * e[exit_row:exit_row + 1, :] + upd
        return y_off

    csf = csf_ref[...]
    y_off_f = stream(xbcf_ref, csf, xewf_ref, hf_ref, q - 1)
    yb_ref[...] = stream(xbcb_ref, csb_ref[...], xewb_ref, hb_ref, 0).astype(BF16)

    acolx = _dot(csf, xa_ref[...])
    rp = rp_ref[...]
    bm = xbcf_ref[:, D_SSD:D_SSD + D_STATE]
    cm = xbcf_ref[:, D_SSD + D_STATE:D_XBC]
    cb = lax.dot_general(cm, bm, (((1,), (1,)), ((), ())), preferred_element_type=F32)
    ti = lax.broadcasted_iota(jnp.int32, (q, q), 0)
    ui = lax.broadcasted_iota(jnp.int32, (q, q), 1)
    lane = lax.broadcasted_iota(jnp.int32, (q, 2 * SSD_HEAD_DIM), 1)
    for j in range(SSD_HEADS // 2):
        ms = []
        for hh in (2 * j, 2 * j + 1):
            hb_ = SSD_HEADS + hh
            seg_f = acolx[:, hh * q:(hh + 1) * q] - rp[hh:hh + 1, :]
            seg_b = acolx[:, hb_ * q:(hb_ + 1) * q] - rp[hb_:hb_ + 1, :]
            dtf = rp[N_DH + hh:N_DH + hh + 1, :]
            dtb = rp[N_DH + hb_:N_DH + hb_ + 1, :]
            coef = jnp.where(ui < ti, dtf, jnp.where(ui > ti, dtb, dtf + dtb))
            ms.append((cb * coef * jnp.exp(jnp.where(ui <= ti, seg_f, seg_b))).astype(BF16))
        cols = slice(j * 2 * SSD_HEAD_DIM, (j + 1) * 2 * SSD_HEAD_DIM)
        xp = xbcf_ref[:, cols]
        zero = jnp.zeros_like(xp)
        rhs = jnp.concatenate([jnp.where(lane < SSD_HEAD_DIM, xp, zero),
                               jnp.where(lane >= SSD_HEAD_DIM, xp, zero)], axis=0)
        y = _dot(jnp.concatenate(ms, axis=1), rhs) + y_off_f[:, cols] + dskip_ref[:, cols] * xp.astype(F32)
        yf_ref[:, cols] = y.astype(BF16)


def _ssd_call(xbc, rowpack, colsplit, xa, xewf, xewb, dskip, h0f, h0b):
    L = xbc.shape[0]
    nc = L // CHUNK
    const = lambda s: (0, 0)
    fwd = lambda s: (s, 0)
    bwd = lambda s: (nc - 1 - s, 0)
    st_shape = jax.ShapeDtypeStruct((D_STATE, D_SSD), F32)
    return pl.pallas_call(
        _ssd_kernel,
        grid=(nc,),
        in_specs=[pl.BlockSpec((CHUNK, D_XBC), fwd), pl.BlockSpec((CHUNK, D_XBC), bwd),
                  pl.BlockSpec((2 * N_DH, CHUNK), lambda s: (0, s)),
                  pl.BlockSpec((CHUNK, 256), fwd), pl.BlockSpec((CHUNK, 256), bwd),
                  pl.BlockSpec(xa.shape, const), pl.BlockSpec(xewf.shape, const),
                  pl.BlockSpec(xewb.shape, const), pl.BlockSpec((1, D_SSD), const),
                  pl.BlockSpec((D_STATE, D_SSD), const), pl.BlockSpec((D_STATE, D_SSD), const)],
        out_specs=[pl.BlockSpec((CHUNK, D_SSD), fwd), pl.BlockSpec((CHUNK, D_SSD), bwd),
                   pl.BlockSpec((D_STATE, D_SSD), const), pl.BlockSpec((D_STATE, D_SSD), const)],
        out_shape=[jax.ShapeDtypeStruct((L, D_SSD), BF16), jax.ShapeDtypeStruct((L, D_SSD), BF16),
                   st_shape, st_shape],
        compiler_params=pltpu.CompilerParams(dimension_semantics=("arbitrary",),
                                             vmem_limit_bytes=VMEM_LIMIT),
        name="ssd",
    )(xbc, xbc, rowpack, colsplit, colsplit, xa, xewf, xewb, dskip, h0f, h0b)


def _shift_rows(v, k):
    n = v.shape[0]
    return pltpu.roll(v, (-k) % n, 0)


def _pool_group(ext, w, col, row0, rows_total, tm):
    hw = w // 2
    t = ext
    k = 1
    while k < w:
        n = t.shape[0] - k * GRID_W
        t = t[0:n] + t[k * GRID_W:k * GRID_W + n]
        k *= 2
    start = POOL_HALO - hw * GRID_W
    r = t[start:start + tm]
    fwd = r
    k = 1
    while k < hw:
        fwd = fwd + jnp.where(col + k < GRID_W, _shift_rows(fwd, k), 0.0)
        k *= 2
    bwd = jnp.where(col >= 1, _shift_rows(r, -1), 0.0)
    k = 1
    while k < hw:
        bwd = bwd + jnp.where(col >= k, _shift_rows(bwd, -k), 0.0)
        k *= 2
    total = fwd + bwd
    cnt_r = jnp.minimum(row0 + hw, rows_total) - jnp.maximum(row0 - hw, 0)
    cnt_c = jnp.minimum(col + hw, GRID_W) - jnp.maximum(col - hw, 0)
    mean = total / cnt_r.astype(F32) / cnt_c.astype(F32)
    return mean - ext[POOL_HALO:POOL_HALO + tm]


def _merge_kernel(yf_ref, yb_ref, z_ref, up_ref, upp_ref, upn_ref, x_ref, ng_ref, pw_ref, ps_ref, wo_ref,
                  elg_ref, elb_ref, g1_ref, l1g_ref, l1b_ref, out_ref, *, tm, rows_total):
    i = pl.program_id(0)
    n = pl.num_programs(0)
    y = yf_ref[...].astype(F32) + yb_ref[...].astype(F32)
    g = y * _silu(z_ref[...].astype(F32))
    yn = (g * lax.rsqrt(jnp.mean(g * g, axis=-1, keepdims=True) + LN_EPS) * ng_ref[...]).astype(BF16)

    c = POOL_GROUP_DIM
    tok = lax.broadcasted_iota(jnp.int32, (tm, c), 0) + i * tm
    col = tok & (GRID_W - 1)
    row0 = tok // GRID_W
    keep_p = (i > 0).astype(F32)
    keep_n = (i < n - 1).astype(F32)
    ps = []
    for gi, w in enumerate(POOL_WINDOWS):
        cols = slice(gi * c, (gi + 1) * c)
        ext = jnp.concatenate([upp_ref[:, cols].astype(F32) * keep_p,
                               up_ref[:, cols].astype(F32),
                               upn_ref[:, cols].astype(F32) * keep_n], axis=0)
        d = _pool_group(ext, w, col, row0, rows_total, tm)
        ps.append((_dot(d.astype(BF16), pw_ref[gi]) * ps_ref[:, cols]).astype(BF16))
    mix = _dot(jnp.concatenate([yn] + ps, axis=1), wo_ref[...])
    xl = _layer_norm(x_ref[...], elg_ref[...], elb_ref[...])
    out_ref[...] = _layer_norm(DEEPNORM_ALPHA * xl + g1_ref[...] * mix, l1g_ref[...], l1b_ref[...])


def _merge_call(yf, yb, z, up, x, norm_g, pool_w, pool_scale, w_out, elg, elb, g1, l1g, l1b, *, tm):
    L = x.shape[0]
    nt = L // tm
    hb = tm // POOL_HALO
    const = lambda i: (0, 0)
    row = lambda i: (i, 0)
    vec = pl.BlockSpec((1, D_MODEL), const)
    return pl.pallas_call(
        functools.partial(_merge_kernel, tm=tm, rows_total=L // GRID_W),
        grid=(nt,),
        in_specs=[pl.BlockSpec((tm, D_SSD), row), pl.BlockSpec((tm, D_SSD), row),
                  pl.BlockSpec((tm, D_SSD), row),
                  pl.BlockSpec((tm, D_POOL), row),
                  pl.BlockSpec((POOL_HALO, D_POOL), lambda i: (jnp.maximum(i * hb - 1, 0), 0)),
                  pl.BlockSpec((POOL_HALO, D_POOL),
                               lambda i: (jnp.minimum((i + 1) * hb, L // POOL_HALO - 1), 0)),
                  pl.BlockSpec((tm, D_MODEL), row),
                  vec,
                  pl.BlockSpec(pool_w.shape, lambda i: (0, 0, 0)),
                  vec,
                  pl.BlockSpec(w_out.shape, const),
                  vec, vec, vec, vec, vec],
        out_specs=pl.BlockSpec((tm, D_MODEL), row),
        out_shape=jax.ShapeDtypeStruct((L, D_MODEL), F32),
        compiler_params=pltpu.CompilerParams(dimension_semantics=("arbitrary",),
                                             vmem_limit_bytes=VMEM_LIMIT),
        name="merge",
    )(yf, yb, z, up, up, up, x, norm_g, pool_w, pool_scale, w_out, elg, elb, g1, l1g, l1b)


def _ffn_kernel(x_ref, sh_ref, sc_ref, g2_ref, wg_ref, wu_ref, wd_ref, lg_ref, lb_ref, out_ref):
    x = x_ref[...]
    h = (x * (1.0 + sc_ref[...]) + sh_ref[...]).astype(BF16)
    half = D_FF // 2
    ffn = None
    for s in (0, half):
        gate = _dot(h, wg_ref[:, s:s + half])
        upv = _dot(h, wu_ref[:, s:s + half])
        part = _dot((_silu(gate) * upv).astype(BF16), wd_ref[s:s + half, :])
        ffn = part if ffn is None else ffn + part
    out_ref[...] = _layer_norm(DEEPNORM_ALPHA * x + g2_ref[...] * ffn, lg_ref[...], lb_ref[...])


def _ffn_call(x, sh, sc, g2, wg, wu, wd, lg, lb, *, tm):
    L = x.shape[0]
    const = lambda i: (0, 0)
    row = lambda i: (i, 0)
    vec = pl.BlockSpec((1, D_MODEL), const)
    return pl.pallas_call(
        _ffn_kernel,
        grid=(L // tm,),
        in_specs=[pl.BlockSpec((tm, D_MODEL), row), vec, vec, vec,
                  pl.BlockSpec(wg.shape, const, pipeline_mode=pl.Buffered(1)),
                  pl.BlockSpec(wu.shape, const, pipeline_mode=pl.Buffered(1)),
                  pl.BlockSpec(wd.shape, const, pipeline_mode=pl.Buffered(1)),
                  vec, vec],
        out_specs=pl.BlockSpec((tm, D_MODEL), row),
        out_shape=jax.ShapeDtypeStruct((L, D_MODEL), F32),
        compiler_params=pltpu.CompilerParams(dimension_semantics=("arbitrary",),
                                             vmem_limit_bytes=VMEM_LIMIT),
        name="ffn",
    )(x, sh, sc, g2, wg, wu, wd, lg, lb)


def kernel(x, c, ctx, c_ctx, emb_ln_g, emb_ln_b, w_ada, b_ada, in_proj, conv_w, conv_b, dt_bias, a_log,
           d_skip, ssd_norm_g, pool_w, pool_scale, w_out, ln1_g, ln1_b, w_gate, w_up, w_down, ln2_g, ln2_b):
    assert x.shape[0] == 1 and w_ada.shape[0] == DEPTH == 1
    xl, xc = x[0], ctx[0]
    rowv = lambda v: v.reshape(1, -1)
    elg, elb = rowv(emb_ln_g), rowv(emb_ln_b)

    mod = _mod_call(jnp.stack([c[0], c_ctx], axis=1), w_ada[0], rowv(b_ada[0]))
    sh1, sc1, g1, sh2, sc2, g2 = [mod[0:1, k * D_MODEL:(k + 1) * D_MODEL] for k in range(6)]
    sh1c, sc1c = mod[1:2, 0:D_MODEL], mod[1:2, D_MODEL:2 * D_MODEL]

    ip = in_proj[0]
    o1, o2, o3 = D_SSD, D_SSD + D_XBC, D_SSD + D_XBC + N_DH
    wz = ip[:, :o1].astype(BF16)
    wxd = jnp.concatenate([ip[:, o1:o3], jnp.zeros((D_MODEL, D_XD - D_XBC - N_DH), F32)], axis=1).astype(BF16)
    wp = ip[:, o3:].astype(BF16)
    proj_args = (wz, wxd, wp, conv_w[0], rowv(conv_b[0]), dt_bias[0].reshape(N_DH, 1), a_log[0].reshape(N_DH, 1))

    xa, xewf, xewb = _expand_matrices()
    dskip = rowv(jnp.repeat(d_skip[0], SSD_HEAD_DIM))
    h_zero = jnp.zeros((D_STATE, D_SSD), F32)

    _, xbc_c, _, rp_c, cs_c = _inproj_call(xc, elg, elb, sh1c, sc1c, *proj_args, tm=xc.shape[0])
    _, _, hf_ctx, hb_ctx = _ssd_call(xbc_c, rp_c, cs_c, xa, xewf, xewb, dskip, h_zero, h_zero)

    z, xbc, up, rp, cs = _inproj_call(xl, elg, elb, sh1, sc1, *proj_args, tm=512)
    yf, yb, _, _ = _ssd_call(xbc, rp, cs, xa, xewf, xewb, dskip, hf_ctx, hb_ctx)
    x1 = _merge_call(yf, yb, z, up, xl, rowv(ssd_norm_g[0]), pool_w[0].astype(BF16), rowv(pool_scale[0]),
                     w_out[0].astype(BF16), elg, elb, g1, rowv(ln1_g[0]), rowv(ln1_b[0]), tm=512)
    x2 = _ffn_call(x1, sh2, sc2, g2, w_gate[0].astype(BF16), w_up[0].astype(BF16), w_down[0].astype(BF16),
                   rowv(ln2_g[0]), rowv(ln2_b[0]), tm=512)
    return x2[None]
```

```python
import functools

import jax
import jax.numpy as jnp
import numpy as np
from jax import lax
from jax.experimental import pallas as pl
from jax.experimental.pallas import tpu as pltpu

F32 = jnp.float32
BF16 = jnp.bfloat16

D_MODEL = 1024
SSD_HEADS = 16
SSD_HEAD_DIM = 64
D_SSD = SSD_HEADS * SSD_HEAD_DIM
D_STATE = 128
D_CONV = 5
CHUNK = 128
D_POOL = 1024
POOL_WINDOWS = (2, 4, 8, 16)
POOL_GROUP_DIM = D_POOL // len(POOL_WINDOWS)
GRID_W = 64
D_XBC = D_SSD + 2 * D_STATE
D_XD = D_XBC + 128
D_FF = 2816
DEPTH = 1
DEEPNORM_ALPHA = (2 * DEPTH) ** 0.25
LN_EPS = 1e-5

HALO = 16
POOL_HALO = 512
POOL_SUB = 256
SSD_BLOCK = 256
N_DH = 2 * SSD_HEADS
VMEM_LIMIT = 56 * 1024 * 1024


def _dot(a, b):
    return jnp.dot(a, b, preferred_element_type=F32)


def _silu(x):
    hx = 0.5 * x
    return hx + hx * jnp.tanh(hx)


def _layer_norm(x, g, b):
    mu = jnp.mean(x, axis=-1, keepdims=True)
    xc = x - mu
    var = jnp.mean(xc * xc, axis=-1, keepdims=True)
    return xc * lax.rsqrt(var + LN_EPS) * g + b


def _bf16_part(v):
    return v.astype(BF16).astype(F32)


def _mod_kernel(cc_ref, w_ref, b_ref, out_ref):
    s = _silu(cc_ref[...])
    w = w_ref[...]
    r0 = jnp.sum(w * s[:, 0:1], axis=0, keepdims=True) + b_ref[...]
    r1 = jnp.sum(w * s[:, 1:2], axis=0, keepdims=True) + b_ref[...]
    out_ref[...] = jnp.concatenate([r0, r1, jnp.zeros((6, w.shape[1]), F32)], axis=0)


def _mod_call(cc, w_ada, b_ada):
    tn = 512
    n = w_ada.shape[-1]
    return pl.pallas_call(
        _mod_kernel,
        grid=(n // tn,),
        in_specs=[pl.BlockSpec((D_MODEL, 2), lambda j: (0, 0)),
                  pl.BlockSpec((None, D_MODEL, tn), lambda j: (0, 0, j)),
                  pl.BlockSpec((1, tn), lambda j: (0, j))],
        out_specs=pl.BlockSpec((8, tn), lambda j: (0, j)),
        out_shape=jax.ShapeDtypeStruct((8, n), F32),
        name="mod",
    )(cc, w_ada, b_ada)


def _inproj_kernel(x_ref, xp_ref, xn_ref, lng_ref, lnb_ref, sh_ref, sc_ref, wz_ref, wxd_ref, wp_ref,
                   cw_ref, cb_ref, dtb_ref, alog_ref,
                   z_ref, xbc_ref, up_ref, xln_ref, rowpack_ref, colsplit_ref, *, tm):
    i = pl.program_id(0)
    n = pl.num_programs(0)
    lng, lnb, sh, sc = lng_ref[...], lnb_ref[...], sh_ref[...], sc_ref[...]

    def modulated(xn):
        return xn * (1.0 + sc) + sh

    xln = _layer_norm(x_ref[...], lng, lnb)
    xln_ref[...] = xln
    h = modulated(xln).astype(BF16)
    hp = (modulated(_layer_norm(xp_ref[...], lng, lnb)) * (i > 0).astype(F32)).astype(BF16)
    hn = (modulated(_layer_norm(xn_ref[...], lng, lnb)) * (i < n - 1).astype(F32)).astype(BF16)

    ext = _dot(jnp.concatenate([hp, h, hn], axis=0), wxd_ref[...])
    z_ref[...] = _dot(h, wz_ref[...]).astype(BF16)
    up_ref[...] = _dot(h, wp_ref[...]).astype(BF16)

    ne = tm + 2 * HALO
    u = ext[:, 0:D_XBC]
    f = [cw_ref[k:k + 1, :] * u for k in range(D_CONV)]
    after = pltpu.roll(f[3] + pltpu.roll(f[4], ne - 1, 0), ne - 1, 0)
    before = pltpu.roll(f[1] + pltpu.roll(f[0], 1, 0), 1, 0)
    acc = (f[2] + after + before)[HALO:HALO + tm] + cb_ref[...]
    xbc_ref[...] = _silu(acc).astype(BF16)

    dt_raw = ext[HALO:HALO + tm, D_XBC:D_XD].T[0:N_DH, :] + dtb_ref[...]
    dt = jnp.maximum(dt_raw, 0.0) + jnp.log(1.0 + jnp.exp(-jnp.abs(dt_raw)))
    a = dt * (-jnp.exp(alog_ref[...]))
    lane = lax.broadcasted_iota(jnp.int32, (N_DH, tm), 1) & (CHUNK - 1)
    row = lax.broadcasted_iota(jnp.int32, (N_DH, tm), 0)
    cf, cr = a, a
    k = 1
    while k < CHUNK:
        cf = cf + jnp.where(lane >= k, pltpu.roll(cf, k, 1), 0.0)
        cr = cr + jnp.where(lane < CHUNK - k, pltpu.roll(cr, tm - k, 1), 0.0)
        k *= 2
    is_fwd = row < SSD_HEADS
    acum = jnp.where(is_fwd, cf, cr)
    e = jnp.exp(acum)
    w_end = jnp.exp(jnp.where(is_fwd, cr, cf) - a) * dt
    rowpack_ref[...] = jnp.concatenate([acum, dt], axis=0)

    a1 = _bf16_part(acum); a2 = _bf16_part(acum - a1); a3 = _bf16_part(acum - a1 - a2)
    e1 = _bf16_part(e); e2 = _bf16_part(e - e1); e3 = _bf16_part(e - e1 - e2)
    w1 = _bf16_part(w_end); w2 = _bf16_part(w_end - w1)
    top = jnp.concatenate([a1, a2, a3, e1], axis=0)
    bot = jnp.concatenate([e2, e3, w1, w2], axis=0)
    colsplit_ref[...] = jnp.concatenate([top.T, bot.T], axis=1).astype(BF16)


def _inproj_call(x, ln_g, ln_b, sh, sc, wz, wxd, wp, conv_w, conv_b, dt_bias, a_log, *, tm):
    L = x.shape[0]
    nt = L // tm
    hb = tm // HALO
    const = lambda i: (0, 0)
    row = lambda i: (i, 0)
    return pl.pallas_call(
        functools.partial(_inproj_kernel, tm=tm),
        grid=(nt,),
        in_specs=[pl.BlockSpec((tm, D_MODEL), row),
                  pl.BlockSpec((HALO, D_MODEL), lambda i: (jnp.maximum(i * hb - 1, 0), 0)),
                  pl.BlockSpec((HALO, D_MODEL), lambda i: (jnp.minimum((i + 1) * hb, L // HALO - 1), 0)),
                  pl.BlockSpec((1, D_MODEL), const), pl.BlockSpec((1, D_MODEL), const),
                  pl.BlockSpec((1, D_MODEL), const), pl.BlockSpec((1, D_MODEL), const),
                  pl.BlockSpec((D_MODEL, D_SSD), const),
                  pl.BlockSpec((D_MODEL, D_XD), const),
                  pl.BlockSpec((D_MODEL, D_POOL), const),
                  pl.BlockSpec((D_CONV, D_XBC), const), pl.BlockSpec((1, D_XBC), const),
                  pl.BlockSpec((N_DH, 1), const), pl.BlockSpec((N_DH, 1), const)],
        out_specs=[pl.BlockSpec((tm, D_SSD), row),
                   pl.BlockSpec((tm, D_XBC), row),
                   pl.BlockSpec((tm, D_POOL), row),
                   pl.BlockSpec((tm, D_MODEL), row),
                   pl.BlockSpec((2 * N_DH, tm), lambda i: (0, i)),
                   pl.BlockSpec((tm, 256), row)],
        out_shape=[jax.ShapeDtypeStruct((L, D_SSD), BF16),
                   jax.ShapeDtypeStruct((L, D_XBC), BF16),
                   jax.ShapeDtypeStruct((L, D_POOL), BF16),
                   jax.ShapeDtypeStruct((L, D_MODEL), F32),
                   jax.ShapeDtypeStruct((2 * N_DH, L), F32),
                   jax.ShapeDtypeStruct((L, 256), BF16)],
        compiler_params=pltpu.CompilerParams(dimension_semantics=("arbitrary",),
                                             vmem_limit_bytes=VMEM_LIMIT),
        name="inproj",
    )(x, x, x, ln_g, ln_b, sh, sc, wz, wxd, wp, conv_w, conv_b, dt_bias, a_log)


def _expand_matrices():
    xa = np.zeros((256, N_DH * CHUNK), np.float32)
    for dh in range(N_DH):
        for part in range(3):
            xa[part * N_DH + dh, dh * CHUNK:(dh + 1) * CHUNK] = 1.0
    xew = np.zeros((2, 256, 2 * D_SSD), np.float32)
    for d in range(2):
        for hh in range(SSD_HEADS):
            dh = d * SSD_HEADS + hh
            cols = slice(hh * SSD_HEAD_DIM, (hh + 1) * SSD_HEAD_DIM)
            for part in (3, 4, 5):
                xew[d, part * N_DH + dh, cols] = 1.0
            cols = slice(D_SSD + hh * SSD_HEAD_DIM, D_SSD + (hh + 1) * SSD_HEAD_DIM)
            for part in (6, 7):
                xew[d, part * N_DH + dh, cols] = 1.0
    return jnp.asarray(xa, BF16), jnp.asarray(xew[0], BF16), jnp.asarray(xew[1], BF16)


def _ssd_kernel(xbcf_ref, xbcb_ref, rp_ref, csf_ref, csb_ref, xa_ref, xewf_ref, xewb_ref, dskip_ref,
                h0f_ref, h0b_ref, yf_ref, yb_ref, hf_ref, hb_ref):
    q = CHUNK
    nsub = SSD_BLOCK // CHUNK

    @pl.when(pl.program_id(0) == 0)
    def _():
        hf_ref[...] = h0f_ref[...]
        hb_ref[...] = h0b_ref[...]

    def stream(xbc_ref, cs, rows, xew_ref, h_ref, exit_row):
        xs = xbc_ref[rows, 0:D_SSD]
        bm = xbc_ref[rows, D_SSD:D_SSD + D_STATE]
        cm = xbc_ref[rows, D_SSD + D_STATE:D_XBC]
        ew = _dot(cs, xew_ref[...])
        e, w_end = ew[:, 0:D_SSD], ew[:, D_SSD:]
        st = h_ref[...]
        y_off = _dot(cm, st.astype(BF16)) * e
        xw = (xs.astype(F32) * w_end).astype(BF16)
        upd = lax.dot_general(bm, xw, (((0,), (0,)), ((), ())), preferred_element_type=F32)
        h_ref[...] = st * e[exit_row:exit_row + 1, :] + upd
        return y_off

    ti = lax.broadcasted_iota(jnp.int32, (q, q), 0)
    ui = lax.broadcasted_iota(jnp.int32, (q, q), 1)
    lane = lax.broadcasted_iota(jnp.int32, (q, 2 * SSD_HEAD_DIM), 1)
    for sub in range(nsub):
        fr = slice(sub * q, (sub + 1) * q)
        br = slice((nsub - 1 - sub) * q, (nsub - sub) * q)
        csf = csf_ref[fr, :]
        y_off_f = stream(xbcf_ref, csf, fr, xewf_ref, hf_ref, q - 1)
        yb_ref[br, :] = stream(xbcb_ref, csb_ref[br, :], br, xewb_ref, hb_ref, 0).astype(BF16)

        acolx = _dot(csf, xa_ref[...])
        rp = rp_ref[:, fr]
        bm = xbcf_ref[fr, D_SSD:D_SSD + D_STATE]
        cm = xbcf_ref[fr, D_SSD + D_STATE:D_XBC]
        cb = lax.dot_general(cm, bm, (((1,), (1,)), ((), ())), preferred_element_type=F32)
        for j in range(SSD_HEADS // 2):
            ms = []
            for hh in (2 * j, 2 * j + 1):
                hb_ = SSD_HEADS + hh
                seg_f = acolx[:, hh * q:(hh + 1) * q] - rp[hh:hh + 1, :]
                seg_b = acolx[:, hb_ * q:(hb_ + 1) * q] - rp[hb_:hb_ + 1, :]
                dtf = rp[N_DH + hh:N_DH + hh + 1, :]
                dtb = rp[N_DH + hb_:N_DH + hb_ + 1, :]
                coef = jnp.where(ui < ti, dtf, jnp.where(ui > ti, dtb, dtf + dtb))
                ms.append((cb * coef * jnp.exp(jnp.where(ui <= ti, seg_f, seg_b))).astype(BF16))
            cols = slice(j * 2 * SSD_HEAD_DIM, (j + 1) * 2 * SSD_HEAD_DIM)
            xp = xbcf_ref[fr, cols]
            zero = jnp.zeros_like(xp)
            rhs = jnp.concatenate([jnp.where(lane < SSD_HEAD_DIM, xp, zero),
                                   jnp.where(lane >= SSD_HEAD_DIM, xp, zero)], axis=0)
            y = (_dot(jnp.concatenate(ms, axis=1), rhs) + y_off_f[:, cols]
                 + dskip_ref[:, cols] * xp.astype(F32))
            yf_ref[fr, cols] = y.astype(BF16)


def _ssd_call(xbc, rowpack, colsplit, xa, xewf, xewb, dskip, h0f, h0b):
    L = xbc.shape[0]
    nc = L // SSD_BLOCK
    const = lambda s: (0, 0)
    fwd = lambda s: (s, 0)
    bwd = lambda s: (nc - 1 - s, 0)
    st_shape = jax.ShapeDtypeStruct((D_STATE, D_SSD), F32)
    return pl.pallas_call(
        _ssd_kernel,
        grid=(nc,),
        in_specs=[pl.BlockSpec((SSD_BLOCK, D_XBC), fwd), pl.BlockSpec((SSD_BLOCK, D_XBC), bwd),
                  pl.BlockSpec((2 * N_DH, SSD_BLOCK), lambda s: (0, s)),
                  pl.BlockSpec((SSD_BLOCK, 256), fwd), pl.BlockSpec((SSD_BLOCK, 256), bwd),
                  pl.BlockSpec(xa.shape, const), pl.BlockSpec(xewf.shape, const),
                  pl.BlockSpec(xewb.shape, const), pl.BlockSpec((1, D_SSD), const),
                  pl.BlockSpec((D_STATE, D_SSD), const), pl.BlockSpec((D_STATE, D_SSD), const)],
        out_specs=[pl.BlockSpec((SSD_BLOCK, D_SSD), fwd), pl.BlockSpec((SSD_BLOCK, D_SSD), bwd),
                   pl.BlockSpec((D_STATE, D_SSD), const), pl.BlockSpec((D_STATE, D_SSD), const)],
        out_shape=[jax.ShapeDtypeStruct((L, D_SSD), BF16), jax.ShapeDtypeStruct((L, D_SSD), BF16),
                   st_shape, st_shape],
        compiler_params=pltpu.CompilerParams(dimension_semantics=("arbitrary",),
                                             vmem_limit_bytes=VMEM_LIMIT),
        name="ssd",
    )(xbc, xbc, rowpack, colsplit, colsplit, xa, xewf, xewb, dskip, h0f, h0b)


def _pool_constants():
    bands, inv_cols = [], []
    t = np.arange(POOL_SUB)
    rt, ct = t // GRID_W, t % GRID_W
    for w in POOL_WINDOWS:
        hw = w // 2
        k = np.arange(POOL_SUB + GRID_W * w)
        rk, ck = k // GRID_W, k % GRID_W
        band = ((rk[None, :] >= rt[:, None]) & (rk[None, :] < rt[:, None] + w)
                & (ck[None, :] >= ct[:, None] - hw) & (ck[None, :] < ct[:, None] + hw))
        bands.append(jnp.asarray(band, BF16))
        cnt_c = np.minimum(ct + hw, GRID_W) - np.maximum(ct - hw, 0)
        inv_cols.append(np.broadcast_to((1.0 / cnt_c)[:, None], (POOL_SUB, 128)))
    return bands, jnp.asarray(np.stack(inv_cols), F32)


def _merge_kernel(yf_ref, yb_ref, z_ref, up_ref, upp_ref, upn_ref, xln_ref, ng_ref, pw_ref, ps_ref, wo_ref,
                  g1_ref, l1g_ref, l1b_ref, band0_ref, band1_ref, band2_ref, band3_ref, invc_ref,
                  out_ref, *, tm, rows_total):
    i = pl.program_id(0)
    n = pl.num_programs(0)
    y = yf_ref[...].astype(F32) + yb_ref[...].astype(F32)
    g = y * _silu(z_ref[...].astype(F32))
    yn = (g * lax.rsqrt(jnp.mean(g * g, axis=-1, keepdims=True) + LN_EPS) * ng_ref[...]).astype(BF16)

    c = POOL_GROUP_DIM
    band_refs = (band0_ref, band1_ref, band2_ref, band3_ref)
    keep_p = (i > 0).astype(BF16)
    keep_n = (i < n - 1).astype(BF16)
    sub_row = lax.broadcasted_iota(jnp.int32, (POOL_SUB, 128), 0) // GRID_W
    nsub = tm // POOL_SUB
    pooled = [[None] * len(POOL_WINDOWS) for _ in range(nsub)]
    for gi, w in enumerate(POOL_WINDOWS):
        hw = w // 2
        cols = slice(gi * c, (gi + 1) * c)
        ext = jnp.concatenate([upp_ref[:, cols] * keep_p, up_ref[:, cols], upn_ref[:, cols] * keep_n], axis=0)
        for b in range(nsub):
            start = POOL_HALO + b * POOL_SUB - hw * GRID_W
            sums = _dot(band_refs[gi][...], ext[start:start + POOL_SUB + GRID_W * w])
            row = sub_row + (i * tm + b * POOL_SUB) // GRID_W
            cnt_r = jnp.minimum(row + hw, rows_total) - jnp.maximum(row - hw, 0)
            inv = invc_ref[gi] / cnt_r.astype(F32)
            d = sums * jnp.concatenate([inv, inv], axis=1) - up_ref[b * POOL_SUB:(b + 1) * POOL_SUB, cols].astype(F32)
            pooled[b][gi] = (_dot(d.astype(BF16), pw_ref[gi]) * ps_ref[:, cols]).astype(BF16)
    p = jnp.concatenate([jnp.concatenate(pb, axis=1) for pb in pooled], axis=0)
    mix = _dot(jnp.concatenate([yn, p], axis=1), wo_ref[...])
    out_ref[...] = _layer_norm(DEEPNORM_ALPHA * xln_ref[...] + g1_ref[...] * mix, l1g_ref[...], l1b_ref[...])


def _merge_call(yf, yb, z, up, xln, norm_g, pool_w, pool_scale, w_out, g1, l1g, l1b, *, tm):
    L = xln.shape[0]
    nt = L // tm
    hb = tm // POOL_HALO
    const = lambda i: (0, 0)
    const3 = lambda i: (0, 0, 0)
    row = lambda i: (i, 0)
    vec = pl.BlockSpec((1, D_MODEL), const)
    bands, inv_cols = _pool_constants()
    return pl.pallas_call(
        functools.partial(_merge_kernel, tm=tm, rows_total=L // GRID_W),
        grid=(nt,),
        in_specs=[pl.BlockSpec((tm, D_SSD), row), pl.BlockSpec((tm, D_SSD), row),
                  pl.BlockSpec((tm, D_SSD), row),
                  pl.BlockSpec((tm, D_POOL), row),
                  pl.BlockSpec((POOL_HALO, D_POOL), lambda i: (jnp.maximum(i * hb - 1, 0), 0)),
                  pl.BlockSpec((POOL_HALO, D_POOL),
                               lambda i: (jnp.minimum((i + 1) * hb, L // POOL_HALO - 1), 0)),
                  pl.BlockSpec((tm, D_MODEL), row),
                  vec,
                  pl.BlockSpec(pool_w.shape, const3),
                  vec,
                  pl.BlockSpec(w_out.shape, const),
                  vec, vec, vec]
                 + [pl.BlockSpec(bm.shape, const) for bm in bands]
                 + [pl.BlockSpec(inv_cols.shape, const3)],
        out_specs=pl.BlockSpec((tm, D_MODEL), row),
        out_shape=jax.ShapeDtypeStruct((L, D_MODEL), F32),
        compiler_params=pltpu.CompilerParams(dimension_semantics=("arbitrary",),
                                             vmem_limit_bytes=VMEM_LIMIT),
        name="merge",
    )(yf, yb, z, up, up, up, xln, norm_g, pool_w, pool_scale, w_out, g1, l1g, l1b, *bands, inv_cols)


def _ffn_kernel(x_ref, sh_ref, sc_ref, g2_ref, wg_ref, wu_ref, wd_ref, lg_ref, lb_ref, out_ref):
    x = x_ref[...]
    h = (x * (1.0 + sc_ref[...]) + sh_ref[...]).astype(BF16)
    half = D_FF // 2
    ffn = None
    for s in (0, half):
        gate = _dot(h, wg_ref[:, s:s + half])
        upv = _dot(h, wu_ref[:, s:s + half])
        part = _dot((_silu(gate) * upv).astype(BF16), wd_ref[s:s + half, :])
        ffn = part if ffn is None else ffn + part
    out_ref[...] = _layer_norm(DEEPNORM_ALPHA * x + g2_ref[...] * ffn, lg_ref[...], lb_ref[...])


def _ffn_call(x, sh, sc, g2, wg, wu, wd, lg, lb, *, tm):
    L = x.shape[0]
    const = lambda i: (0, 0)
    row = lambda i: (i, 0)
    vec = pl.BlockSpec((1, D_MODEL), const)
    return pl.pallas_call(
        _ffn_kernel,
        grid=(L // tm,),
        in_specs=[pl.BlockSpec((tm, D_MODEL), row), vec, vec, vec,
                  pl.BlockSpec(wg.shape, const, pipeline_mode=pl.Buffered(1)),
                  pl.BlockSpec(wu.shape, const, pipeline_mode=pl.Buffered(1)),
                  pl.BlockSpec(wd.shape, const, pipeline_mode=pl.Buffered(1)),
                  vec, vec],
        out_specs=pl.BlockSpec((tm, D_MODEL), row),
        out_shape=jax.ShapeDtypeStruct((L, D_MODEL), F32),
        compiler_params=pltpu.CompilerParams(dimension_semantics=("arbitrary",),
                                             vmem_limit_bytes=VMEM_LIMIT),
        name="ffn",
    )(x, sh, sc, g2, wg, wu, wd, lg, lb)


def kernel(x, c, ctx, c_ctx, emb_ln_g, emb_ln_b, w_ada, b_ada, in_proj, conv_w, conv_b, dt_bias, a_log,
           d_skip, ssd_norm_g, pool_w, pool_scale, w_out, ln1_g, ln1_b, w_gate, w_up, w_down, ln2_g, ln2_b):
    assert x.shape[0] == 1 and w_ada.shape[0] == DEPTH == 1
    xl, xc = x[0], ctx[0]
    rowv = lambda v: v.reshape(1, -1)
    elg, elb = rowv(emb_ln_g), rowv(emb_ln_b)

    mod = _mod_call(jnp.stack([c[0], c_ctx], axis=1), w_ada, rowv(b_ada[0]))
    sh1, sc1, g1, sh2, sc2, g2 = [mod[0:1, k * D_MODEL:(k + 1) * D_MODEL] for k in range(6)]
    sh1c, sc1c = mod[1:2, 0:D_MODEL], mod[1:2, D_MODEL:2 * D_MODEL]

    ip = in_proj[0]
    o1, o2, o3 = D_SSD, D_SSD + D_XBC, D_SSD + D_XBC + N_DH
    wz = ip[:, :o1].astype(BF16)
    wxd = jnp.concatenate([ip[:, o1:o3], jnp.zeros((D_MODEL, D_XD - D_XBC - N_DH), F32)], axis=1).astype(BF16)
    wp = ip[:, o3:].astype(BF16)
    proj_args = (wz, wxd, wp, conv_w[0], rowv(conv_b[0]), dt_bias[0].reshape(N_DH, 1), a_log[0].reshape(N_DH, 1))

    xa, xewf, xewb = _expand_matrices()
    dskip = rowv(jnp.repeat(d_skip[0], SSD_HEAD_DIM))
    h_zero = jnp.zeros((D_STATE, D_SSD), F32)

    _, xbc_c, _, _, rp_c, cs_c = _inproj_call(xc, elg, elb, sh1c, sc1c, *proj_args, tm=xc.shape[0])
    _, _, hf_ctx, hb_ctx = _ssd_call(xbc_c, rp_c, cs_c, xa, xewf, xewb, dskip, h_zero, h_zero)

    z, xbc, up, xln, rp, cs = _inproj_call(xl, elg, elb, sh1, sc1, *proj_args, tm=512)
    yf, yb, _, _ = _ssd_call(xbc, rp, cs, xa, xewf, xewb, dskip, hf_ctx, hb_ctx)
    x1 = _merge_call(yf, yb, z, up, xln, rowv(ssd_norm_g[0]), pool_w[0].astype(BF16), rowv(pool_scale[0]),
                     w_out[0].astype(BF16), g1, rowv(ln1_g[0]), rowv(ln1_b[0]), tm=512)
    x2 = _ffn_call(x1, sh2, sc2, g2, w_gate[0].astype(BF16), w_up[0].astype(BF16), w_down[0].astype(BF16),
                   rowv(ln2_g[0]), rowv(ln2_b[0]), tm=512)
    return x2[None]
```

```python
import functools

import jax
import jax.numpy as jnp
import numpy as np
from jax import lax
from jax.experimental import pallas as pl
from jax.experimental.pallas import tpu as pltpu

F32 = jnp.float32
BF16 = jnp.bfloat16

D_MODEL = 1024
SSD_HEADS = 16
SSD_HEAD_DIM = 64
D_SSD = SSD_HEADS * SSD_HEAD_DIM
D_STATE = 128
D_CONV = 5
CHUNK = 128
D_POOL = 1024
POOL_WINDOWS = (2, 4, 8, 16)
POOL_GROUP_DIM = D_POOL // len(POOL_WINDOWS)
GRID_W = 64
D_XBC = D_SSD + 2 * D_STATE
D_XD = D_XBC + 128
D_FF = 2816
DEPTH = 1
DEEPNORM_ALPHA = (2 * DEPTH) ** 0.25
LN_EPS = 1e-5

HALO = 16
POOL_HALO = 512
POOL_SUB = 256
SSD_BLOCK = 256
PROJ_BLOCK = 256
N_DH = 2 * SSD_HEADS
VMEM_LIMIT = 56 * 1024 * 1024


def _dot(a, b):
    return jnp.dot(a, b, preferred_element_type=F32)


def _silu(x):
    hx = 0.5 * x
    return hx + hx * jnp.tanh(hx)


def _layer_norm(x, g, b):
    mu = jnp.mean(x, axis=-1, keepdims=True)
    xc = x - mu
    var = jnp.mean(xc * xc, axis=-1, keepdims=True)
    return xc * lax.rsqrt(var + LN_EPS) * g + b


def _bf16_part(v):
    return v.astype(BF16).astype(F32)


def _mod_kernel(cc_ref, w_ref, b_ref, out_ref):
    s = _silu(cc_ref[...])
    w = w_ref[...]
    r0 = jnp.sum(w * s[:, 0:1], axis=0, keepdims=True) + b_ref[...]
    r1 = jnp.sum(w * s[:, 1:2], axis=0, keepdims=True) + b_ref[...]
    out_ref[...] = jnp.concatenate([r0, r1, jnp.zeros((6, w.shape[1]), F32)], axis=0)


def _mod_call(cc, w_ada, b_ada):
    tn = 512
    n = w_ada.shape[-1]
    return pl.pallas_call(
        _mod_kernel,
        grid=(n // tn,),
        in_specs=[pl.BlockSpec((D_MODEL, 2), lambda j: (0, 0)),
                  pl.BlockSpec((None, D_MODEL, tn), lambda j: (0, 0, j)),
                  pl.BlockSpec((1, tn), lambda j: (0, j))],
        out_specs=pl.BlockSpec((8, tn), lambda j: (0, j)),
        out_shape=jax.ShapeDtypeStruct((8, n), F32),
        name="mod",
    )(cc, w_ada, b_ada)


def _inproj_kernel(x_ref, xp_ref, xn_ref, lng_ref, lnb_ref, sh_ref, sc_ref, wz_ref, wxd_ref, wp_ref,
                   cw_ref, cb_ref, dtb_ref, alog_ref,
                   z_ref, xbc_ref, up_ref, xln_ref, rowpack_ref, colsplit_ref, h_scr, *, tm):
    i = pl.program_id(0)
    n = pl.num_programs(0)
    lng, lnb, sh, sc = lng_ref[...], lnb_ref[...], sh_ref[...], sc_ref[...]

    def modulated(xn):
        return xn * (1.0 + sc) + sh

    xln = _layer_norm(x_ref[...], lng, lnb)
    xln_ref[...] = xln
    h = modulated(xln).astype(BF16)
    hp = (modulated(_layer_norm(xp_ref[...], lng, lnb)) * (i > 0).astype(F32)).astype(BF16)
    hn = (modulated(_layer_norm(xn_ref[...], lng, lnb)) * (i < n - 1).astype(F32)).astype(BF16)

    h_scr[0:HALO, :] = hp
    h_scr[HALO:HALO + tm, :] = h
    h_scr[HALO + tm:HALO + tm + HALO, :] = hn
    dt_raw = _dot(h_scr[HALO:HALO + tm, :], wxd_ref[:, D_XBC:D_XD]).T[0:N_DH, :] + dtb_ref[...]
    dt = jnp.maximum(dt_raw, 0.0) + jnp.log(1.0 + jnp.exp(-jnp.abs(dt_raw)))
    a = dt * (-jnp.exp(alog_ref[...]))
    lane = lax.broadcasted_iota(jnp.int32, (N_DH, tm), 1) & (CHUNK - 1)
    row = lax.broadcasted_iota(jnp.int32, (N_DH, tm), 0)
    cf, cr = a, a
    k = 1
    while k < CHUNK:
        cf = cf + jnp.where(lane >= k, pltpu.roll(cf, k, 1), 0.0)
        cr = cr + jnp.where(lane < CHUNK - k, pltpu.roll(cr, tm - k, 1), 0.0)
        k *= 2
    is_fwd = row < SSD_HEADS
    acum = jnp.where(is_fwd, cf, cr)
    e = jnp.exp(acum)
    w_end = jnp.exp(jnp.where(is_fwd, cr, cf) - a) * dt
    rowpack_ref[...] = jnp.concatenate([acum, dt], axis=0)

    a1 = _bf16_part(acum); a2 = _bf16_part(acum - a1); a3 = _bf16_part(acum - a1 - a2)
    e1 = _bf16_part(e); e2 = _bf16_part(e - e1); e3 = _bf16_part(e - e1 - e2)
    w1 = _bf16_part(w_end); w2 = _bf16_part(w_end - w1)
    top = jnp.concatenate([a1, a2, a3, e1], axis=0)
    bot = jnp.concatenate([e2, e3, w1, w2], axis=0)
    colsplit_ref[...] = jnp.concatenate([top.T, bot.T], axis=1).astype(BF16)

    ne = tm + 2 * HALO
    nb = PROJ_BLOCK
    zu_blocks = [(w_ref, o_ref, c) for w_ref, o_ref in ((wz_ref, z_ref), (wp_ref, up_ref))
                 for c in range(0, D_SSD, nb)]
    for j, c0 in enumerate(range(0, D_XBC, nb)):
        cols = slice(c0, c0 + nb)
        u = _dot(h_scr[...], wxd_ref[:, cols])
        take = -(-len(zu_blocks) // (D_XBC // nb - j))
        for w_ref, o_ref, c in zu_blocks[:take]:
            o_ref[:, c:c + nb] = _dot(h_scr[HALO:HALO + tm, :], w_ref[:, c:c + nb]).astype(BF16)
        zu_blocks = zu_blocks[take:]
        f = [cw_ref[k:k + 1, cols] * u for k in range(D_CONV)]
        after = pltpu.roll(f[3] + pltpu.roll(f[4], ne - 1, 0), ne - 1, 0)
        before = pltpu.roll(f[1] + pltpu.roll(f[0], 1, 0), 1, 0)
        acc = (f[2] + after + before)[HALO:HALO + tm] + cb_ref[:, cols]
        xbc_ref[:, cols] = _silu(acc).astype(BF16)


def _inproj_call(x, ln_g, ln_b, sh, sc, wz, wxd, wp, conv_w, conv_b, dt_bias, a_log, *, tm):
    L = x.shape[0]
    nt = L // tm
    hb = tm // HALO
    const = lambda i: (0, 0)
    row = lambda i: (i, 0)
    return pl.pallas_call(
        functools.partial(_inproj_kernel, tm=tm),
        grid=(nt,),
        in_specs=[pl.BlockSpec((tm, D_MODEL), row),
                  pl.BlockSpec((HALO, D_MODEL), lambda i: (jnp.maximum(i * hb - 1, 0), 0)),
                  pl.BlockSpec((HALO, D_MODEL), lambda i: (jnp.minimum((i + 1) * hb, L // HALO - 1), 0)),
                  pl.BlockSpec((1, D_MODEL), const), pl.BlockSpec((1, D_MODEL), const),
                  pl.BlockSpec((1, D_MODEL), const), pl.BlockSpec((1, D_MODEL), const),
                  pl.BlockSpec((D_MODEL, D_SSD), const),
                  pl.BlockSpec((D_MODEL, D_XD), const),
                  pl.BlockSpec((D_MODEL, D_POOL), const),
                  pl.BlockSpec((D_CONV, D_XBC), const), pl.BlockSpec((1, D_XBC), const),
                  pl.BlockSpec((N_DH, 1), const), pl.BlockSpec((N_DH, 1), const)],
        out_specs=[pl.BlockSpec((tm, D_SSD), row),
                   pl.BlockSpec((tm, D_XBC), row),
                   pl.BlockSpec((tm, D_POOL), row),
                   pl.BlockSpec((tm, D_MODEL), row),
                   pl.BlockSpec((2 * N_DH, tm), lambda i: (0, i)),
                   pl.BlockSpec((tm, 256), row)],
        out_shape=[jax.ShapeDtypeStruct((L, D_SSD), BF16),
                   jax.ShapeDtypeStruct((L, D_XBC), BF16),
                   jax.ShapeDtypeStruct((L, D_POOL), BF16),
                   jax.ShapeDtypeStruct((L, D_MODEL), F32),
                   jax.ShapeDtypeStruct((2 * N_DH, L), F32),
                   jax.ShapeDtypeStruct((L, 256), BF16)],
        scratch_shapes=[pltpu.VMEM((tm + 2 * HALO, D_MODEL), BF16)],
        compiler_params=pltpu.CompilerParams(dimension_semantics=("arbitrary",),
                                             vmem_limit_bytes=VMEM_LIMIT),
        name="inproj",
    )(x, x, x, ln_g, ln_b, sh, sc, wz, wxd, wp, conv_w, conv_b, dt_bias, a_log)


def _expand_matrices():
    xa = np.zeros((256, N_DH * CHUNK), np.float32)
    for dh in range(N_DH):
        for part in range(3):
            xa[part * N_DH + dh, dh * CHUNK:(dh + 1) * CHUNK] = 1.0
    xew = np.zeros((2, 256, 2 * D_SSD), np.float32)
    for d in range(2):
        for hh in range(SSD_HEADS):
            dh = d * SSD_HEADS + hh
            cols = slice(hh * SSD_HEAD_DIM, (hh + 1) * SSD_HEAD_DIM)
            for part in (3, 4, 5):
                xew[d, part * N_DH + dh, cols] = 1.0
            cols = slice(D_SSD + hh * SSD_HEAD_DIM, D_SSD + (hh + 1) * SSD_HEAD_DIM)
            for part in (6, 7):
                xew[d, part * N_DH + dh, cols] = 1.0
    return jnp.asarray(xa, BF16), jnp.asarray(xew[0], BF16), jnp.asarray(xew[1], BF16)


def _ssd_kernel(xbcf_ref, xbcb_ref, rp_ref, csf_ref, csb_ref, xa_ref, xewf_ref, xewb_ref, dskip_ref,
                h0f_ref, h0b_ref, yf_ref, yb_ref, hf_ref, hb_ref):
    q = CHUNK
    nsub = SSD_BLOCK // CHUNK

    @pl.when(pl.program_id(0) == 0)
    def _():
        hf_ref[...] = h0f_ref[...]
        hb_ref[...] = h0b_ref[...]

    def stream(xbc_ref, cs, rows, xew_ref, h_ref, exit_row):
        xs = xbc_ref[rows, 0:D_SSD]
        bm = xbc_ref[rows, D_SSD:D_SSD + D_STATE]
        cm = xbc_ref[rows, D_SSD + D_STATE:D_XBC]
        ew = _dot(cs, xew_ref[...])
        e, w_end = ew[:, 0:D_SSD], ew[:, D_SSD:]
        st = h_ref[...]
        y_off = _dot(cm, st.astype(BF16)) * e
        xw = (xs.astype(F32) * w_end).astype(BF16)
        upd = lax.dot_general(bm, xw, (((0,), (0,)), ((), ())), preferred_element_type=F32)
        h_ref[...] = st * e[exit_row:exit_row + 1, :] + upd
        return y_off

    ti = lax.broadcasted_iota(jnp.int32, (q, q), 0)
    ui = lax.broadcasted_iota(jnp.int32, (q, q), 1)
    lane = lax.broadcasted_iota(jnp.int32, (q, 2 * SSD_HEAD_DIM), 1)
    for sub in range(nsub):
        fr = slice(sub * q, (sub + 1) * q)
        br = slice((nsub - 1 - sub) * q, (nsub - sub) * q)
        csf = csf_ref[fr, :]
        y_off_f = stream(xbcf_ref, csf, fr, xewf_ref, hf_ref, q - 1)
        yb_ref[br, :] = stream(xbcb_ref, csb_ref[br, :], br, xewb_ref, hb_ref, 0).astype(BF16)

        acolx = _dot(csf, xa_ref[...])
        rp = rp_ref[:, fr]
        bm = xbcf_ref[fr, D_SSD:D_SSD + D_STATE]
        cm = xbcf_ref[fr, D_SSD + D_STATE:D_XBC]
        cb = lax.dot_general(cm, bm, (((1,), (1,)), ((), ())), preferred_element_type=F32)
        for j in range(SSD_HEADS // 2):
            ms = []
            for hh in (2 * j, 2 * j + 1):
                hb_ = SSD_HEADS + hh
                seg_f = acolx[:, hh * q:(hh + 1) * q] - rp[hh:hh + 1, :]
                seg_b = acolx[:, hb_ * q:(hb_ + 1) * q] - rp[hb_:hb_ + 1, :]
                dtf = rp[N_DH + hh:N_DH + hh + 1, :]
                dtb = rp[N_DH + hb_:N_DH + hb_ + 1, :]
                coef = jnp.where(ui < ti, dtf, jnp.where(ui > ti, dtb, dtf + dtb))
                ms.append((cb * coef * jnp.exp(jnp.where(ui <= ti, seg_f, seg_b))).astype(BF16))
            cols = slice(j * 2 * SSD_HEAD_DIM, (j + 1) * 2 * SSD_HEAD_DIM)
            xp = xbcf_ref[fr, cols]
            zero = jnp.zeros_like(xp)
            rhs = jnp.concatenate([jnp.where(lane < SSD_HEAD_DIM, xp, zero),
                                   jnp.where(lane >= SSD_HEAD_DIM, xp, zero)], axis=0)
            y = (_dot(jnp.concatenate(ms, axis=1), rhs) + y_off_f[:, cols]
                 + dskip_ref[:, cols] * xp.astype(F32))
            yf_ref[fr, cols] = y.astype(BF16)


def _ssd_call(xbc, rowpack, colsplit, xa, xewf, xewb, dskip, h0f, h0b):
    L = xbc.shape[0]
    nc = L // SSD_BLOCK
    const = lambda s: (0, 0)
    fwd = lambda s: (s, 0)
    bwd = lambda s: (nc - 1 - s, 0)
    st_shape = jax.ShapeDtypeStruct((D_STATE, D_SSD), F32)
    return pl.pallas_call(
        _ssd_kernel,
        grid=(nc,),
        in_specs=[pl.BlockSpec((SSD_BLOCK, D_XBC), fwd), pl.BlockSpec((SSD_BLOCK, D_XBC), bwd),
                  pl.BlockSpec((2 * N_DH, SSD_BLOCK), lambda s: (0, s)),
                  pl.BlockSpec((SSD_BLOCK, 256), fwd), pl.BlockSpec((SSD_BLOCK, 256), bwd),
                  pl.BlockSpec(xa.shape, const), pl.BlockSpec(xewf.shape, const),
                  pl.BlockSpec(xewb.shape, const), pl.BlockSpec((1, D_SSD), const),
                  pl.BlockSpec((D_STATE, D_SSD), const), pl.BlockSpec((D_STATE, D_SSD), const)],
        out_specs=[pl.BlockSpec((SSD_BLOCK, D_SSD), fwd), pl.BlockSpec((SSD_BLOCK, D_SSD), bwd),
                   pl.BlockSpec((D_STATE, D_SSD), const), pl.BlockSpec((D_STATE, D_SSD), const)],
        out_shape=[jax.ShapeDtypeStruct((L, D_SSD), BF16), jax.ShapeDtypeStruct((L, D_SSD), BF16),
                   st_shape, st_shape],
        compiler_params=pltpu.CompilerParams(dimension_semantics=("arbitrary",),
                                             vmem_limit_bytes=VMEM_LIMIT),
        name="ssd",
    )(xbc, xbc, rowpack, colsplit, colsplit, xa, xewf, xewb, dskip, h0f, h0b)


def _pool_constants():
    bands, inv_cols = [], []
    t = np.arange(POOL_SUB)
    rt, ct = t // GRID_W, t % GRID_W
    for w in POOL_WINDOWS:
        hw = w // 2
        k = np.arange(POOL_SUB + GRID_W * w)
        rk, ck = k // GRID_W, k % GRID_W
        band = ((rk[None, :] >= rt[:, None]) & (rk[None, :] < rt[:, None] + w)
                & (ck[None, :] >= ct[:, None] - hw) & (ck[None, :] < ct[:, None] + hw))
        bands.append(jnp.asarray(band, BF16))
        cnt_c = np.minimum(ct + hw, GRID_W) - np.maximum(ct - hw, 0)
        inv_cols.append(np.broadcast_to((1.0 / cnt_c)[:, None], (POOL_SUB, 128)))
    return bands, jnp.asarray(np.stack(inv_cols), F32)


def _merge_kernel(yf_ref, yb_ref, z_ref, up_ref, upp_ref, upn_ref, xln_ref, ng_ref, pw_ref, ps_ref, wo_ref,
                  g1_ref, l1g_ref, l1b_ref, band0_ref, band1_ref, band2_ref, band3_ref, invc_ref,
                  out_ref, *, tm, rows_total):
    i = pl.program_id(0)
    n = pl.num_programs(0)
    y = yf_ref[...].astype(F32) + yb_ref[...].astype(F32)
    g = y * _silu(z_ref[...].astype(F32))
    yn = (g * lax.rsqrt(jnp.mean(g * g, axis=-1, keepdims=True) + LN_EPS) * ng_ref[...]).astype(BF16)

    c = POOL_GROUP_DIM
    band_refs = (band0_ref, band1_ref, band2_ref, band3_ref)
    keep_p = (i > 0).astype(BF16)
    keep_n = (i < n - 1).astype(BF16)
    sub_row = lax.broadcasted_iota(jnp.int32, (POOL_SUB, 128), 0) // GRID_W
    nsub = tm // POOL_SUB
    sums = []
    for gi, w in enumerate(POOL_WINDOWS):
        hw = w // 2
        cols = slice(gi * c, (gi + 1) * c)
        ext = jnp.concatenate([upp_ref[:, cols] * keep_p, up_ref[:, cols], upn_ref[:, cols] * keep_n], axis=0)
        starts = [POOL_HALO + b * POOL_SUB - hw * GRID_W for b in range(nsub)]
        sums.append([_dot(band_refs[gi][...], ext[s:s + POOL_SUB + GRID_W * w]) for s in starts])
    diffs = []
    for gi, w in enumerate(POOL_WINDOWS):
        hw = w // 2
        cols = slice(gi * c, (gi + 1) * c)
        parts = []
        for b in range(nsub):
            row = sub_row + (i * tm + b * POOL_SUB) // GRID_W
            cnt_r = jnp.minimum(row + hw, rows_total) - jnp.maximum(row - hw, 0)
            inv = invc_ref[gi] / cnt_r.astype(F32)
            u = up_ref[b * POOL_SUB:(b + 1) * POOL_SUB, cols].astype(F32)
            parts.append((sums[gi][b] * jnp.concatenate([inv, inv], axis=1) - u).astype(BF16))
        diffs.append(jnp.concatenate(parts, axis=0))
    p = [(_dot(diffs[gi], pw_ref[gi]) * ps_ref[:, gi * c:(gi + 1) * c]).astype(BF16)
         for gi in range(len(POOL_WINDOWS))]
    lhs = jnp.concatenate([yn] + p, axis=1)
    hr = tm // 2
    for r in range(0, tm, hr):
        mix = _dot(lhs[r:r + hr], wo_ref[...])
        out_ref[r:r + hr, :] = _layer_norm(DEEPNORM_ALPHA * xln_ref[r:r + hr, :] + g1_ref[...] * mix,
                                           l1g_ref[...], l1b_ref[...])


def _merge_call(yf, yb, z, up, xln, norm_g, pool_w, pool_scale, w_out, g1, l1g, l1b, *, tm):
    L = xln.shape[0]
    nt = L // tm
    hb = tm // POOL_HALO
    const = lambda i: (0, 0)
    const3 = lambda i: (0, 0, 0)
    row = lambda i: (i, 0)
    vec = pl.BlockSpec((1, D_MODEL), const)
    bands, inv_cols = _pool_constants()
    return pl.pallas_call(
        functools.partial(_merge_kernel, tm=tm, rows_total=L // GRID_W),
        grid=(nt,),
        in_specs=[pl.BlockSpec((tm, D_SSD), row), pl.BlockSpec((tm, D_SSD), row),
                  pl.BlockSpec((tm, D_SSD), row),
                  pl.BlockSpec((tm, D_POOL), row),
                  pl.BlockSpec((POOL_HALO, D_POOL), lambda i: (jnp.maximum(i * hb - 1, 0), 0)),
                  pl.BlockSpec((POOL_HALO, D_POOL),
                               lambda i: (jnp.minimum((i + 1) * hb, L // POOL_HALO - 1), 0)),
                  pl.BlockSpec((tm, D_MODEL), row),
                  vec,
                  pl.BlockSpec(pool_w.shape, const3),
                  vec,
                  pl.BlockSpec(w_out.shape, const),
                  vec, vec, vec]
                 + [pl.BlockSpec(bm.shape, const) for bm in bands]
                 + [pl.BlockSpec(inv_cols.shape, const3)],
        out_specs=pl.BlockSpec((tm, D_MODEL), row),
        out_shape=jax.ShapeDtypeStruct((L, D_MODEL), F32),
        compiler_params=pltpu.CompilerParams(dimension_semantics=("arbitrary",),
                                             vmem_limit_bytes=VMEM_LIMIT),
        name="merge",
    )(yf, yb, z, up, up, up, xln, norm_g, pool_w, pool_scale, w_out, g1, l1g, l1b, *bands, inv_cols)


def _ffn_kernel(x_ref, sh_ref, sc_ref, g2_ref, wg_ref, wu_ref, wd_ref, lg_ref, lb_ref, out_ref):
    x = x_ref[...]
    h = (x * (1.0 + sc_ref[...]) + sh_ref[...]).astype(BF16)
    half = D_FF // 2
    ffn = None
    for s in (0, half):
        gate = _dot(h, wg_ref[:, s:s + half])
        upv = _dot(h, wu_ref[:, s:s + half])
        part = _dot((_silu(gate) * upv).astype(BF16), wd_ref[s:s + half, :])
        ffn = part if ffn is None else ffn + part
    out_ref[...] = _layer_norm(DEEPNORM_ALPHA * x + g2_ref[...] * ffn, lg_ref[...], lb_ref[...])


def _ffn_call(x, sh, sc, g2, wg, wu, wd, lg, lb, *, tm):
    L = x.shape[0]
    const = lambda i: (0, 0)
    row = lambda i: (i, 0)
    vec = pl.BlockSpec((1, D_MODEL), const)
    return pl.pallas_call(
        _ffn_kernel,
        grid=(L // tm,),
        in_specs=[pl.BlockSpec((tm, D_MODEL), row), vec, vec, vec,
                  pl.BlockSpec(wg.shape, const, pipeline_mode=pl.Buffered(1)),
                  pl.BlockSpec(wu.shape, const, pipeline_mode=pl.Buffered(1)),
                  pl.BlockSpec(wd.shape, const, pipeline_mode=pl.Buffered(1)),
                  vec, vec],
        out_specs=pl.BlockSpec((tm, D_MODEL), row),
        out_shape=jax.ShapeDtypeStruct((L, D_MODEL), F32),
        compiler_params=pltpu.CompilerParams(dimension_semantics=("arbitrary",),
                                             vmem_limit_bytes=VMEM_LIMIT),
        name="ffn",
    )(x, sh, sc, g2, wg, wu, wd, lg, lb)


def kernel(x, c, ctx, c_ctx, emb_ln_g, emb_ln_b, w_ada, b_ada, in_proj, conv_w, conv_b, dt_bias, a_log,
           d_skip, ssd_norm_g, pool_w, pool_scale, w_out, ln1_g, ln1_b, w_gate, w_up, w_down, ln2_g, ln2_b):
    assert x.shape[0] == 1 and w_ada.shape[0] == DEPTH == 1
    xl, xc = x[0], ctx[0]
    rowv = lambda v: v.reshape(1, -1)
    elg, elb = rowv(emb_ln_g), rowv(emb_ln_b)

    mod = _mod_call(jnp.stack([c[0], c_ctx], axis=1), w_ada, rowv(b_ada[0]))
    sh1, sc1, g1, sh2, sc2, g2 = [mod[0:1, k * D_MODEL:(k + 1) * D_MODEL] for k in range(6)]
    sh1c, sc1c = mod[1:2, 0:D_MODEL], mod[1:2, D_MODEL:2 * D_MODEL]

    ip = in_proj[0]
    o1, o2, o3 = D_SSD, D_SSD + D_XBC, D_SSD + D_XBC + N_DH
    wz = ip[:, :o1].astype(BF16)
    wxd = jnp.concatenate([ip[:, o1:o3], jnp.zeros((D_MODEL, D_XD - D_XBC - N_DH), F32)], axis=1).astype(BF16)
    wp = ip[:, o3:].astype(BF16)
    proj_args = (wz, wxd, wp, conv_w[0], rowv(conv_b[0]), dt_bias[0].reshape(N_DH, 1), a_log[0].reshape(N_DH, 1))

    xa, xewf, xewb = _expand_matrices()
    dskip = rowv(jnp.repeat(d_skip[0], SSD_HEAD_DIM))
    h_zero = jnp.zeros((D_STATE, D_SSD), F32)

    _, xbc_c, _, _, rp_c, cs_c = _inproj_call(xc, elg, elb, sh1c, sc1c, *proj_args, tm=xc.shape[0])
    _, _, hf_ctx, hb_ctx = _ssd_call(xbc_c, rp_c, cs_c, xa, xewf, xewb, dskip, h_zero, h_zero)

    z, xbc, up, xln, rp, cs = _inproj_call(xl, elg, elb, sh1, sc1, *proj_args, tm=512)
    yf, yb, _, _ = _ssd_call(xbc, rp, cs, xa, xewf, xewb, dskip, hf_ctx, hb_ctx)
    x1 = _merge_call(yf, yb, z, up, xln, rowv(ssd_norm_g[0]), pool_w[0].astype(BF16), rowv(pool_scale[0]),
                     w_out[0].astype(BF16), g1, rowv(ln1_g[0]), rowv(ln1_b[0]), tm=512)
    x2 = _ffn_call(x1, sh2, sc2, g2, w_gate[0].astype(BF16), w_up[0].astype(BF16), w_down[0].astype(BF16),
                   rowv(ln2_g[0]), rowv(ln2_b[0]), tm=1024)
    return x2[None]
```

```python
import functools

import jax
import jax.numpy as jnp
import numpy as np
from jax import lax
from jax.experimental import pallas as pl
from jax.experimental.pallas import tpu as pltpu

F32 = jnp.float32
BF16 = jnp.bfloat16

D_MODEL = 1024
SSD_HEADS = 16
SSD_HEAD_DIM = 64
D_SSD = SSD_HEADS * SSD_HEAD_DIM
D_STATE = 128
D_CONV = 5
CHUNK = 128
D_POOL = 1024
POOL_WINDOWS = (2, 4, 8, 16)
POOL_GROUP_DIM = D_POOL // len(POOL_WINDOWS)
GRID_W = 64
D_XBC = D_SSD + 2 * D_STATE
D_XD = D_XBC + 128
D_FF = 2816
DEPTH = 1
DEEPNORM_ALPHA = (2 * DEPTH) ** 0.25
LN_EPS = 1e-5

HALO = 16
POOL_HALO = 512
POOL_SUB = 256
SSD_BLOCK = 256
PROJ_BLOCK = 256
MOD_KB = 256
N_DH = 2 * SSD_HEADS
VMEM_LIMIT = 56 * 1024 * 1024


def _dot(a, b):
    return jnp.dot(a, b, preferred_element_type=F32)


def _silu(x):
    hx = 0.5 * x
    return hx + hx * jnp.tanh(hx)


def _layer_norm(x, g, b):
    mu = jnp.mean(x, axis=-1, keepdims=True)
    xc = x - mu
    var = jnp.mean(xc * xc, axis=-1, keepdims=True)
    return xc * lax.rsqrt(var + LN_EPS) * g + b


def _bf16_part(v):
    return v.astype(BF16).astype(F32)


def _mod_kernel(cc_ref, w_ref, b_ref, out_ref):
    k = pl.program_id(0)

    @pl.when(k == 0)
    def _():
        out_ref[...] = jnp.broadcast_to(b_ref[...], out_ref.shape)

    s = _silu(cc_ref[...])
    w = w_ref[...]
    out_ref[0:1, :] += jnp.sum(w * s[:, 0:1], axis=0, keepdims=True)
    out_ref[1:2, :] += jnp.sum(w * s[:, 1:2], axis=0, keepdims=True)


def _mod_call(cc, w_ada, b_ada):
    n = w_ada.shape[-1]
    return pl.pallas_call(
        _mod_kernel,
        grid=(D_MODEL // MOD_KB,),
        in_specs=[pl.BlockSpec((MOD_KB, 2), lambda k: (k, 0)),
                  pl.BlockSpec((None, MOD_KB, n), lambda k: (0, k, 0)),
                  pl.BlockSpec((1, n), lambda k: (0, 0))],
        out_specs=pl.BlockSpec((8, n), lambda k: (0, 0)),
        out_shape=jax.ShapeDtypeStruct((8, n), F32),
        compiler_params=pltpu.CompilerParams(dimension_semantics=("arbitrary",),
                                             vmem_limit_bytes=VMEM_LIMIT),
        name="mod",
    )(cc, w_ada, b_ada)


def _inproj_kernel(x_ref, xp_ref, xn_ref, lng_ref, lnb_ref, sh_ref, sc_ref, wz_ref, wxd_ref, wp_ref,
                   cw_ref, cb_ref, dtb_ref, alog_ref,
                   z_ref, xbc_ref, up_ref, xln_ref, rowpack_ref, colsplit_ref, h_scr, *, tm):
    i = pl.program_id(0)
    n = pl.num_programs(0)
    lng, lnb, sh, sc = lng_ref[...], lnb_ref[...], sh_ref[...], sc_ref[...]

    def modulated(xn):
        return xn * (1.0 + sc) + sh

    xln = _layer_norm(x_ref[...], lng, lnb)
    xln_ref[...] = xln
    h = modulated(xln).astype(BF16)
    hp = (modulated(_layer_norm(xp_ref[...], lng, lnb)) * (i > 0).astype(F32)).astype(BF16)
    hn = (modulated(_layer_norm(xn_ref[...], lng, lnb)) * (i < n - 1).astype(F32)).astype(BF16)

    h_scr[0:HALO, :] = hp
    h_scr[HALO:HALO + tm, :] = h
    h_scr[HALO + tm:HALO + tm + HALO, :] = hn
    dt_raw = _dot(h_scr[HALO:HALO + tm, :], wxd_ref[:, D_XBC:D_XD]).T[0:N_DH, :] + dtb_ref[...]
    dt = jnp.maximum(dt_raw, 0.0) + jnp.log(1.0 + jnp.exp(-jnp.abs(dt_raw)))
    a = dt * (-jnp.exp(alog_ref[...]))
    lane = lax.broadcasted_iota(jnp.int32, (N_DH, tm), 1) & (CHUNK - 1)
    row = lax.broadcasted_iota(jnp.int32, (N_DH, tm), 0)
    cf, cr = a, a
    k = 1
    while k < CHUNK:
        cf = cf + jnp.where(lane >= k, pltpu.roll(cf, k, 1), 0.0)
        cr = cr + jnp.where(lane < CHUNK - k, pltpu.roll(cr, tm - k, 1), 0.0)
        k *= 2
    is_fwd = row < SSD_HEADS
    acum = jnp.where(is_fwd, cf, cr)
    e = jnp.exp(acum)
    w_end = jnp.exp(jnp.where(is_fwd, cr, cf) - a) * dt
    rowpack_ref[...] = jnp.concatenate([acum, dt], axis=0)

    a1 = _bf16_part(acum); a2 = _bf16_part(acum - a1); a3 = _bf16_part(acum - a1 - a2)
    e1 = _bf16_part(e); e2 = _bf16_part(e - e1); e3 = _bf16_part(e - e1 - e2)
    w1 = _bf16_part(w_end); w2 = _bf16_part(w_end - w1)
    top = jnp.concatenate([a1, a2, a3, e1], axis=0)
    bot = jnp.concatenate([e2, e3, w1, w2], axis=0)
    colsplit_ref[...] = jnp.concatenate([top.T, bot.T], axis=1).astype(BF16)

    ne = tm + 2 * HALO
    nb = PROJ_BLOCK
    zu_blocks = [(w_ref, o_ref, c) for w_ref, o_ref in ((wz_ref, z_ref), (wp_ref, up_ref))
                 for c in range(0, D_SSD, nb)]
    for j, c0 in enumerate(range(0, D_XBC, nb)):
        cols = slice(c0, c0 + nb)
        u = _dot(h_scr[...], wxd_ref[:, cols])
        take = -(-len(zu_blocks) // (D_XBC // nb - j))
        for w_ref, o_ref, c in zu_blocks[:take]:
            o_ref[:, c:c + nb] = _dot(h_scr[HALO:HALO + tm, :], w_ref[:, c:c + nb]).astype(BF16)
        zu_blocks = zu_blocks[take:]
        f = [cw_ref[k:k + 1, cols] * u for k in range(D_CONV)]
        after = pltpu.roll(f[3] + pltpu.roll(f[4], ne - 1, 0), ne - 1, 0)
        before = pltpu.roll(f[1] + pltpu.roll(f[0], 1, 0), 1, 0)
        acc = (f[2] + after + before)[HALO:HALO + tm] + cb_ref[:, cols]
        xbc_ref[:, cols] = _silu(acc).astype(BF16)


def _inproj_convert_kernel(x_ref, xp_ref, xn_ref, lng_ref, lnb_ref, sh_ref, sc_ref, wf_ref,
                           cw_ref, cb_ref, dtb_ref, alog_ref,
                           z_ref, xbc_ref, up_ref, xln_ref, rowpack_ref, colsplit_ref,
                           wz_ref, wxd_ref, wp_ref, h_scr, *, tm):
    @pl.when(pl.program_id(0) == 0)
    def _():
        wz_ref[...] = wf_ref[:, 0:D_SSD].astype(BF16)
        wxd_ref[...] = wf_ref[:, D_SSD:D_SSD + D_XD].astype(BF16)
        o3 = D_SSD + D_XBC + N_DH
        wp_ref[...] = wf_ref[:, o3:o3 + D_POOL].astype(BF16)

    _inproj_kernel(x_ref, xp_ref, xn_ref, lng_ref, lnb_ref, sh_ref, sc_ref, wz_ref, wxd_ref, wp_ref,
                   cw_ref, cb_ref, dtb_ref, alog_ref,
                   z_ref, xbc_ref, up_ref, xln_ref, rowpack_ref, colsplit_ref, h_scr, tm=tm)


def _inproj_call(x, ln_g, ln_b, sh, sc, weights, conv_w, conv_b, dt_bias, a_log, *, tm):
    L = x.shape[0]
    nt = L // tm
    hb = tm // HALO
    const = lambda i: (0, 0)
    row = lambda i: (i, 0)
    w_shapes = [(D_MODEL, D_SSD), (D_MODEL, D_XD), (D_MODEL, D_POOL)]
    w_specs = [pl.BlockSpec(s, const) for s in w_shapes]
    convert = not isinstance(weights, tuple)
    if convert:
        body = _inproj_convert_kernel
        w_in_specs = [pl.BlockSpec((None,) + weights.shape[1:], lambda i: (0, 0, 0),
                                   pipeline_mode=pl.Buffered(1))]
        w_args = (weights,)
    else:
        body, w_in_specs, w_args = _inproj_kernel, w_specs, weights
    return pl.pallas_call(
        functools.partial(body, tm=tm),
        grid=(nt,),
        in_specs=[pl.BlockSpec((tm, D_MODEL), row),
                  pl.BlockSpec((HALO, D_MODEL), lambda i: (jnp.maximum(i * hb - 1, 0), 0)),
                  pl.BlockSpec((HALO, D_MODEL), lambda i: (jnp.minimum((i + 1) * hb, L // HALO - 1), 0)),
                  pl.BlockSpec((1, D_MODEL), const), pl.BlockSpec((1, D_MODEL), const),
                  pl.BlockSpec((1, D_MODEL), const), pl.BlockSpec((1, D_MODEL), const)]
                 + w_in_specs
                 + [pl.BlockSpec((None, D_CONV, D_XBC), lambda i: (0, 0, 0)), pl.BlockSpec((1, D_XBC), const),
                    pl.BlockSpec((N_DH, 1), const), pl.BlockSpec((N_DH, 1), const)],
        out_specs=[pl.BlockSpec((tm, D_SSD), row),
                   pl.BlockSpec((tm, D_XBC), row),
                   pl.BlockSpec((tm, D_POOL), row),
                   pl.BlockSpec((tm, D_MODEL), row),
                   pl.BlockSpec((2 * N_DH, tm), lambda i: (0, i)),
                   pl.BlockSpec((tm, 256), row)] + (w_specs if convert else []),
        out_shape=[jax.ShapeDtypeStruct((L, D_SSD), BF16),
                   jax.ShapeDtypeStruct((L, D_XBC), BF16),
                   jax.ShapeDtypeStruct((L, D_POOL), BF16),
                   jax.ShapeDtypeStruct((L, D_MODEL), F32),
                   jax.ShapeDtypeStruct((2 * N_DH, L), F32),
                   jax.ShapeDtypeStruct((L, 256), BF16)]
                  + ([jax.ShapeDtypeStruct(s, BF16) for s in w_shapes] if convert else []),
        scratch_shapes=[pltpu.VMEM((tm + 2 * HALO, D_MODEL), BF16)],
        compiler_params=pltpu.CompilerParams(dimension_semantics=("arbitrary",),
                                             vmem_limit_bytes=VMEM_LIMIT),
        name="inproj",
    )(x, x, x, ln_g, ln_b, sh, sc, *w_args, conv_w, conv_b, dt_bias, a_log)


def _expand_matrices():
    xa = np.zeros((256, N_DH * CHUNK), np.float32)
    for dh in range(N_DH):
        for part in range(3):
            xa[part * N_DH + dh, dh * CHUNK:(dh + 1) * CHUNK] = 1.0
    xew = np.zeros((2, 256, 2 * D_SSD), np.float32)
    for d in range(2):
        for hh in range(SSD_HEADS):
            dh = d * SSD_HEADS + hh
            cols = slice(hh * SSD_HEAD_DIM, (hh + 1) * SSD_HEAD_DIM)
            for part in (3, 4, 5):
                xew[d, part * N_DH + dh, cols] = 1.0
            cols = slice(D_SSD + hh * SSD_HEAD_DIM, D_SSD + (hh + 1) * SSD_HEAD_DIM)
            for part in (6, 7):
                xew[d, part * N_DH + dh, cols] = 1.0
    return jnp.asarray(xa, BF16), jnp.asarray(xew[0], BF16), jnp.asarray(xew[1], BF16)


def _ssd_kernel(xbcf_ref, xbcb_ref, rp_ref, csf_ref, csb_ref, xa_ref, xewf_ref, xewb_ref, dskip_ref,
                h0f_ref, h0b_ref, yf_ref, yb_ref, hf_ref, hb_ref):
    q = CHUNK
    nsub = SSD_BLOCK // CHUNK

    @pl.when(pl.program_id(0) == 0)
    def _():
        hf_ref[...] = h0f_ref[...]
        hb_ref[...] = h0b_ref[...]

    def stream(xbc_ref, cs, rows, xew_ref, h_ref, exit_row):
        xs = xbc_ref[rows, 0:D_SSD]
        bm = xbc_ref[rows, D_SSD:D_SSD + D_STATE]
        cm = xbc_ref[rows, D_SSD + D_STATE:D_XBC]
        ew = _dot(cs, xew_ref[...])
        e, w_end = ew[:, 0:D_SSD], ew[:, D_SSD:]
        st = h_ref[...]
        y_off = _dot(cm, st.astype(BF16)) * e
        xw = (xs.astype(F32) * w_end).astype(BF16)
        upd = lax.dot_general(bm, xw, (((0,), (0,)), ((), ())), preferred_element_type=F32)
        h_ref[...] = st * e[exit_row:exit_row + 1, :] + upd
        return y_off

    ti = lax.broadcasted_iota(jnp.int32, (q, q), 0)
    ui = lax.broadcasted_iota(jnp.int32, (q, q), 1)
    lane = lax.broadcasted_iota(jnp.int32, (q, 2 * SSD_HEAD_DIM), 1)
    for sub in range(nsub):
        fr = slice(sub * q, (sub + 1) * q)
        br = slice((nsub - 1 - sub) * q, (nsub - sub) * q)
        csf = csf_ref[fr, :]
        y_off_f = stream(xbcf_ref, csf, fr, xewf_ref, hf_ref, q - 1)
        yb_ref[br, :] = stream(xbcb_ref, csb_ref[br, :], br, xewb_ref, hb_ref, 0).astype(BF16)

        acolx = _dot(csf, xa_ref[...])
        rp = rp_ref[:, fr]
        bm = xbcf_ref[fr, D_SSD:D_SSD + D_STATE]
        cm = xbcf_ref[fr, D_SSD + D_STATE:D_XBC]
        cb = lax.dot_general(cm, bm, (((1,), (1,)), ((), ())), preferred_element_type=F32)
        for j in range(SSD_HEADS // 2):
            ms = []
            for hh in (2 * j, 2 * j + 1):
                hb_ = SSD_HEADS + hh
                seg_f = acolx[:, hh * q:(hh + 1) * q] - rp[hh:hh + 1, :]
                seg_b = acolx[:, hb_ * q:(hb_ + 1) * q] - rp[hb_:hb_ + 1, :]
                dtf = rp[N_DH + hh:N_DH + hh + 1, :]
                dtb = rp[N_DH + hb_:N_DH + hb_ + 1, :]
                coef = jnp.where(ui < ti, dtf, jnp.where(ui > ti, dtb, dtf + dtb))
                ms.append((cb * coef * jnp.exp(jnp.where(ui <= ti, seg_f, seg_b))).astype(BF16))
            cols = slice(j * 2 * SSD_HEAD_DIM, (j + 1) * 2 * SSD_HEAD_DIM)
            xp = xbcf_ref[fr, cols]
            zero = jnp.zeros_like(xp)
            rhs = jnp.concatenate([jnp.where(lane < SSD_HEAD_DIM, xp, zero),
                                   jnp.where(lane >= SSD_HEAD_DIM, xp, zero)], axis=0)
            y = (_dot(jnp.concatenate(ms, axis=1), rhs) + y_off_f[:, cols]
                 + dskip_ref[:, cols] * xp.astype(F32))
            yf_ref[fr, cols] = y.astype(BF16)


def _ssd_call(xbc, rowpack, colsplit, xa, xewf, xewb, dskip, h0f, h0b):
    L = xbc.shape[0]
    nc = L // SSD_BLOCK
    const = lambda s: (0, 0)
    fwd = lambda s: (s, 0)
    bwd = lambda s: (nc - 1 - s, 0)
    st_shape = jax.ShapeDtypeStruct((D_STATE, D_SSD), F32)
    return pl.pallas_call(
        _ssd_kernel,
        grid=(nc,),
        in_specs=[pl.BlockSpec((SSD_BLOCK, D_XBC), fwd), pl.BlockSpec((SSD_BLOCK, D_XBC), bwd),
                  pl.BlockSpec((2 * N_DH, SSD_BLOCK), lambda s: (0, s)),
                  pl.BlockSpec((SSD_BLOCK, 256), fwd), pl.BlockSpec((SSD_BLOCK, 256), bwd),
                  pl.BlockSpec(xa.shape, const), pl.BlockSpec(xewf.shape, const),
                  pl.BlockSpec(xewb.shape, const), pl.BlockSpec((1, D_SSD), const),
                  pl.BlockSpec((D_STATE, D_SSD), const), pl.BlockSpec((D_STATE, D_SSD), const)],
        out_specs=[pl.BlockSpec((SSD_BLOCK, D_SSD), fwd), pl.BlockSpec((SSD_BLOCK, D_SSD), bwd),
                   pl.BlockSpec((D_STATE, D_SSD), const), pl.BlockSpec((D_STATE, D_SSD), const)],
        out_shape=[jax.ShapeDtypeStruct((L, D_SSD), BF16), jax.ShapeDtypeStruct((L, D_SSD), BF16),
                   st_shape, st_shape],
        compiler_params=pltpu.CompilerParams(dimension_semantics=("arbitrary",),
                                             vmem_limit_bytes=VMEM_LIMIT),
        name="ssd",
    )(xbc, xbc, rowpack, colsplit, colsplit, xa, xewf, xewb, dskip, h0f, h0b)


def _pool_constants():
    bands, inv_cols = [], []
    t = np.arange(POOL_SUB)
    rt, ct = t // GRID_W, t % GRID_W
    for w in POOL_WINDOWS:
        hw = w // 2
        k = np.arange(POOL_SUB + GRID_W * w)
        rk, ck = k // GRID_W, k % GRID_W
        band = ((rk[None, :] >= rt[:, None]) & (rk[None, :] < rt[:, None] + w)
                & (ck[None, :] >= ct[:, None] - hw) & (ck[None, :] < ct[:, None] + hw))
        bands.append(jnp.asarray(band, BF16))
        cnt_c = np.minimum(ct + hw, GRID_W) - np.maximum(ct - hw, 0)
        inv_cols.append(np.broadcast_to((1.0 / cnt_c)[:, None], (POOL_SUB, 128)))
    return bands, jnp.asarray(np.stack(inv_cols), F32)


def _merge_kernel(yf_ref, yb_ref, z_ref, up_ref, upp_ref, upn_ref, xln_ref, ng_ref, pw_ref, ps_ref, wo_ref,
                  g1_ref, l1g_ref, l1b_ref, band0_ref, band1_ref, band2_ref, band3_ref, invc_ref,
                  out_ref, pw_scr, wo_scr, *, tm, rows_total):
    i = pl.program_id(0)
    n = pl.num_programs(0)

    @pl.when(i == 0)
    def _():
        pw_scr[...] = pw_ref[...].astype(BF16)
        wo_scr[...] = wo_ref[...].astype(BF16)

    y = yf_ref[...].astype(F32) + yb_ref[...].astype(F32)
    g = y * _silu(z_ref[...].astype(F32))
    yn = (g * lax.rsqrt(jnp.mean(g * g, axis=-1, keepdims=True) + LN_EPS) * ng_ref[...]).astype(BF16)

    c = POOL_GROUP_DIM
    band_refs = (band0_ref, band1_ref, band2_ref, band3_ref)
    keep_p = (i > 0).astype(BF16)
    keep_n = (i < n - 1).astype(BF16)
    sub_row = lax.broadcasted_iota(jnp.int32, (POOL_SUB, 128), 0) // GRID_W
    nsub = tm // POOL_SUB
    sums = []
    for gi, w in enumerate(POOL_WINDOWS):
        hw = w // 2
        cols = slice(gi * c, (gi + 1) * c)
        ext = jnp.concatenate([upp_ref[:, cols] * keep_p, up_ref[:, cols], upn_ref[:, cols] * keep_n], axis=0)
        starts = [POOL_HALO + b * POOL_SUB - hw * GRID_W for b in range(nsub)]
        sums.append([_dot(band_refs[gi][...], ext[s:s + POOL_SUB + GRID_W * w]) for s in starts])
    diffs = []
    for gi, w in enumerate(POOL_WINDOWS):
        hw = w // 2
        cols = slice(gi * c, (gi + 1) * c)
        parts = []
        for b in range(nsub):
            row = sub_row + (i * tm + b * POOL_SUB) // GRID_W
            cnt_r = jnp.minimum(row + hw, rows_total) - jnp.maximum(row - hw, 0)
            inv = invc_ref[gi] / cnt_r.astype(F32)
            u = up_ref[b * POOL_SUB:(b + 1) * POOL_SUB, cols].astype(F32)
            parts.append((sums[gi][b] * jnp.concatenate([inv, inv], axis=1) - u).astype(BF16))
        diffs.append(jnp.concatenate(parts, axis=0))
    p = [(_dot(diffs[gi], pw_scr[gi]) * ps_ref[:, gi * c:(gi + 1) * c]).astype(BF16)
         for gi in range(len(POOL_WINDOWS))]
    lhs = jnp.concatenate([yn] + p, axis=1)
    hr = tm // 2
    for r in range(0, tm, hr):
        mix = _dot(lhs[r:r + hr], wo_scr[...])
        out_ref[r:r + hr, :] = _layer_norm(DEEPNORM_ALPHA * xln_ref[r:r + hr, :] + g1_ref[...] * mix,
                                           l1g_ref[...], l1b_ref[...])


def _merge_call(yf, yb, z, up, xln, norm_g, pool_w, pool_scale, w_out, g1, l1g, l1b, *, tm):
    L = xln.shape[0]
    nt = L // tm
    hb = tm // POOL_HALO
    const = lambda i: (0, 0)
    const3 = lambda i: (0, 0, 0)
    row = lambda i: (i, 0)
    vec = pl.BlockSpec((1, D_MODEL), const)
    bands, inv_cols = _pool_constants()
    return pl.pallas_call(
        functools.partial(_merge_kernel, tm=tm, rows_total=L // GRID_W),
        grid=(nt,),
        in_specs=[pl.BlockSpec((tm, D_SSD), row), pl.BlockSpec((tm, D_SSD), row),
                  pl.BlockSpec((tm, D_SSD), row),
                  pl.BlockSpec((tm, D_POOL), row),
                  pl.BlockSpec((POOL_HALO, D_POOL), lambda i: (jnp.maximum(i * hb - 1, 0), 0)),
                  pl.BlockSpec((POOL_HALO, D_POOL),
                               lambda i: (jnp.minimum((i + 1) * hb, L // POOL_HALO - 1), 0)),
                  pl.BlockSpec((tm, D_MODEL), row),
                  vec,
                  pl.BlockSpec((None,) + pool_w.shape[1:], lambda i: (0, 0, 0, 0), pipeline_mode=pl.Buffered(1)),
                  vec,
                  pl.BlockSpec((None,) + w_out.shape[1:], const3, pipeline_mode=pl.Buffered(1)),
                  vec, vec, vec]
                 + [pl.BlockSpec(bm.shape, const) for bm in bands]
                 + [pl.BlockSpec(inv_cols.shape, const3)],
        out_specs=pl.BlockSpec((tm, D_MODEL), row),
        out_shape=jax.ShapeDtypeStruct((L, D_MODEL), F32),
        scratch_shapes=[pltpu.VMEM(pool_w.shape[1:], BF16), pltpu.VMEM(w_out.shape[1:], BF16)],
        compiler_params=pltpu.CompilerParams(dimension_semantics=("arbitrary",),
                                             vmem_limit_bytes=VMEM_LIMIT),
        name="merge",
    )(yf, yb, z, up, up, up, xln, norm_g, pool_w, pool_scale, w_out, g1, l1g, l1b, *bands, inv_cols)


def _ffn_kernel(x_ref, sh_ref, sc_ref, g2_ref, wg_ref, wu_ref, wd_ref, lg_ref, lb_ref, out_ref):
    x = x_ref[...]
    h = (x * (1.0 + sc_ref[...]) + sh_ref[...]).astype(BF16)
    half = D_FF // 2
    ffn = None
    for s in (0, half):
        gate = _dot(h, wg_ref[:, s:s + half])
        upv = _dot(h, wu_ref[:, s:s + half])
        part = _dot((_silu(gate) * upv).astype(BF16), wd_ref[s:s + half, :])
        ffn = part if ffn is None else ffn + part
    out_ref[...] = _layer_norm(DEEPNORM_ALPHA * x + g2_ref[...] * ffn, lg_ref[...], lb_ref[...])


def _ffn_call(x, sh, sc, g2, wg, wu, wd, lg, lb, *, tm):
    L = x.shape[0]
    const = lambda i: (0, 0)
    row = lambda i: (i, 0)
    vec = pl.BlockSpec((1, D_MODEL), const)
    return pl.pallas_call(
        _ffn_kernel,
        grid=(L // tm,),
        in_specs=[pl.BlockSpec((tm, D_MODEL), row), vec, vec, vec,
                  pl.BlockSpec(wg.shape, const, pipeline_mode=pl.Buffered(1)),
                  pl.BlockSpec(wu.shape, const, pipeline_mode=pl.Buffered(1)),
                  pl.BlockSpec(wd.shape, const, pipeline_mode=pl.Buffered(1)),
                  vec, vec],
        out_specs=pl.BlockSpec((tm, D_MODEL), row),
        out_shape=jax.ShapeDtypeStruct((L, D_MODEL), F32),
        compiler_params=pltpu.CompilerParams(dimension_semantics=("arbitrary",),
                                             vmem_limit_bytes=VMEM_LIMIT),
        name="ffn",
    )(x, sh, sc, g2, wg, wu, wd, lg, lb)


def kernel(x, c, ctx, c_ctx, emb_ln_g, emb_ln_b, w_ada, b_ada, in_proj, conv_w, conv_b, dt_bias, a_log,
           d_skip, ssd_norm_g, pool_w, pool_scale, w_out, ln1_g, ln1_b, w_gate, w_up, w_down, ln2_g, ln2_b):
    assert x.shape[0] == 1 and w_ada.shape[0] == DEPTH == 1
    xl, xc = x[0], ctx[0]
    rowv = lambda v: v.reshape(1, -1)
    elg, elb = rowv(emb_ln_g), rowv(emb_ln_b)

    mod = _mod_call(jnp.stack([c[0], c_ctx], axis=1), w_ada, rowv(b_ada[0]))
    sh1, sc1, g1, sh2, sc2, g2 = [mod[0:1, k * D_MODEL:(k + 1) * D_MODEL] for k in range(6)]
    sh1c, sc1c = mod[1:2, 0:D_MODEL], mod[1:2, D_MODEL:2 * D_MODEL]

    conv_args = (conv_w, rowv(conv_b[0]), dt_bias[0].reshape(N_DH, 1), a_log[0].reshape(N_DH, 1))

    xa, xewf, xewb = _expand_matrices()
    dskip = rowv(jnp.repeat(d_skip[0], SSD_HEAD_DIM))
    h_zero = jnp.zeros((D_STATE, D_SSD), F32)

    _, xbc_c, _, _, rp_c, cs_c, wz, wxd, wp = _inproj_call(xc, elg, elb, sh1c, sc1c, in_proj, *conv_args,
                                                           tm=xc.shape[0])
    _, _, hf_ctx, hb_ctx = _ssd_call(xbc_c, rp_c, cs_c, xa, xewf, xewb, dskip, h_zero, h_zero)

    z, xbc, up, xln, rp, cs = _inproj_call(xl, elg, elb, sh1, sc1, (wz, wxd, wp), *conv_args, tm=512)
    yf, yb, _, _ = _ssd_call(xbc, rp, cs, xa, xewf, xewb, dskip, hf_ctx, hb_ctx)
    x1 = _merge_call(yf, yb, z, up, xln, rowv(ssd_norm_g[0]), pool_w, rowv(pool_scale[0]),
                     w_out, g1, rowv(ln1_g[0]), rowv(ln1_b[0]), tm=512)
    x2 = _ffn_call(x1, sh2, sc2, g2, w_gate[0].astype(BF16), w_up[0].astype(BF16), w_down[0].astype(BF16),
                   rowv(ln2_g[0]), rowv(ln2_b[0]), tm=1024)
    return x2[None]
```

```python
import functools

import jax
import jax.numpy as jnp
import numpy as np
from jax import lax
from jax.experimental import pallas as pl
from jax.experimental.pallas import tpu as pltpu

F32 = jnp.float32
BF16 = jnp.bfloat16

D_MODEL = 1024
SSD_HEADS = 16
SSD_HEAD_DIM = 64
D_SSD = SSD_HEADS * SSD_HEAD_DIM
D_STATE = 128
D_CONV = 5
CHUNK = 128
D_POOL = 1024
POOL_WINDOWS = (2, 4, 8, 16)
POOL_GROUP_DIM = D_POOL // len(POOL_WINDOWS)
GRID_W = 64
D_XBC = D_SSD + 2 * D_STATE
D_XD = D_XBC + 128
D_FF = 2816
DEPTH = 1
DEEPNORM_ALPHA = (2 * DEPTH) ** 0.25
LN_EPS = 1e-5

HALO = 16
POOL_HALO = 512
POOL_SUB = 256
SSD_BLOCK = 256
PROJ_BLOCK = 256
MOD_KB = 256
FFN_STAGE_CHUNKS = 8
N_DH = 2 * SSD_HEADS
VMEM_LIMIT = 56 * 1024 * 1024


def _dot(a, b):
    return jnp.dot(a, b, preferred_element_type=F32)


def _dot_nt(a, b):
    return lax.dot_general(a, b, (((1,), (1,)), ((), ())), preferred_element_type=F32)


def _silu(x):
    hx = 0.5 * x
    return hx + hx * jnp.tanh(hx)


def _layer_norm(x, g, b):
    mu = jnp.mean(x, axis=-1, keepdims=True)
    xc = x - mu
    var = jnp.mean(xc * xc, axis=-1, keepdims=True)
    return xc * lax.rsqrt(var + LN_EPS) * g + b


def _bf16_part(v):
    return v.astype(BF16).astype(F32)


def _mod_kernel(cc_ref, w_ref, b_ref, out_ref):
    k = pl.program_id(0)

    @pl.when(k == 0)
    def _():
        out_ref[...] = jnp.broadcast_to(b_ref[...], out_ref.shape)

    s = _silu(cc_ref[...])
    w = w_ref[...]
    out_ref[0:1, :] += jnp.sum(w * s[:, 0:1], axis=0, keepdims=True)
    out_ref[1:2, :] += jnp.sum(w * s[:, 1:2], axis=0, keepdims=True)


def _mod_call(cc, w_ada, b_ada):
    n = w_ada.shape[-1]
    return pl.pallas_call(
        _mod_kernel,
        grid=(D_MODEL // MOD_KB,),
        in_specs=[pl.BlockSpec((MOD_KB, 2), lambda k: (k, 0)),
                  pl.BlockSpec((None, MOD_KB, n), lambda k: (0, k, 0)),
                  pl.BlockSpec((1, n), lambda k: (0, 0))],
        out_specs=pl.BlockSpec((8, n), lambda k: (0, 0)),
        out_shape=jax.ShapeDtypeStruct((8, n), F32),
        compiler_params=pltpu.CompilerParams(dimension_semantics=("arbitrary",),
                                             vmem_limit_bytes=VMEM_LIMIT),
        name="mod",
    )(cc, w_ada, b_ada)


def _inproj_kernel(x_ref, xp_ref, xn_ref, lng_ref, lnb_ref, sh_ref, sc_ref, wz_ref, wxd_ref, wp_ref,
                   cw_ref, cb_ref, dtb_ref, alog_ref,
                   z_ref, xbc_ref, up_ref, xln_ref, rowpack_ref, colsplit_ref, h_scr, *, tm):
    i = pl.program_id(0)
    n = pl.num_programs(0)
    lng, lnb, sh, sc = lng_ref[...], lnb_ref[...], sh_ref[...], sc_ref[...]

    def modulated(xn):
        return xn * (1.0 + sc) + sh

    xln = _layer_norm(x_ref[...], lng, lnb)
    xln_ref[...] = xln
    h = modulated(xln).astype(BF16)
    hp = (modulated(_layer_norm(xp_ref[...], lng, lnb)) * (i > 0).astype(F32)).astype(BF16)
    hn = (modulated(_layer_norm(xn_ref[...], lng, lnb)) * (i < n - 1).astype(F32)).astype(BF16)

    h_scr[0:HALO, :] = hp
    h_scr[HALO:HALO + tm, :] = h
    h_scr[HALO + tm:HALO + tm + HALO, :] = hn
    dt_raw = _dot(h_scr[HALO:HALO + tm, :], wxd_ref[:, D_XBC:D_XD]).T[0:N_DH, :] + dtb_ref[...]
    dt = jnp.maximum(dt_raw, 0.0) + jnp.log(1.0 + jnp.exp(-jnp.abs(dt_raw)))
    a = dt * (-jnp.exp(alog_ref[...]))
    lane = lax.broadcasted_iota(jnp.int32, (N_DH, tm), 1) & (CHUNK - 1)
    row = lax.broadcasted_iota(jnp.int32, (N_DH, tm), 0)
    cf, cr = a, a
    k = 1
    while k < CHUNK:
        cf = cf + jnp.where(lane >= k, pltpu.roll(cf, k, 1), 0.0)
        cr = cr + jnp.where(lane < CHUNK - k, pltpu.roll(cr, tm - k, 1), 0.0)
        k *= 2
    is_fwd = row < SSD_HEADS
    acum = jnp.where(is_fwd, cf, cr)
    e = jnp.exp(acum)
    w_end = jnp.exp(jnp.where(is_fwd, cr, cf) - a) * dt
    rowpack_ref[...] = jnp.concatenate([acum, dt], axis=0)

    a1 = _bf16_part(acum); a2 = _bf16_part(acum - a1); a3 = _bf16_part(acum - a1 - a2)
    e1 = _bf16_part(e); e2 = _bf16_part(e - e1); e3 = _bf16_part(e - e1 - e2)
    w1 = _bf16_part(w_end); w2 = _bf16_part(w_end - w1)
    top = jnp.concatenate([a1, a2, a3, e1], axis=0)
    bot = jnp.concatenate([e2, e3, w1, w2], axis=0)
    colsplit_ref[...] = jnp.concatenate([top.T, bot.T], axis=1).astype(BF16)

    ne = tm + 2 * HALO
    nb = PROJ_BLOCK
    zu_blocks = [(w_ref, o_ref, c) for w_ref, o_ref in ((wz_ref, z_ref), (wp_ref, up_ref))
                 for c in range(0, D_SSD, nb)]
    for j, c0 in enumerate(range(0, D_XBC, nb)):
        cols = slice(c0, c0 + nb)
        u = _dot(h_scr[...], wxd_ref[:, cols])
        take = -(-len(zu_blocks) // (D_XBC // nb - j))
        for w_ref, o_ref, c in zu_blocks[:take]:
            o_ref[:, c:c + nb] = _dot(h_scr[HALO:HALO + tm, :], w_ref[:, c:c + nb]).astype(BF16)
        zu_blocks = zu_blocks[take:]
        f = [cw_ref[k:k + 1, cols] * u for k in range(D_CONV)]
        after = pltpu.roll(f[3] + pltpu.roll(f[4], ne - 1, 0), ne - 1, 0)
        before = pltpu.roll(f[1] + pltpu.roll(f[0], 1, 0), 1, 0)
        acc = (f[2] + after + before)[HALO:HALO + tm] + cb_ref[:, cols]
        xbc_ref[:, cols] = _silu(acc).astype(BF16)


def _inproj_convert_kernel(x_ref, xp_ref, xn_ref, lng_ref, lnb_ref, sh_ref, sc_ref, wf_ref,
                           cw_ref, cb_ref, dtb_ref, alog_ref,
                           z_ref, xbc_ref, up_ref, xln_ref, rowpack_ref, colsplit_ref,
                           wz_ref, wxd_ref, wp_ref, h_scr, *, tm):
    @pl.when(pl.program_id(0) == 0)
    def _():
        wz_ref[...] = wf_ref[0:D_SSD, :].T.astype(BF16)
        wxd_ref[...] = wf_ref[D_SSD:D_SSD + D_XD, :].T.astype(BF16)
        o3 = D_SSD + D_XBC + N_DH
        wp_ref[...] = wf_ref[o3:o3 + D_POOL, :].T.astype(BF16)

    _inproj_kernel(x_ref, xp_ref, xn_ref, lng_ref, lnb_ref, sh_ref, sc_ref, wz_ref, wxd_ref, wp_ref,
                   cw_ref, cb_ref, dtb_ref, alog_ref,
                   z_ref, xbc_ref, up_ref, xln_ref, rowpack_ref, colsplit_ref, h_scr, tm=tm)


def _inproj_call(x, ln_g, ln_b, sh, sc, weights, conv_w, conv_b, dt_bias, a_log, *, tm):
    L = x.shape[0]
    nt = L // tm
    hb = tm // HALO
    const = lambda i: (0, 0)
    row = lambda i: (i, 0)
    w_shapes = [(D_MODEL, D_SSD), (D_MODEL, D_XD), (D_MODEL, D_POOL)]
    w_specs = [pl.BlockSpec(s, const) for s in w_shapes]
    convert = not isinstance(weights, tuple)
    if convert:
        body = _inproj_convert_kernel
        w_in_specs = [pl.BlockSpec((None,) + weights.shape[1:], lambda i: (0, 0, 0),
                                   pipeline_mode=pl.Buffered(1))]
        w_args = (weights,)
    else:
        body, w_in_specs, w_args = _inproj_kernel, w_specs, weights
    return pl.pallas_call(
        functools.partial(body, tm=tm),
        grid=(nt,),
        in_specs=[pl.BlockSpec((tm, D_MODEL), row),
                  pl.BlockSpec((HALO, D_MODEL), lambda i: (jnp.maximum(i * hb - 1, 0), 0)),
                  pl.BlockSpec((HALO, D_MODEL), lambda i: (jnp.minimum((i + 1) * hb, L // HALO - 1), 0)),
                  pl.BlockSpec((1, D_MODEL), const), pl.BlockSpec((1, D_MODEL), const),
                  pl.BlockSpec((1, D_MODEL), const), pl.BlockSpec((1, D_MODEL), const)]
                 + w_in_specs
                 + [pl.BlockSpec((None, D_CONV, D_XBC), lambda i: (0, 0, 0)), pl.BlockSpec((1, D_XBC), const),
                    pl.BlockSpec((N_DH, 1), const), pl.BlockSpec((N_DH, 1), const)],
        out_specs=[pl.BlockSpec((tm, D_SSD), row),
                   pl.BlockSpec((tm, D_XBC), row),
                   pl.BlockSpec((tm, D_POOL), row),
                   pl.BlockSpec((tm, D_MODEL), row),
                   pl.BlockSpec((2 * N_DH, tm), lambda i: (0, i)),
                   pl.BlockSpec((tm, 256), row)] + (w_specs if convert else []),
        out_shape=[jax.ShapeDtypeStruct((L, D_SSD), BF16),
                   jax.ShapeDtypeStruct((L, D_XBC), BF16),
                   jax.ShapeDtypeStruct((L, D_POOL), BF16),
                   jax.ShapeDtypeStruct((L, D_MODEL), F32),
                   jax.ShapeDtypeStruct((2 * N_DH, L), F32),
                   jax.ShapeDtypeStruct((L, 256), BF16)]
                  + ([jax.ShapeDtypeStruct(s, BF16) for s in w_shapes] if convert else []),
        scratch_shapes=[pltpu.VMEM((tm + 2 * HALO, D_MODEL), BF16)],
        compiler_params=pltpu.CompilerParams(dimension_semantics=("arbitrary",),
                                             vmem_limit_bytes=VMEM_LIMIT),
        name="inproj",
    )(x, x, x, ln_g, ln_b, sh, sc, *w_args, conv_w, conv_b, dt_bias, a_log)


def _expand_matrices():
    xa = np.zeros((256, N_DH * CHUNK), np.float32)
    for dh in range(N_DH):
        for part in range(3):
            xa[part * N_DH + dh, dh * CHUNK:(dh + 1) * CHUNK] = 1.0
    xew = np.zeros((2, 256, 2 * D_SSD), np.float32)
    for d in range(2):
        for hh in range(SSD_HEADS):
            dh = d * SSD_HEADS + hh
            cols = slice(hh * SSD_HEAD_DIM, (hh + 1) * SSD_HEAD_DIM)
            for part in (3, 4, 5):
                xew[d, part * N_DH + dh, cols] = 1.0
            cols = slice(D_SSD + hh * SSD_HEAD_DIM, D_SSD + (hh + 1) * SSD_HEAD_DIM)
            for part in (6, 7):
                xew[d, part * N_DH + dh, cols] = 1.0
    return jnp.asarray(xa, BF16), jnp.asarray(xew[0], BF16), jnp.asarray(xew[1], BF16)


def _ssd_kernel(xbcf_ref, xbcb_ref, rp_ref, csf_ref, csb_ref, xa_ref, xewf_ref, xewb_ref, dskip_ref,
                h0f_ref, h0b_ref, yf_ref, yb_ref, hf_ref, hb_ref):
    q = CHUNK
    nsub = SSD_BLOCK // CHUNK

    @pl.when(pl.program_id(0) == 0)
    def _():
        hf_ref[...] = h0f_ref[...]
        hb_ref[...] = h0b_ref[...]

    def stream(xbc_ref, cs, rows, xew_ref, h_ref, exit_row):
        xs = xbc_ref[rows, 0:D_SSD]
        bm = xbc_ref[rows, D_SSD:D_SSD + D_STATE]
        cm = xbc_ref[rows, D_SSD + D_STATE:D_XBC]
        ew = _dot(cs, xew_ref[...])
        e, w_end = ew[:, 0:D_SSD], ew[:, D_SSD:]
        st = h_ref[...]
        y_off = _dot(cm, st.astype(BF16)) * e
        xw = (xs.astype(F32) * w_end).astype(BF16)
        upd = lax.dot_general(bm, xw, (((0,), (0,)), ((), ())), preferred_element_type=F32)
        h_ref[...] = st * e[exit_row:exit_row + 1, :] + upd
        return y_off

    ti = lax.broadcasted_iota(jnp.int32, (q, q), 0)
    ui = lax.broadcasted_iota(jnp.int32, (q, q), 1)
    lane = lax.broadcasted_iota(jnp.int32, (q, 2 * SSD_HEAD_DIM), 1)
    for sub in range(nsub):
        fr = slice(sub * q, (sub + 1) * q)
        br = slice((nsub - 1 - sub) * q, (nsub - sub) * q)
        csf = csf_ref[fr, :]
        y_off_f = stream(xbcf_ref, csf, fr, xewf_ref, hf_ref, q - 1)
        yb_ref[br, :] = stream(xbcb_ref, csb_ref[br, :], br, xewb_ref, hb_ref, 0).astype(BF16)

        acolx = _dot(csf, xa_ref[...])
        rp = rp_ref[:, fr]
        bm = xbcf_ref[fr, D_SSD:D_SSD + D_STATE]
        cm = xbcf_ref[fr, D_SSD + D_STATE:D_XBC]
        cb = _dot_nt(cm, bm)
        for j in range(SSD_HEADS // 2):
            ms = []
            for hh in (2 * j, 2 * j + 1):
                hb_ = SSD_HEADS + hh
                seg_f = acolx[:, hh * q:(hh + 1) * q] - rp[hh:hh + 1, :]
                seg_b = acolx[:, hb_ * q:(hb_ + 1) * q] - rp[hb_:hb_ + 1, :]
                dtf = rp[N_DH + hh:N_DH + hh + 1, :]
                dtb = rp[N_DH + hb_:N_DH + hb_ + 1, :]
                coef = jnp.where(ui < ti, dtf, jnp.where(ui > ti, dtb, dtf + dtb))
                ms.append((cb * coef * jnp.exp(jnp.where(ui <= ti, seg_f, seg_b))).astype(BF16))
            cols = slice(j * 2 * SSD_HEAD_DIM, (j + 1) * 2 * SSD_HEAD_DIM)
            xp = xbcf_ref[fr, cols]
            zero = jnp.zeros_like(xp)
            rhs = jnp.concatenate([jnp.where(lane < SSD_HEAD_DIM, xp, zero),
                                   jnp.where(lane >= SSD_HEAD_DIM, xp, zero)], axis=0)
            y = (_dot(jnp.concatenate(ms, axis=1), rhs) + y_off_f[:, cols]
                 + dskip_ref[:, cols] * xp.astype(F32))
            yf_ref[fr, cols] = y.astype(BF16)


def _ssd_call(xbc, rowpack, colsplit, xa, xewf, xewb, dskip, h0f, h0b):
    L = xbc.shape[0]
    nc = L // SSD_BLOCK
    const = lambda s: (0, 0)
    fwd = lambda s: (s, 0)
    bwd = lambda s: (nc - 1 - s, 0)
    st_shape = jax.ShapeDtypeStruct((D_STATE, D_SSD), F32)
    return pl.pallas_call(
        _ssd_kernel,
        grid=(nc,),
        in_specs=[pl.BlockSpec((SSD_BLOCK, D_XBC), fwd), pl.BlockSpec((SSD_BLOCK, D_XBC), bwd),
                  pl.BlockSpec((2 * N_DH, SSD_BLOCK), lambda s: (0, s)),
                  pl.BlockSpec((SSD_BLOCK, 256), fwd), pl.BlockSpec((SSD_BLOCK, 256), bwd),
                  pl.BlockSpec(xa.shape, const), pl.BlockSpec(xewf.shape, const),
                  pl.BlockSpec(xewb.shape, const), pl.BlockSpec((1, D_SSD), const),
                  pl.BlockSpec((D_STATE, D_SSD), const), pl.BlockSpec((D_STATE, D_SSD), const)],
        out_specs=[pl.BlockSpec((SSD_BLOCK, D_SSD), fwd), pl.BlockSpec((SSD_BLOCK, D_SSD), bwd),
                   pl.BlockSpec((D_STATE, D_SSD), const), pl.BlockSpec((D_STATE, D_SSD), const)],
        out_shape=[jax.ShapeDtypeStruct((L, D_SSD), BF16), jax.ShapeDtypeStruct((L, D_SSD), BF16),
                   st_shape, st_shape],
        compiler_params=pltpu.CompilerParams(dimension_semantics=("arbitrary",),
                                             vmem_limit_bytes=VMEM_LIMIT),
        name="ssd",
    )(xbc, xbc, rowpack, colsplit, colsplit, xa, xewf, xewb, dskip, h0f, h0b)


def _pool_constants():
    bands, inv_cols = [], []
    t = np.arange(POOL_SUB)
    rt, ct = t // GRID_W, t % GRID_W
    for w in POOL_WINDOWS:
        hw = w // 2
        k = np.arange(POOL_SUB + GRID_W * w)
        rk, ck = k // GRID_W, k % GRID_W
        band = ((rk[None, :] >= rt[:, None]) & (rk[None, :] < rt[:, None] + w)
                & (ck[None, :] >= ct[:, None] - hw) & (ck[None, :] < ct[:, None] + hw))
        bands.append(jnp.asarray(band, BF16))
        cnt_c = np.minimum(ct + hw, GRID_W) - np.maximum(ct - hw, 0)
        inv_cols.append(np.broadcast_to((1.0 / cnt_c)[:, None], (POOL_SUB, 128)))
    return bands, jnp.asarray(np.stack(inv_cols), F32)


def _merge_kernel(yf_ref, yb_ref, z_ref, up_ref, upp_ref, upn_ref, xln_ref, ng_ref, pw_ref, ps_ref, wo_ref,
                  g1_ref, l1g_ref, l1b_ref, band0_ref, band1_ref, band2_ref, band3_ref, invc_ref,
                  out_ref, pw_scr, wo_scr, *, tm, rows_total):
    i = pl.program_id(0)
    n = pl.num_programs(0)

    @pl.when(i == 0)
    def _():
        pw_scr[...] = pw_ref[...].astype(BF16)
        wo_scr[...] = wo_ref[...].astype(BF16)

    y = yf_ref[...].astype(F32) + yb_ref[...].astype(F32)
    g = y * _silu(z_ref[...].astype(F32))
    yn = (g * lax.rsqrt(jnp.mean(g * g, axis=-1, keepdims=True) + LN_EPS) * ng_ref[...]).astype(BF16)

    c = POOL_GROUP_DIM
    band_refs = (band0_ref, band1_ref, band2_ref, band3_ref)
    keep_p = (i > 0).astype(BF16)
    keep_n = (i < n - 1).astype(BF16)
    sub_row = lax.broadcasted_iota(jnp.int32, (POOL_SUB, 128), 0) // GRID_W
    nsub = tm // POOL_SUB
    sums = []
    for gi, w in enumerate(POOL_WINDOWS):
        hw = w // 2
        cols = slice(gi * c, (gi + 1) * c)
        ext = jnp.concatenate([upp_ref[:, cols] * keep_p, up_ref[:, cols], upn_ref[:, cols] * keep_n], axis=0)
        starts = [POOL_HALO + b * POOL_SUB - hw * GRID_W for b in range(nsub)]
        sums.append([_dot(band_refs[gi][...], ext[s:s + POOL_SUB + GRID_W * w]) for s in starts])
    diffs = []
    for gi, w in enumerate(POOL_WINDOWS):
        hw = w // 2
        cols = slice(gi * c, (gi + 1) * c)
        parts = []
        for b in range(nsub):
            row = sub_row + (i * tm + b * POOL_SUB) // GRID_W
            cnt_r = jnp.minimum(row + hw, rows_total) - jnp.maximum(row - hw, 0)
            inv = invc_ref[gi] / cnt_r.astype(F32)
            u = up_ref[b * POOL_SUB:(b + 1) * POOL_SUB, cols].astype(F32)
            parts.append((sums[gi][b] * jnp.concatenate([inv, inv], axis=1) - u).astype(BF16))
        diffs.append(jnp.concatenate(parts, axis=0))
    p = [(_dot(diffs[gi], pw_scr[gi]) * ps_ref[:, gi * c:(gi + 1) * c]).astype(BF16)
         for gi in range(len(POOL_WINDOWS))]
    lhs = jnp.concatenate([yn] + p, axis=1)
    hr = tm // 2
    for r in range(0, tm, hr):
        mix = _dot(lhs[r:r + hr], wo_scr[...])
        out_ref[r:r + hr, :] = _layer_norm(DEEPNORM_ALPHA * xln_ref[r:r + hr, :] + g1_ref[...] * mix,
                                           l1g_ref[...], l1b_ref[...])


def _merge_call(yf, yb, z, up, xln, norm_g, pool_w, pool_scale, w_out, g1, l1g, l1b, *, tm):
    L = xln.shape[0]
    nt = L // tm
    hb = tm // POOL_HALO
    const = lambda i: (0, 0)
    const3 = lambda i: (0, 0, 0)
    row = lambda i: (i, 0)
    vec = pl.BlockSpec((1, D_MODEL), const)
    bands, inv_cols = _pool_constants()
    return pl.pallas_call(
        functools.partial(_merge_kernel, tm=tm, rows_total=L // GRID_W),
        grid=(nt,),
        in_specs=[pl.BlockSpec((tm, D_SSD), row), pl.BlockSpec((tm, D_SSD), row),
                  pl.BlockSpec((tm, D_SSD), row),
                  pl.BlockSpec((tm, D_POOL), row),
                  pl.BlockSpec((POOL_HALO, D_POOL), lambda i: (jnp.maximum(i * hb - 1, 0), 0)),
                  pl.BlockSpec((POOL_HALO, D_POOL),
                               lambda i: (jnp.minimum((i + 1) * hb, L // POOL_HALO - 1), 0)),
                  pl.BlockSpec((tm, D_MODEL), row),
                  vec,
                  pl.BlockSpec((None,) + pool_w.shape[1:], lambda i: (0, 0, 0, 0), pipeline_mode=pl.Buffered(1)),
                  vec,
                  pl.BlockSpec((None,) + w_out.shape[1:], const3, pipeline_mode=pl.Buffered(1)),
                  vec, vec, vec]
                 + [pl.BlockSpec(bm.shape, const) for bm in bands]
                 + [pl.BlockSpec(inv_cols.shape, const3)],
        out_specs=pl.BlockSpec((tm, D_MODEL), row),
        out_shape=jax.ShapeDtypeStruct((L, D_MODEL), F32),
        scratch_shapes=[pltpu.VMEM(pool_w.shape[1:], BF16), pltpu.VMEM(w_out.shape[1:], BF16)],
        compiler_params=pltpu.CompilerParams(dimension_semantics=("arbitrary",),
                                             vmem_limit_bytes=VMEM_LIMIT),
        name="merge",
    )(yf, yb, z, up, up, up, xln, norm_g, pool_w, pool_scale, w_out, g1, l1g, l1b, *bands, inv_cols)


def _round_weights(jobs, sem):
    steps = []
    for src, dst, stage, sem0 in jobs:
        rows = stage.shape[1]
        for k in range(dst.shape[0] // rows):
            slot = k % 2
            copy = pltpu.make_async_copy(src.at[0, pl.ds(k * rows, rows)], stage.at[slot], sem.at[sem0 + slot])
            steps.append((copy, stage, slot, dst, k * rows, rows))
    steps[0][0].start()
    for n, (copy, stage, slot, dst, r0, rows) in enumerate(steps):
        if n + 1 < len(steps):
            steps[n + 1][0].start()
        copy.wait()
        dst[r0:r0 + rows, :] = stage[slot].astype(BF16)


def _ffn_kernel(x_ref, sh_ref, sc_ref, g2_ref, wg_hbm, wu_hbm, wd_hbm, lg_ref, lb_ref, out_ref,
                wg_ref, wu_ref, wd_ref, stage_wide, stage_tall, sem):
    @pl.when(pl.program_id(0) == 0)
    def _():
        _round_weights([(wg_hbm, wg_ref, stage_wide, 0), (wu_hbm, wu_ref, stage_wide, 0),
                        (wd_hbm, wd_ref, stage_tall, 2)], sem)

    x = x_ref[...]
    h = (x * (1.0 + sc_ref[...]) + sh_ref[...]).astype(BF16)
    half = D_FF // 2
    ffn = None
    for s in (0, half):
        gate = _dot(h, wg_ref[:, s:s + half])
        upv = _dot(h, wu_ref[:, s:s + half])
        part = _dot((_silu(gate) * upv).astype(BF16), wd_ref[s:s + half, :])
        ffn = part if ffn is None else ffn + part
    out_ref[...] = _layer_norm(DEEPNORM_ALPHA * x + g2_ref[...] * ffn, lg_ref[...], lb_ref[...])


def _ffn_call(x, sh, sc, g2, wg, wu, wd, lg, lb, *, tm):
    L = x.shape[0]
    const = lambda i: (0, 0)
    row = lambda i: (i, 0)
    vec = pl.BlockSpec((1, D_MODEL), const)
    hbm = pl.BlockSpec(memory_space=pl.ANY)
    return pl.pallas_call(
        _ffn_kernel,
        grid=(L // tm,),
        in_specs=[pl.BlockSpec((tm, D_MODEL), row), vec, vec, vec, hbm, hbm, hbm, vec, vec],
        out_specs=pl.BlockSpec((tm, D_MODEL), row),
        out_shape=jax.ShapeDtypeStruct((L, D_MODEL), F32),
        scratch_shapes=[pltpu.VMEM(wg.shape[1:], BF16), pltpu.VMEM(wu.shape[1:], BF16),
                        pltpu.VMEM(wd.shape[1:], BF16),
                        pltpu.VMEM((2, D_MODEL // FFN_STAGE_CHUNKS, D_FF), F32),
                        pltpu.VMEM((2, D_FF // FFN_STAGE_CHUNKS, D_MODEL), F32),
                        pltpu.SemaphoreType.DMA((4,))],
        compiler_params=pltpu.CompilerParams(dimension_semantics=("arbitrary",),
                                             vmem_limit_bytes=VMEM_LIMIT),
        name="ffn",
    )(x, sh, sc, g2, wg, wu, wd, lg, lb)


def kernel(x, c, ctx, c_ctx, emb_ln_g, emb_ln_b, w_ada, b_ada, in_proj, conv_w, conv_b, dt_bias, a_log,
           d_skip, ssd_norm_g, pool_w, pool_scale, w_out, ln1_g, ln1_b, w_gate, w_up, w_down, ln2_g, ln2_b):
    assert x.shape[0] == 1 and w_ada.shape[0] == DEPTH == 1
    xl, xc = x[0], ctx[0]
    rowv = lambda v: v.reshape(1, -1)
    elg, elb = rowv(emb_ln_g), rowv(emb_ln_b)

    mod = _mod_call(jnp.stack([c[0], c_ctx], axis=1), w_ada, rowv(b_ada[0]))
    sh1, sc1, g1, sh2, sc2, g2 = [mod[0:1, k * D_MODEL:(k + 1) * D_MODEL] for k in range(6)]
    sh1c, sc1c = mod[1:2, 0:D_MODEL], mod[1:2, D_MODEL:2 * D_MODEL]

    conv_args = (conv_w, rowv(conv_b[0]), dt_bias[0].reshape(N_DH, 1), a_log[0].reshape(N_DH, 1))

    xa, xewf, xewb = _expand_matrices()
    dskip = rowv(jnp.repeat(d_skip[0], SSD_HEAD_DIM))
    h_zero = jnp.zeros((D_STATE, D_SSD), F32)

    _, xbc_c, _, _, rp_c, cs_c, wz, wxd, wp = _inproj_call(xc, elg, elb, sh1c, sc1c, jnp.swapaxes(in_proj, 1, 2),
                                                           *conv_args, tm=xc.shape[0])
    _, _, hf_ctx, hb_ctx = _ssd_call(xbc_c, rp_c, cs_c, xa, xewf, xewb, dskip, h_zero, h_zero)

    z, xbc, up, xln, rp, cs = _inproj_call(xl, elg, elb, sh1, sc1, (wz, wxd, wp), *conv_args, tm=512)
    yf, yb, _, _ = _ssd_call(xbc, rp, cs, xa, xewf, xewb, dskip, hf_ctx, hb_ctx)
    x1 = _merge_call(yf, yb, z, up, xln, rowv(ssd_norm_g[0]), pool_w, rowv(pool_scale[0]),
                     w_out, g1, rowv(ln1_g[0]), rowv(ln1_b[0]), tm=512)
    x2 = _ffn_call(x1, sh2, sc2, g2, w_gate, w_up, w_down, rowv(ln2_g[0]), rowv(ln2_b[0]), tm=512)
    return x2[None]
```

```python
import functools

import jax
import jax.numpy as jnp
import numpy as np
from jax import lax
from jax.experimental import pallas as pl
from jax.experimental.pallas import tpu as pltpu

F32 = jnp.float32
BF16 = jnp.bfloat16

D_MODEL = 1024
SSD_HEADS = 16
SSD_HEAD_DIM = 64
D_SSD = SSD_HEADS * SSD_HEAD_DIM
D_STATE = 128
D_CONV = 5
CHUNK = 128
D_POOL = 1024
POOL_WINDOWS = (2, 4, 8, 16)
POOL_GROUP_DIM = D_POOL // len(POOL_WINDOWS)
GRID_W = 64
D_XBC = D_SSD + 2 * D_STATE
D_XD = D_XBC + 128
D_FF = 2816
DEPTH = 1
DEEPNORM_ALPHA = (2 * DEPTH) ** 0.25
LN_EPS = 1e-5

HALO = 16
POOL_HALO = 512
POOL_SUB = 256
SSD_BLOCK = 512
PROJ_BLOCK = 256
CONV_BLOCK = 256
MOD_KB = 256
FFN_STAGE_CHUNKS = 8
N_DH = 2 * SSD_HEADS
VMEM_LIMIT = 56 * 1024 * 1024


def _dot(a, b):
    return jnp.dot(a, b, preferred_element_type=F32)


def _dot_nt(a, b):
    return lax.dot_general(a, b, (((1,), (1,)), ((), ())), preferred_element_type=F32)


def _silu(x):
    hx = 0.5 * x
    return hx + hx * jnp.tanh(hx)


def _layer_norm(x, g, b):
    mu = jnp.mean(x, axis=-1, keepdims=True)
    xc = x - mu
    var = jnp.mean(xc * xc, axis=-1, keepdims=True)
    return xc * lax.rsqrt(var + LN_EPS) * g + b


def _bf16_part(v):
    return v.astype(BF16).astype(F32)


def _mod_kernel(cc_ref, w_ref, b_ref, out_ref):
    k = pl.program_id(0)

    @pl.when(k == 0)
    def _():
        out_ref[...] = jnp.broadcast_to(b_ref[...], out_ref.shape)

    s = _silu(cc_ref[...])
    w = w_ref[...]
    out_ref[0:1, :] += jnp.sum(w * s[:, 0:1], axis=0, keepdims=True)
    out_ref[1:2, :] += jnp.sum(w * s[:, 1:2], axis=0, keepdims=True)


def _mod_call(cc, w_ada, b_ada):
    n = w_ada.shape[-1]
    return pl.pallas_call(
        _mod_kernel,
        grid=(D_MODEL // MOD_KB,),
        in_specs=[pl.BlockSpec((MOD_KB, 2), lambda k: (k, 0)),
                  pl.BlockSpec((None, MOD_KB, n), lambda k: (0, k, 0)),
                  pl.BlockSpec((1, n), lambda k: (0, 0))],
        out_specs=pl.BlockSpec((8, n), lambda k: (0, 0)),
        out_shape=jax.ShapeDtypeStruct((8, n), F32),
        compiler_params=pltpu.CompilerParams(dimension_semantics=("arbitrary",),
                                             vmem_limit_bytes=VMEM_LIMIT),
        name="mod",
    )(cc, w_ada, b_ada)


def _inproj_kernel(x_ref, xp_ref, xn_ref, lng_ref, lnb_ref, sh_ref, sc_ref, wz_ref, wxd_ref, wp_ref,
                   cw_ref, cb_ref, dtb_ref, alog_ref,
                   z_ref, xbc_ref, up_ref, xln_ref, rowpack_ref, colsplit_ref, h_scr, *, tm):
    i = pl.program_id(0)
    n = pl.num_programs(0)
    lng, lnb, sh, sc = lng_ref[...], lnb_ref[...], sh_ref[...], sc_ref[...]

    def modulated(xn):
        return xn * (1.0 + sc) + sh

    xln = _layer_norm(x_ref[...], lng, lnb)
    xln_ref[...] = xln
    h = modulated(xln).astype(BF16)
    hp = (modulated(_layer_norm(xp_ref[...], lng, lnb)) * (i > 0).astype(F32)).astype(BF16)
    hn = (modulated(_layer_norm(xn_ref[...], lng, lnb)) * (i < n - 1).astype(F32)).astype(BF16)

    h_scr[0:HALO, :] = hp
    h_scr[HALO:HALO + tm, :] = h
    h_scr[HALO + tm:HALO + tm + HALO, :] = hn
    dt_raw = _dot(h_scr[HALO:HALO + tm, :], wxd_ref[:, D_XBC:D_XD]).T[0:N_DH, :] + dtb_ref[...]
    dt = jnp.maximum(dt_raw, 0.0) + jnp.log(1.0 + jnp.exp(-jnp.abs(dt_raw)))
    a = dt * (-jnp.exp(alog_ref[...]))
    lane = lax.broadcasted_iota(jnp.int32, (N_DH, tm), 1) & (CHUNK - 1)
    row = lax.broadcasted_iota(jnp.int32, (N_DH, tm), 0)
    cf, cr = a, a
    k = 1
    while k < CHUNK:
        cf = cf + jnp.where(lane >= k, pltpu.roll(cf, k, 1), 0.0)
        cr = cr + jnp.where(lane < CHUNK - k, pltpu.roll(cr, tm - k, 1), 0.0)
        k *= 2
    is_fwd = row < SSD_HEADS
    acum = jnp.where(is_fwd, cf, cr)
    e = jnp.exp(acum)
    w_end = jnp.exp(jnp.where(is_fwd, cr, cf) - a) * dt
    rowpack_ref[...] = jnp.concatenate([acum, dt], axis=0)

    a1 = _bf16_part(acum); a2 = _bf16_part(acum - a1); a3 = _bf16_part(acum - a1 - a2)
    e1 = _bf16_part(e); e2 = _bf16_part(e - e1); e3 = _bf16_part(e - e1 - e2)
    w1 = _bf16_part(w_end); w2 = _bf16_part(w_end - w1)
    top = jnp.concatenate([a1, a2, a3, e1], axis=0)
    bot = jnp.concatenate([e2, e3, w1, w2], axis=0)
    colsplit_ref[...] = jnp.concatenate([top.T, bot.T], axis=1).astype(BF16)

    ne = tm + 2 * HALO
    nb, zb = CONV_BLOCK, PROJ_BLOCK
    zu_blocks = [(w_ref, o_ref, c) for w_ref, o_ref in ((wz_ref, z_ref), (wp_ref, up_ref))
                 for c in range(0, D_SSD, zb)]
    for j, c0 in enumerate(range(0, D_XBC, nb)):
        cols = slice(c0, c0 + nb)
        u = _dot(h_scr[...], wxd_ref[:, cols])
        take = -(-len(zu_blocks) // (D_XBC // nb - j))
        for w_ref, o_ref, c in zu_blocks[:take]:
            o_ref[:, c:c + zb] = _dot(h_scr[HALO:HALO + tm, :], w_ref[:, c:c + zb]).astype(BF16)
        zu_blocks = zu_blocks[take:]
        f = [cw_ref[k:k + 1, cols] * u for k in range(D_CONV)]
        after = pltpu.roll(f[3] + pltpu.roll(f[4], ne - 1, 0), ne - 1, 0)
        before = pltpu.roll(f[1] + pltpu.roll(f[0], 1, 0), 1, 0)
        acc = (f[2] + after + before)[HALO:HALO + tm] + cb_ref[:, cols]
        xbc_ref[:, cols] = _silu(acc).astype(BF16)


def _inproj_convert_kernel(x_ref, xp_ref, xn_ref, lng_ref, lnb_ref, sh_ref, sc_ref, wf_ref,
                           cw_ref, cb_ref, dtb_ref, alog_ref,
                           z_ref, xbc_ref, up_ref, xln_ref, rowpack_ref, colsplit_ref,
                           wz_ref, wxd_ref, wp_ref, h_scr, *, tm):
    @pl.when(pl.program_id(0) == 0)
    def _():
        wz_ref[...] = wf_ref[0:D_SSD, :].T.astype(BF16)
        wxd_ref[...] = wf_ref[D_SSD:D_SSD + D_XD, :].T.astype(BF16)
        o3 = D_SSD + D_XBC + N_DH
        wp_ref[...] = wf_ref[o3:o3 + D_POOL, :].T.astype(BF16)

    _inproj_kernel(x_ref, xp_ref, xn_ref, lng_ref, lnb_ref, sh_ref, sc_ref, wz_ref, wxd_ref, wp_ref,
                   cw_ref, cb_ref, dtb_ref, alog_ref,
                   z_ref, xbc_ref, up_ref, xln_ref, rowpack_ref, colsplit_ref, h_scr, tm=tm)


def _inproj_call(x, ln_g, ln_b, sh, sc, weights, conv_w, conv_b, dt_bias, a_log, *, tm):
    L = x.shape[0]
    nt = L // tm
    hb = tm // HALO
    const = lambda i: (0, 0)
    row = lambda i: (i, 0)
    w_shapes = [(D_MODEL, D_SSD), (D_MODEL, D_XD), (D_MODEL, D_POOL)]
    w_specs = [pl.BlockSpec(s, const) for s in w_shapes]
    convert = not isinstance(weights, tuple)
    if convert:
        body = _inproj_convert_kernel
        w_in_specs = [pl.BlockSpec((None,) + weights.shape[1:], lambda i: (0, 0, 0),
                                   pipeline_mode=pl.Buffered(1))]
        w_args = (weights,)
    else:
        body, w_args = _inproj_kernel, weights
        w_in_specs = [pl.BlockSpec(s, const, pipeline_mode=pl.Buffered(1)) for s in w_shapes]
    return pl.pallas_call(
        functools.partial(body, tm=tm),
        grid=(nt,),
        in_specs=[pl.BlockSpec((tm, D_MODEL), row),
                  pl.BlockSpec((HALO, D_MODEL), lambda i: (jnp.maximum(i * hb - 1, 0), 0)),
                  pl.BlockSpec((HALO, D_MODEL), lambda i: (jnp.minimum((i + 1) * hb, L // HALO - 1), 0)),
                  pl.BlockSpec((1, D_MODEL), const), pl.BlockSpec((1, D_MODEL), const),
                  pl.BlockSpec((1, D_MODEL), const), pl.BlockSpec((1, D_MODEL), const)]
                 + w_in_specs
                 + [pl.BlockSpec((None, D_CONV, D_XBC), lambda i: (0, 0, 0)), pl.BlockSpec((1, D_XBC), const),
                    pl.BlockSpec((N_DH, 1), const), pl.BlockSpec((N_DH, 1), const)],
        out_specs=[pl.BlockSpec((tm, D_SSD), row),
                   pl.BlockSpec((tm, D_XBC), row),
                   pl.BlockSpec((tm, D_POOL), row),
                   pl.BlockSpec((tm, D_MODEL), row),
                   pl.BlockSpec((2 * N_DH, tm), lambda i: (0, i)),
                   pl.BlockSpec((tm, 256), row)] + (w_specs if convert else []),
        out_shape=[jax.ShapeDtypeStruct((L, D_SSD), BF16),
                   jax.ShapeDtypeStruct((L, D_XBC), BF16),
                   jax.ShapeDtypeStruct((L, D_POOL), BF16),
                   jax.ShapeDtypeStruct((L, D_MODEL), F32),
                   jax.ShapeDtypeStruct((2 * N_DH, L), F32),
                   jax.ShapeDtypeStruct((L, 256), BF16)]
                  + ([jax.ShapeDtypeStruct(s, BF16) for s in w_shapes] if convert else []),
        scratch_shapes=[pltpu.VMEM((tm + 2 * HALO, D_MODEL), BF16)],
        compiler_params=pltpu.CompilerParams(dimension_semantics=("arbitrary",),
                                             vmem_limit_bytes=VMEM_LIMIT),
        name="inproj",
    )(x, x, x, ln_g, ln_b, sh, sc, *w_args, conv_w, conv_b, dt_bias, a_log)


def _expand_matrices():
    xa = np.zeros((256, N_DH * CHUNK), np.float32)
    for dh in range(N_DH):
        for part in range(3):
            xa[part * N_DH + dh, dh * CHUNK:(dh + 1) * CHUNK] = 1.0
    xew = np.zeros((2, 256, 2 * D_SSD), np.float32)
    for d in range(2):
        for hh in range(SSD_HEADS):
            dh = d * SSD_HEADS + hh
            cols = slice(hh * SSD_HEAD_DIM, (hh + 1) * SSD_HEAD_DIM)
            for part in (3, 4, 5):
                xew[d, part * N_DH + dh, cols] = 1.0
            cols = slice(D_SSD + hh * SSD_HEAD_DIM, D_SSD + (hh + 1) * SSD_HEAD_DIM)
            for part in (6, 7):
                xew[d, part * N_DH + dh, cols] = 1.0
    return jnp.asarray(xa, BF16), jnp.asarray(xew[0], BF16), jnp.asarray(xew[1], BF16)


def _ssd_kernel(xbcf_ref, xbcb_ref, rp_ref, csf_ref, csb_ref, xa_ref, xewf_ref, xewb_ref, dskip_ref,
                h0f_ref, h0b_ref, yf_ref, yb_ref, hf_ref, hb_ref):
    q = CHUNK
    nsub = xbcf_ref.shape[0] // CHUNK

    @pl.when(pl.program_id(0) == 0)
    def _():
        hf_ref[...] = h0f_ref[...]
        hb_ref[...] = h0b_ref[...]

    def stream(xbc_ref, cs, rows, xew_ref, h_ref, exit_row, emit):
        xs = xbc_ref[rows, 0:D_SSD]
        bm = xbc_ref[rows, D_SSD:D_SSD + D_STATE]
        cm = xbc_ref[rows, D_SSD + D_STATE:D_XBC]
        ew = _dot(cs, xew_ref[...])
        e, w_end = ew[:, 0:D_SSD], ew[:, D_SSD:]
        yield
        st = h_ref[...]
        emit(_dot(cm, st.astype(BF16)) * e)
        yield
        xw = (xs.astype(F32) * w_end).astype(BF16)
        upd = lax.dot_general(bm, xw, (((0,), (0,)), ((), ())), preferred_element_type=F32)
        h_ref[...] = st * e[exit_row:exit_row + 1, :] + upd
        yield

    def lead(sub, out):
        fr = slice(sub * q, (sub + 1) * q)
        br = slice((nsub - 1 - sub) * q, (nsub - sub) * q)
        csf = csf_ref[fr, :]

        def store_back(v):
            yb_ref[br, :] = v.astype(BF16)

        yield from stream(xbcf_ref, csf, fr, xewf_ref, hf_ref, q - 1, lambda v: out.update(y_off_f=v))
        yield from stream(xbcb_ref, csb_ref[br, :], br, xewb_ref, hb_ref, 0, store_back)
        half = SSD_HEADS * q
        out.update(acol_f=_dot(csf, xa_ref[:, 0:half]))
        yield
        out.update(acol_b=_dot(csf, xa_ref[:, half:]),
                   cb=_dot_nt(xbcf_ref[fr, D_SSD + D_STATE:D_XBC], xbcf_ref[fr, D_SSD:D_SSD + D_STATE]))
        yield

    ti = lax.broadcasted_iota(jnp.int32, (q, q), 0)
    ui = lax.broadcasted_iota(jnp.int32, (q, q), 1)
    lane = lax.broadcasted_iota(jnp.int32, (q, 2 * SSD_HEAD_DIM), 1)

    def intra(sub, j, pre):
        fr = slice(sub * q, (sub + 1) * q)
        rp = rp_ref[:, fr]
        cb = pre["cb"]
        ms = []
        for hh in (2 * j, 2 * j + 1):
            hb_ = SSD_HEADS + hh
            seg_f = pre["acol_f"][:, hh * q:(hh + 1) * q] - rp[hh:hh + 1, :]
            seg_b = pre["acol_b"][:, hh * q:(hh + 1) * q] - rp[hb_:hb_ + 1, :]
            dtf = rp[N_DH + hh:N_DH + hh + 1, :]
            dtb = rp[N_DH + hb_:N_DH + hb_ + 1, :]
            coef = jnp.where(ui < ti, dtf, jnp.where(ui > ti, dtb, dtf + dtb))
            ms.append((cb * coef * jnp.exp(jnp.where(ui <= ti, seg_f, seg_b))).astype(BF16))
        cols = slice(j * 2 * SSD_HEAD_DIM, (j + 1) * 2 * SSD_HEAD_DIM)
        xp = xbcf_ref[fr, cols]
        zero = jnp.zeros_like(xp)
        rhs = jnp.concatenate([jnp.where(lane < SSD_HEAD_DIM, xp, zero),
                               jnp.where(lane >= SSD_HEAD_DIM, xp, zero)], axis=0)
        y = (_dot(jnp.concatenate(ms, axis=1), rhs) + pre["y_off_f"][:, cols]
             + dskip_ref[:, cols] * xp.astype(F32))
        yf_ref[fr, cols] = y.astype(BF16)

    pre = [dict() for _ in range(nsub)]
    for _ in lead(0, pre[0]):
        pass
    for sub in range(nsub):
        nxt = lead(sub + 1, pre[sub + 1]) if sub + 1 < nsub else iter(())
        for j in range(SSD_HEADS // 2):
            intra(sub, j, pre[sub])
            next(nxt, None)
        for _ in nxt:
            pass


def _ssd_call(xbc, rowpack, colsplit, xa, xewf, xewb, dskip, h0f, h0b):
    L = xbc.shape[0]
    blk = min(SSD_BLOCK, L)
    nc = L // blk
    const = lambda s: (0, 0)
    fwd = lambda s: (s, 0)
    bwd = lambda s: (nc - 1 - s, 0)
    st_shape = jax.ShapeDtypeStruct((D_STATE, D_SSD), F32)
    return pl.pallas_call(
        _ssd_kernel,
        grid=(nc,),
        in_specs=[pl.BlockSpec((blk, D_XBC), fwd), pl.BlockSpec((blk, D_XBC), bwd),
                  pl.BlockSpec((2 * N_DH, blk), lambda s: (0, s)),
                  pl.BlockSpec((blk, 256), fwd), pl.BlockSpec((blk, 256), bwd),
                  pl.BlockSpec(xa.shape, const), pl.BlockSpec(xewf.shape, const),
                  pl.BlockSpec(xewb.shape, const), pl.BlockSpec((1, D_SSD), const),
                  pl.BlockSpec((D_STATE, D_SSD), const), pl.BlockSpec((D_STATE, D_SSD), const)],
        out_specs=[pl.BlockSpec((blk, D_SSD), fwd), pl.BlockSpec((blk, D_SSD), bwd),
                   pl.BlockSpec((D_STATE, D_SSD), const), pl.BlockSpec((D_STATE, D_SSD), const)],
        out_shape=[jax.ShapeDtypeStruct((L, D_SSD), BF16), jax.ShapeDtypeStruct((L, D_SSD), BF16),
                   st_shape, st_shape],
        compiler_params=pltpu.CompilerParams(dimension_semantics=("arbitrary",),
                                             vmem_limit_bytes=VMEM_LIMIT),
        name="ssd",
    )(xbc, xbc, rowpack, colsplit, colsplit, xa, xewf, xewb, dskip, h0f, h0b)


def _pool_constants():
    bands, inv_cols = [], []
    t = np.arange(POOL_SUB)
    rt, ct = t // GRID_W, t % GRID_W
    for w in POOL_WINDOWS:
        hw = w // 2
        k = np.arange(POOL_SUB + GRID_W * w)
        rk, ck = k // GRID_W, k % GRID_W
        band = ((rk[None, :] >= rt[:, None]) & (rk[None, :] < rt[:, None] + w)
                & (ck[None, :] >= ct[:, None] - hw) & (ck[None, :] < ct[:, None] + hw))
        bands.append(jnp.asarray(band, BF16))
        cnt_c = np.minimum(ct + hw, GRID_W) - np.maximum(ct - hw, 0)
        inv_cols.append(np.broadcast_to((1.0 / cnt_c)[:, None], (POOL_SUB, 128)))
    return bands, jnp.asarray(np.stack(inv_cols), F32)


def _merge_kernel(yf_ref, yb_ref, z_ref, up_ref, upp_ref, upn_ref, xln_ref, ng_ref, pw_ref, ps_ref, wo_ref,
                  g1_ref, l1g_ref, l1b_ref, band0_ref, band1_ref, band2_ref, band3_ref, invc_ref,
                  out_ref, pw_scr, wo_scr, *, tm, rows_total):
    i = pl.program_id(0)
    n = pl.num_programs(0)

    @pl.when(i == 0)
    def _():
        pw_scr[...] = pw_ref[...].astype(BF16)
        wo_scr[...] = wo_ref[...].astype(BF16)

    y = yf_ref[...].astype(F32) + yb_ref[...].astype(F32)
    g = y * _silu(z_ref[...].astype(F32))
    yn = (g * lax.rsqrt(jnp.mean(g * g, axis=-1, keepdims=True) + LN_EPS) * ng_ref[...]).astype(BF16)

    c = POOL_GROUP_DIM
    band_refs = (band0_ref, band1_ref, band2_ref, band3_ref)
    keep_p = (i > 0).astype(BF16)
    keep_n = (i < n - 1).astype(BF16)
    sub_row = lax.broadcasted_iota(jnp.int32, (POOL_SUB, 128), 0) // GRID_W
    nsub = tm // POOL_SUB
    sums = []
    for gi, w in enumerate(POOL_WINDOWS):
        hw = w // 2
        cols = slice(gi * c, (gi + 1) * c)
        ext = jnp.concatenate([upp_ref[:, cols] * keep_p, up_ref[:, cols], upn_ref[:, cols] * keep_n], axis=0)
        starts = [POOL_HALO + b * POOL_SUB - hw * GRID_W for b in range(nsub)]
        sums.append([_dot(band_refs[gi][...], ext[s:s + POOL_SUB + GRID_W * w]) for s in starts])
    diffs = []
    for gi, w in enumerate(POOL_WINDOWS):
        hw = w // 2
        cols = slice(gi * c, (gi + 1) * c)
        parts = []
        for b in range(nsub):
            row = sub_row + (i * tm + b * POOL_SUB) // GRID_W
            cnt_r = jnp.minimum(row + hw, rows_total) - jnp.maximum(row - hw, 0)
            inv = invc_ref[gi] / cnt_r.astype(F32)
            u = up_ref[b * POOL_SUB:(b + 1) * POOL_SUB, cols].astype(F32)
            parts.append((sums[gi][b] * jnp.concatenate([inv, inv], axis=1) - u).astype(BF16))
        diffs.append(jnp.concatenate(parts, axis=0))
    p = [(_dot(diffs[gi], pw_scr[gi]) * ps_ref[:, gi * c:(gi + 1) * c]).astype(BF16)
         for gi in range(len(POOL_WINDOWS))]
    lhs = jnp.concatenate([yn] + p, axis=1)
    hr = tm // 2
    for r in range(0, tm, hr):
        mix = _dot(lhs[r:r + hr], wo_scr[...])
        out_ref[r:r + hr, :] = _layer_norm(DEEPNORM_ALPHA * xln_ref[r:r + hr, :] + g1_ref[...] * mix,
                                           l1g_ref[...], l1b_ref[...])


def _merge_call(yf, yb, z, up, xln, norm_g, pool_w, pool_scale, w_out, g1, l1g, l1b, *, tm):
    L = xln.shape[0]
    nt = L // tm
    hb = tm // POOL_HALO
    const = lambda i: (0, 0)
    const3 = lambda i: (0, 0, 0)
    row = lambda i: (i, 0)
    vec = pl.BlockSpec((1, D_MODEL), const)
    bands, inv_cols = _pool_constants()
    return pl.pallas_call(
        functools.partial(_merge_kernel, tm=tm, rows_total=L // GRID_W),
        grid=(nt,),
        in_specs=[pl.BlockSpec((tm, D_SSD), row), pl.BlockSpec((tm, D_SSD), row),
                  pl.BlockSpec((tm, D_SSD), row),
                  pl.BlockSpec((tm, D_POOL), row),
                  pl.BlockSpec((POOL_HALO, D_POOL), lambda i: (jnp.maximum(i * hb - 1, 0), 0)),
                  pl.BlockSpec((POOL_HALO, D_POOL),
                               lambda i: (jnp.minimum((i + 1) * hb, L // POOL_HALO - 1), 0)),
                  pl.BlockSpec((tm, D_MODEL), row),
                  vec,
                  pl.BlockSpec((None,) + pool_w.shape[1:], lambda i: (0, 0, 0, 0), pipeline_mode=pl.Buffered(1)),
                  vec,
                  pl.BlockSpec((None,) + w_out.shape[1:], const3, pipeline_mode=pl.Buffered(1)),
                  vec, vec, vec]
                 + [pl.BlockSpec(bm.shape, const) for bm in bands]
                 + [pl.BlockSpec(inv_cols.shape, const3)],
        out_specs=pl.BlockSpec((tm, D_MODEL), row),
        out_shape=jax.ShapeDtypeStruct((L, D_MODEL), F32),
        scratch_shapes=[pltpu.VMEM(pool_w.shape[1:], BF16), pltpu.VMEM(w_out.shape[1:], BF16)],
        compiler_params=pltpu.CompilerParams(dimension_semantics=("arbitrary",),
                                             vmem_limit_bytes=VMEM_LIMIT),
        name="merge",
    )(yf, yb, z, up, up, up, xln, norm_g, pool_w, pool_scale, w_out, g1, l1g, l1b, *bands, inv_cols)


def _round_weights(jobs, sem):
    steps = []
    for src, dst, stage, sem0 in jobs:
        rows = stage.shape[1]
        for k in range(dst.shape[0] // rows):
            slot = k % 2
            copy = pltpu.make_async_copy(src.at[0, pl.ds(k * rows, rows)], stage.at[slot], sem.at[sem0 + slot])
            steps.append((copy, stage, slot, dst, k * rows, rows))
    steps[0][0].start()
    for n, (copy, stage, slot, dst, r0, rows) in enumerate(steps):
        if n + 1 < len(steps):
            steps[n + 1][0].start()
        copy.wait()
        dst[r0:r0 + rows, :] = stage[slot].astype(BF16)


def _ffn_kernel(x_ref, sh_ref, sc_ref, g2_ref, wg_hbm, wu_hbm, wd_hbm, lg_ref, lb_ref, out_ref,
                wg_ref, wu_ref, wd_ref, stage_wide, stage_tall, sem):
    @pl.when(pl.program_id(0) == 0)
    def _():
        _round_weights([(wg_hbm, wg_ref, stage_wide, 0), (wu_hbm, wu_ref, stage_wide, 0),
                        (wd_hbm, wd_ref, stage_tall, 2)], sem)

    x = x_ref[...]
    h = (x * (1.0 + sc_ref[...]) + sh_ref[...]).astype(BF16)
    half = D_FF // 2
    ffn = None
    for s in (0, half):
        gate = _dot(h, wg_ref[:, s:s + half])
        upv = _dot(h, wu_ref[:, s:s + half])
        part = _dot((_silu(gate) * upv).astype(BF16), wd_ref[s:s + half, :])
        ffn = part if ffn is None else ffn + part
    out_ref[...] = _layer_norm(DEEPNORM_ALPHA * x + g2_ref[...] * ffn, lg_ref[...], lb_ref[...])


def _ffn_call(x, sh, sc, g2, wg, wu, wd, lg, lb, *, tm):
    L = x.shape[0]
    const = lambda i: (0, 0)
    row = lambda i: (i, 0)
    vec = pl.BlockSpec((1, D_MODEL), const)
    hbm = pl.BlockSpec(memory_space=pl.ANY)
    return pl.pallas_call(
        _ffn_kernel,
        grid=(L // tm,),
        in_specs=[pl.BlockSpec((tm, D_MODEL), row), vec, vec, vec, hbm, hbm, hbm, vec, vec],
        out_specs=pl.BlockSpec((tm, D_MODEL), row),
        out_shape=jax.ShapeDtypeStruct((L, D_MODEL), F32),
        scratch_shapes=[pltpu.VMEM(wg.shape[1:], BF16), pltpu.VMEM(wu.shape[1:], BF16),
                        pltpu.VMEM(wd.shape[1:], BF16),
                        pltpu.VMEM((2, D_MODEL // FFN_STAGE_CHUNKS, D_FF), F32),
                        pltpu.VMEM((2, D_FF // FFN_STAGE_CHUNKS, D_MODEL), F32),
                        pltpu.SemaphoreType.DMA((4,))],
        compiler_params=pltpu.CompilerParams(dimension_semantics=("arbitrary",),
                                             vmem_limit_bytes=VMEM_LIMIT),
        name="ffn",
    )(x, sh, sc, g2, wg, wu, wd, lg, lb)


def kernel(x, c, ctx, c_ctx, emb_ln_g, emb_ln_b, w_ada, b_ada, in_proj, conv_w, conv_b, dt_bias, a_log,
           d_skip, ssd_norm_g, pool_w, pool_scale, w_out, ln1_g, ln1_b, w_gate, w_up, w_down, ln2_g, ln2_b):
    assert x.shape[0] == 1 and w_ada.shape[0] == DEPTH == 1
    xl, xc = x[0], ctx[0]
    rowv = lambda v: v.reshape(1, -1)
    elg, elb = rowv(emb_ln_g), rowv(emb_ln_b)

    mod = _mod_call(jnp.stack([c[0], c_ctx], axis=1), w_ada, rowv(b_ada[0]))
    sh1, sc1, g1, sh2, sc2, g2 = [mod[0:1, k * D_MODEL:(k + 1) * D_MODEL] for k in range(6)]
    sh1c, sc1c = mod[1:2, 0:D_MODEL], mod[1:2, D_MODEL:2 * D_MODEL]

    conv_args = (conv_w, rowv(conv_b[0]), dt_bias[0].reshape(N_DH, 1), a_log[0].reshape(N_DH, 1))

    xa, xewf, xewb = _expand_matrices()
    dskip = rowv(jnp.repeat(d_skip[0], SSD_HEAD_DIM))
    h_zero = jnp.zeros((D_STATE, D_SSD), F32)

    _, xbc_c, _, _, rp_c, cs_c, wz, wxd, wp = _inproj_call(xc, elg, elb, sh1c, sc1c, jnp.swapaxes(in_proj, 1, 2),
                                                           *conv_args, tm=xc.shape[0])
    _, _, hf_ctx, hb_ctx = _ssd_call(xbc_c, rp_c, cs_c, xa, xewf, xewb, dskip, h_zero, h_zero)

    z, xbc, up, xln, rp, cs = _inproj_call(xl, elg, elb, sh1, sc1, (wz, wxd, wp), *conv_args, tm=1024)
    yf, yb, _, _ = _ssd_call(xbc, rp, cs, xa, xewf, xewb, dskip, hf_ctx, hb_ctx)
    x1 = _merge_call(yf, yb, z, up, xln, rowv(ssd_norm_g[0]), pool_w, rowv(pool_scale[0]),
                     w_out, g1, rowv(ln1_g[0]), rowv(ln1_b[0]), tm=512)
    x2 = _ffn_call(x1, sh2, sc2, g2, w_gate, w_up, w_down, rowv(ln2_g[0]), rowv(ln2_b[0]), tm=512)
    return x2[None]
```

```python
import functools

import jax
import jax.numpy as jnp
import numpy as np
from jax import lax
from jax.experimental import pallas as pl
from jax.experimental.pallas import tpu as pltpu

F32 = jnp.float32
BF16 = jnp.bfloat16

D_MODEL = 1024
SSD_HEADS = 16
SSD_HEAD_DIM = 64
D_SSD = SSD_HEADS * SSD_HEAD_DIM
D_STATE = 128
D_CONV = 5
CHUNK = 128
D_POOL = 1024
POOL_WINDOWS = (2, 4, 8, 16)
POOL_GROUP_DIM = D_POOL // len(POOL_WINDOWS)
GRID_W = 64
D_XBC = D_SSD + 2 * D_STATE
D_XD = D_XBC + 128
D_FF = 2816
DEPTH = 1
DEEPNORM_ALPHA = (2 * DEPTH) ** 0.25
LN_EPS = 1e-5
LOG2_E = 1.4426950408889634

HALO = 16
POOL_HALO = 512
POOL_SUB = 256
SSD_BLOCK = 512
PROJ_BLOCK = 256
CONV_BLOCK = 256
MOD_KB = 256
FFN_STAGE_CHUNKS = 8
N_DH = 2 * SSD_HEADS
VMEM_LIMIT = 56 * 1024 * 1024


def _dot(a, b):
    return jnp.dot(a, b, preferred_element_type=F32)


def _dot_nt(a, b):
    return lax.dot_general(a, b, (((1,), (1,)), ((), ())), preferred_element_type=F32)


def _silu(x):
    hx = 0.5 * x
    return hx + hx * jnp.tanh(hx)


def _layer_norm(x, g, b):
    mu = jnp.mean(x, axis=-1, keepdims=True)
    xc = x - mu
    var = jnp.mean(xc * xc, axis=-1, keepdims=True)
    return xc * lax.rsqrt(var + LN_EPS) * g + b


def _mod_kernel(cc_ref, w_ref, b_ref, out_ref):
    k = pl.program_id(0)

    @pl.when(k == 0)
    def _():
        out_ref[...] = jnp.broadcast_to(b_ref[...], out_ref.shape)

    s = _silu(cc_ref[...])
    w = w_ref[...]
    out_ref[0:1, :] += jnp.sum(w * s[:, 0:1], axis=0, keepdims=True)
    out_ref[1:2, :] += jnp.sum(w * s[:, 1:2], axis=0, keepdims=True)


def _mod_call(cc, w_ada, b_ada):
    n = w_ada.shape[-1]
    return pl.pallas_call(
        _mod_kernel,
        grid=(D_MODEL // MOD_KB,),
        in_specs=[pl.BlockSpec((MOD_KB, 2), lambda k: (k, 0)),
                  pl.BlockSpec((None, MOD_KB, n), lambda k: (0, k, 0)),
                  pl.BlockSpec((1, n), lambda k: (0, 0))],
        out_specs=pl.BlockSpec((8, n), lambda k: (0, 0)),
        out_shape=jax.ShapeDtypeStruct((8, n), F32),
        compiler_params=pltpu.CompilerParams(dimension_semantics=("arbitrary",),
                                             vmem_limit_bytes=VMEM_LIMIT),
        name="mod",
    )(cc, w_ada, b_ada)


def _inproj_kernel(x_ref, xp_ref, xn_ref, lng_ref, lnb_ref, sh_ref, sc_ref, wz_ref, wxd_ref, wp_ref,
                   cw_ref, cb_ref, dtb_ref, alog_ref,
                   z_ref, xbc_ref, up_ref, xln_ref, rowpack_ref, colsplit_ref, h_scr, *, tm):
    i = pl.program_id(0)
    n = pl.num_programs(0)
    lng, lnb, sh, sc = lng_ref[...], lnb_ref[...], sh_ref[...], sc_ref[...]

    def modulated(xn):
        return xn * (1.0 + sc) + sh

    xln = _layer_norm(x_ref[...], lng, lnb)
    xln_ref[...] = xln
    h = modulated(xln).astype(BF16)
    hp = (modulated(_layer_norm(xp_ref[...], lng, lnb)) * (i > 0).astype(F32)).astype(BF16)
    hn = (modulated(_layer_norm(xn_ref[...], lng, lnb)) * (i < n - 1).astype(F32)).astype(BF16)

    h_scr[0:HALO, :] = hp
    h_scr[HALO:HALO + tm, :] = h
    h_scr[HALO + tm:HALO + tm + HALO, :] = hn
    dt_raw = _dot(h_scr[HALO:HALO + tm, :], wxd_ref[:, D_XBC:D_XD]).T[0:N_DH, :] + dtb_ref[...]
    dt = jnp.maximum(dt_raw, 0.0) + jnp.log(1.0 + jnp.exp(-jnp.abs(dt_raw)))
    a = dt * (-jnp.exp(alog_ref[...]))
    lane = lax.broadcasted_iota(jnp.int32, (N_DH, tm), 1) & (CHUNK - 1)
    row = lax.broadcasted_iota(jnp.int32, (N_DH, tm), 0)
    cf, cr = a, a
    k = 1
    while k < CHUNK:
        cf = cf + jnp.where(lane >= k, pltpu.roll(cf, k, 1), 0.0)
        cr = cr + jnp.where(lane < CHUNK - k, pltpu.roll(cr, tm - k, 1), 0.0)
        k *= 2
    is_fwd = row < SSD_HEADS
    acum = jnp.where(is_fwd, cf, cr)
    e = jnp.exp(acum)
    w_end = jnp.exp(jnp.where(is_fwd, cr, cf) - a) * dt
    p = acum * LOG2_E
    src = p - jnp.log2(dt)
    diag = jnp.log2(dt[0:SSD_HEADS] + dt[SSD_HEADS:])
    rowpack_ref[...] = jnp.concatenate([src, diag, jnp.zeros_like(diag), e, w_end], axis=0)
    p1 = p.astype(BF16).astype(F32)
    p2 = (p - p1).astype(BF16).astype(F32)
    p3 = (p - p1 - p2).astype(BF16).astype(F32)
    colsplit_ref[...] = jnp.concatenate([p1, p2, p3, jnp.zeros_like(p1)], axis=0).T.astype(BF16)

    ne = tm + 2 * HALO
    nb, zb = CONV_BLOCK, PROJ_BLOCK
    zu_blocks = [(w_ref, o_ref, c) for w_ref, o_ref in ((wz_ref, z_ref), (wp_ref, up_ref))
                 for c in range(0, D_SSD, zb)]
    for j, c0 in enumerate(range(0, D_XBC, nb)):
        cols = slice(c0, c0 + nb)
        u = _dot(h_scr[...], wxd_ref[:, cols])
        take = -(-len(zu_blocks) // (D_XBC // nb - j))
        for w_ref, o_ref, c in zu_blocks[:take]:
            o_ref[:, c:c + zb] = _dot(h_scr[HALO:HALO + tm, :], w_ref[:, c:c + zb]).astype(BF16)
        zu_blocks = zu_blocks[take:]
        f = [cw_ref[k:k + 1, cols] * u for k in range(D_CONV)]
        after = pltpu.roll(f[3] + pltpu.roll(f[4], ne - 1, 0), ne - 1, 0)
        before = pltpu.roll(f[1] + pltpu.roll(f[0], 1, 0), 1, 0)
        acc = (f[2] + after + before)[HALO:HALO + tm] + cb_ref[:, cols]
        xbc_ref[:, cols] = _silu(acc).astype(BF16)


def _inproj_convert_kernel(x_ref, xp_ref, xn_ref, lng_ref, lnb_ref, sh_ref, sc_ref, wf_ref,
                           cw_ref, cb_ref, dtb_ref, alog_ref,
                           z_ref, xbc_ref, up_ref, xln_ref, rowpack_ref, colsplit_ref,
                           wz_ref, wxd_ref, wp_ref, h_scr, *, tm):
    @pl.when(pl.program_id(0) == 0)
    def _():
        wz_ref[...] = wf_ref[0:D_SSD, :].T.astype(BF16)
        wxd_ref[...] = wf_ref[D_SSD:D_SSD + D_XD, :].T.astype(BF16)
        o3 = D_SSD + D_XBC + N_DH
        wp_ref[...] = wf_ref[o3:o3 + D_POOL, :].T.astype(BF16)

    _inproj_kernel(x_ref, xp_ref, xn_ref, lng_ref, lnb_ref, sh_ref, sc_ref, wz_ref, wxd_ref, wp_ref,
                   cw_ref, cb_ref, dtb_ref, alog_ref,
                   z_ref, xbc_ref, up_ref, xln_ref, rowpack_ref, colsplit_ref, h_scr, tm=tm)


def _inproj_call(x, ln_g, ln_b, sh, sc, weights, conv_w, conv_b, dt_bias, a_log, *, tm):
    L = x.shape[0]
    nt = L // tm
    hb = tm // HALO
    const = lambda i: (0, 0)
    row = lambda i: (i, 0)
    w_shapes = [(D_MODEL, D_SSD), (D_MODEL, D_XD), (D_MODEL, D_POOL)]
    w_specs = [pl.BlockSpec(s, const) for s in w_shapes]
    convert = not isinstance(weights, tuple)
    if convert:
        body = _inproj_convert_kernel
        w_in_specs = [pl.BlockSpec((None,) + weights.shape[1:], lambda i: (0, 0, 0),
                                   pipeline_mode=pl.Buffered(1))]
        w_args = (weights,)
    else:
        body, w_args = _inproj_kernel, weights
        w_in_specs = [pl.BlockSpec(s, const, pipeline_mode=pl.Buffered(1)) for s in w_shapes]
    return pl.pallas_call(
        functools.partial(body, tm=tm),
        grid=(nt,),
        in_specs=[pl.BlockSpec((tm, D_MODEL), row),
                  pl.BlockSpec((HALO, D_MODEL), lambda i: (jnp.maximum(i * hb - 1, 0), 0)),
                  pl.BlockSpec((HALO, D_MODEL), lambda i: (jnp.minimum((i + 1) * hb, L // HALO - 1), 0)),
                  pl.BlockSpec((1, D_MODEL), const), pl.BlockSpec((1, D_MODEL), const),
                  pl.BlockSpec((1, D_MODEL), const), pl.BlockSpec((1, D_MODEL), const)]
                 + w_in_specs
                 + [pl.BlockSpec((None, D_CONV, D_XBC), lambda i: (0, 0, 0)), pl.BlockSpec((1, D_XBC), const),
                    pl.BlockSpec((N_DH, 1), const), pl.BlockSpec((N_DH, 1), const)],
        out_specs=[pl.BlockSpec((tm, D_SSD), row),
                   pl.BlockSpec((tm, D_XBC), row),
                   pl.BlockSpec((tm, D_POOL), row),
                   pl.BlockSpec((tm, D_MODEL), row),
                   pl.BlockSpec((4 * N_DH, tm), lambda i: (0, i)),
                   pl.BlockSpec((tm, 4 * N_DH), row)] + (w_specs if convert else []),
        out_shape=[jax.ShapeDtypeStruct((L, D_SSD), BF16),
                   jax.ShapeDtypeStruct((L, D_XBC), BF16),
                   jax.ShapeDtypeStruct((L, D_POOL), BF16),
                   jax.ShapeDtypeStruct((L, D_MODEL), F32),
                   jax.ShapeDtypeStruct((4 * N_DH, L), F32),
                   jax.ShapeDtypeStruct((L, 4 * N_DH), BF16)]
                  + ([jax.ShapeDtypeStruct(s, BF16) for s in w_shapes] if convert else []),
        scratch_shapes=[pltpu.VMEM((tm + 2 * HALO, D_MODEL), BF16)],
        compiler_params=pltpu.CompilerParams(dimension_semantics=("arbitrary",),
                                             vmem_limit_bytes=VMEM_LIMIT),
        name="inproj",
    )(x, x, x, ln_g, ln_b, sh, sc, *w_args, conv_w, conv_b, dt_bias, a_log)


def _across_lanes(rows, width):
    q = rows.shape[1]
    tall = jnp.concatenate([jnp.broadcast_to(rows[k:k + 1, :], (width, q)) for k in range(rows.shape[0])],
                           axis=0)
    return tall.T


def _lane_tile_matrix():
    m = np.zeros((4 * N_DH, N_DH * CHUNK), np.float32)
    for dh in range(N_DH):
        for piece in range(3):
            m[piece * N_DH + dh, dh * CHUNK:(dh + 1) * CHUNK] = 1.0
    return jnp.asarray(m, BF16)


def _ssd_kernel(xbcf_ref, xbcb_ref, rpf_ref, rpb_ref, cs_ref, xa_ref, dskip_ref, h0f_ref, h0b_ref,
                yf_ref, yb_ref, hf_ref, hb_ref):
    q = CHUNK
    nh = SSD_HEADS
    nsub = xbcf_ref.shape[0] // CHUNK

    @pl.when(pl.program_id(0) == 0)
    def _():
        hf_ref[...] = h0f_ref[...]
        hb_ref[...] = h0b_ref[...]

    def stream(xbc_ref, rp_ref, rows, d, h_ref, exit_row, emit):
        xs = xbc_ref[rows, 0:D_SSD]
        bm = xbc_ref[rows, D_SSD:D_SSD + D_STATE]
        cm = xbc_ref[rows, D_SSD + D_STATE:D_XBC]
        e = _across_lanes(rp_ref[(4 + d) * nh:(5 + d) * nh, rows], SSD_HEAD_DIM)
        w_end = _across_lanes(rp_ref[(6 + d) * nh:(7 + d) * nh, rows], SSD_HEAD_DIM)
        yield
        st = h_ref[...]
        emit(_dot(cm, st.astype(BF16)) * e)
        yield
        xw = (xs.astype(F32) * w_end).astype(BF16)
        upd = lax.dot_general(bm, xw, (((0,), (0,)), ((), ())), preferred_element_type=F32)
        h_ref[...] = st * e[exit_row:exit_row + 1, :] + upd
        yield

    def lead(sub, out):
        fr = slice(sub * q, (sub + 1) * q)
        br = slice((nsub - 1 - sub) * q, (nsub - sub) * q)

        def store_back(v):
            yb_ref[br, :] = v.astype(BF16)

        yield from stream(xbcf_ref, rpf_ref, fr, 0, hf_ref, q - 1, lambda v: out.update(y_off_f=v))
        yield from stream(xbcb_ref, rpb_ref, br, 1, hb_ref, 0, store_back)
        out.update(acol_f=_dot(cs_ref[fr, :], xa_ref[:, 0:nh * q]))
        yield
        out.update(acol_b=_dot(cs_ref[fr, :], xa_ref[:, nh * q:]),
                   cb=_dot_nt(xbcf_ref[fr, D_SSD + D_STATE:D_XBC], xbcf_ref[fr, D_SSD:D_SSD + D_STATE]))
        yield

    ti = lax.broadcasted_iota(jnp.int32, (q, q), 0)
    ui = lax.broadcasted_iota(jnp.int32, (q, q), 1)
    lane = lax.broadcasted_iota(jnp.int32, (q, 2 * SSD_HEAD_DIM), 1)

    def intra(sub, j, pre):
        fr = slice(sub * q, (sub + 1) * q)
        rp = rpf_ref[0:3 * nh, fr]
        cb = pre["cb"]
        ms = []
        for hh in (2 * j, 2 * j + 1):
            seg_f = pre["acol_f"][:, hh * q:(hh + 1) * q] - rp[hh:hh + 1, :]
            seg_b = pre["acol_b"][:, hh * q:(hh + 1) * q] - rp[nh + hh:nh + hh + 1, :]
            both = rp[2 * nh + hh:2 * nh + hh + 1, :]
            power = jnp.where(ui < ti, seg_f, jnp.where(ui > ti, seg_b, both))
            ms.append((cb * jnp.exp2(power)).astype(BF16))
        cols = slice(j * 2 * SSD_HEAD_DIM, (j + 1) * 2 * SSD_HEAD_DIM)
        xp = xbcf_ref[fr, cols]
        zero = jnp.zeros_like(xp)
        rhs = jnp.concatenate([jnp.where(lane < SSD_HEAD_DIM, xp, zero),
                               jnp.where(lane >= SSD_HEAD_DIM, xp, zero)], axis=0)
        y = (_dot(jnp.concatenate(ms, axis=1), rhs) + pre["y_off_f"][:, cols]
             + dskip_ref[:, cols] * xp.astype(F32))
        yf_ref[fr, cols] = y.astype(BF16)

    pre = [dict() for _ in range(nsub)]
    for _ in lead(0, pre[0]):
        pass
    for sub in range(nsub):
        nxt = lead(sub + 1, pre[sub + 1]) if sub + 1 < nsub else iter(())
        for j in range(SSD_HEADS // 2):
            intra(sub, j, pre[sub])
            next(nxt, None)
        for _ in nxt:
            pass


def _ssd_call(xbc, rowpack, colsplit, dskip, h0f, h0b):
    L = xbc.shape[0]
    blk = min(SSD_BLOCK, L)
    nc = L // blk
    const = lambda s: (0, 0)
    fwd = lambda s: (s, 0)
    bwd = lambda s: (nc - 1 - s, 0)
    st_shape = jax.ShapeDtypeStruct((D_STATE, D_SSD), F32)
    xa = _lane_tile_matrix()
    return pl.pallas_call(
        _ssd_kernel,
        grid=(nc,),
        in_specs=[pl.BlockSpec((blk, D_XBC), fwd), pl.BlockSpec((blk, D_XBC), bwd),
                  pl.BlockSpec((4 * N_DH, blk), lambda s: (0, s)),
                  pl.BlockSpec((4 * N_DH, blk), lambda s: (0, nc - 1 - s)),
                  pl.BlockSpec((blk, 4 * N_DH), fwd), pl.BlockSpec(xa.shape, const),
                  pl.BlockSpec((1, D_SSD), const),
                  pl.BlockSpec((D_STATE, D_SSD), const), pl.BlockSpec((D_STATE, D_SSD), const)],
        out_specs=[pl.BlockSpec((blk, D_SSD), fwd), pl.BlockSpec((blk, D_SSD), bwd),
                   pl.BlockSpec((D_STATE, D_SSD), const), pl.BlockSpec((D_STATE, D_SSD), const)],
        out_shape=[jax.ShapeDtypeStruct((L, D_SSD), BF16), jax.ShapeDtypeStruct((L, D_SSD), BF16),
                   st_shape, st_shape],
        compiler_params=pltpu.CompilerParams(dimension_semantics=("arbitrary",),
                                             vmem_limit_bytes=VMEM_LIMIT),
        name="ssd",
    )(xbc, xbc, rowpack, rowpack, colsplit, xa, dskip, h0f, h0b)


def _pool_constants():
    bands, inv_cols = [], []
    t = np.arange(POOL_SUB)
    rt, ct = t // GRID_W, t % GRID_W
    for w in POOL_WINDOWS:
        hw = w // 2
        k = np.arange(POOL_SUB + GRID_W * w)
        rk, ck = k // GRID_W, k % GRID_W
        band = ((rk[None, :] >= rt[:, None]) & (rk[None, :] < rt[:, None] + w)
                & (ck[None, :] >= ct[:, None] - hw) & (ck[None, :] < ct[:, None] + hw))
        bands.append(jnp.asarray(band, BF16))
        cnt_c = np.minimum(ct + hw, GRID_W) - np.maximum(ct - hw, 0)
        inv_cols.append(np.broadcast_to((1.0 / cnt_c)[:, None], (POOL_SUB, 128)))
    return bands, jnp.asarray(np.stack(inv_cols), F32)


def _merge_kernel(yf_ref, yb_ref, z_ref, up_ref, upp_ref, upn_ref, xln_ref, ng_ref, pw_ref, ps_ref, wo_ref,
                  g1_ref, l1g_ref, l1b_ref, band0_ref, band1_ref, band2_ref, band3_ref, invc_ref,
                  out_ref, pw_scr, wo_scr, *, tm, rows_total):
    i = pl.program_id(0)
    n = pl.num_programs(0)

    @pl.when(i == 0)
    def _():
        pw_scr[...] = pw_ref[...].astype(BF16)
        wo_scr[...] = wo_ref[...].astype(BF16)

    y = yf_ref[...].astype(F32) + yb_ref[...].astype(F32)
    g = y * _silu(z_ref[...].astype(F32))
    yn = (g * lax.rsqrt(jnp.mean(g * g, axis=-1, keepdims=True) + LN_EPS) * ng_ref[...]).astype(BF16)

    c = POOL_GROUP_DIM
    band_refs = (band0_ref, band1_ref, band2_ref, band3_ref)
    keep_p = (i > 0).astype(BF16)
    keep_n = (i < n - 1).astype(BF16)
    sub_row = lax.broadcasted_iota(jnp.int32, (POOL_SUB, 128), 0) // GRID_W
    nsub = tm // POOL_SUB
    sums = []
    for gi, w in enumerate(POOL_WINDOWS):
        hw = w // 2
        cols = slice(gi * c, (gi + 1) * c)
        ext = jnp.concatenate([upp_ref[:, cols] * keep_p, up_ref[:, cols], upn_ref[:, cols] * keep_n], axis=0)
        starts = [POOL_HALO + b * POOL_SUB - hw * GRID_W for b in range(nsub)]
        sums.append([_dot(band_refs[gi][...], ext[s:s + POOL_SUB + GRID_W * w]) for s in starts])
    diffs = []
    for gi, w in enumerate(POOL_WINDOWS):
        hw = w // 2
        cols = slice(gi * c, (gi + 1) * c)
        parts = []
        for b in range(nsub):
            row = sub_row + (i * tm + b * POOL_SUB) // GRID_W
            cnt_r = jnp.minimum(row + hw, rows_total) - jnp.maximum(row - hw, 0)
            inv = invc_ref[gi] / cnt_r.astype(F32)
            u = up_ref[b * POOL_SUB:(b + 1) * POOL_SUB, cols].astype(F32)
            parts.append((sums[gi][b] * jnp.concatenate([inv, inv], axis=1) - u).astype(BF16))
        diffs.append(jnp.concatenate(parts, axis=0))
    p = [(_dot(diffs[gi], pw_scr[gi]) * ps_ref[:, gi * c:(gi + 1) * c]).astype(BF16)
         for gi in range(len(POOL_WINDOWS))]
    lhs = jnp.concatenate([yn] + p, axis=1)
    hr = tm // 2
    for r in range(0, tm, hr):
        mix = _dot(lhs[r:r + hr], wo_scr[...])
        out_ref[r:r + hr, :] = _layer_norm(DEEPNORM_ALPHA * xln_ref[r:r + hr, :] + g1_ref[...] * mix,
                                           l1g_ref[...], l1b_ref[...])


def _merge_call(yf, yb, z, up, xln, norm_g, pool_w, pool_scale, w_out, g1, l1g, l1b, *, tm):
    L = xln.shape[0]
    nt = L // tm
    hb = tm // POOL_HALO
    const = lambda i: (0, 0)
    const3 = lambda i: (0, 0, 0)
    row = lambda i: (i, 0)
    vec = pl.BlockSpec((1, D_MODEL), const)
    bands, inv_cols = _pool_constants()
    return pl.pallas_call(
        functools.partial(_merge_kernel, tm=tm, rows_total=L // GRID_W),
        grid=(nt,),
        in_specs=[pl.BlockSpec((tm, D_SSD), row), pl.BlockSpec((tm, D_SSD), row),
                  pl.BlockSpec((tm, D_SSD), row),
                  pl.BlockSpec((tm, D_POOL), row),
                  pl.BlockSpec((POOL_HALO, D_POOL), lambda i: (jnp.maximum(i * hb - 1, 0), 0)),
                  pl.BlockSpec((POOL_HALO, D_POOL),
                               lambda i: (jnp.minimum((i + 1) * hb, L // POOL_HALO - 1), 0)),
                  pl.BlockSpec((tm, D_MODEL), row),
                  vec,
                  pl.BlockSpec((None,) + pool_w.shape[1:], lambda i: (0, 0, 0, 0), pipeline_mode=pl.Buffered(1)),
                  vec,
                  pl.BlockSpec((None,) + w_out.shape[1:], const3, pipeline_mode=pl.Buffered(1)),
                  vec, vec, vec]
                 + [pl.BlockSpec(bm.shape, const) for bm in bands]
                 + [pl.BlockSpec(inv_cols.shape, const3)],
        out_specs=pl.BlockSpec((tm, D_MODEL), row),
        out_shape=jax.ShapeDtypeStruct((L, D_MODEL), F32),
        scratch_shapes=[pltpu.VMEM(pool_w.shape[1:], BF16), pltpu.VMEM(w_out.shape[1:], BF16)],
        compiler_params=pltpu.CompilerParams(dimension_semantics=("arbitrary",),
                                             vmem_limit_bytes=VMEM_LIMIT),
        name="merge",
    )(yf, yb, z, up, up, up, xln, norm_g, pool_w, pool_scale, w_out, g1, l1g, l1b, *bands, inv_cols)


def _round_weights(jobs, sem):
    steps = []
    for src, dst, stage, sem0 in jobs:
        rows = stage.shape[1]
        for k in range(dst.shape[0] // rows):
            slot = k % 2
            copy = pltpu.make_async_copy(src.at[0, pl.ds(k * rows, rows)], stage.at[slot], sem.at[sem0 + slot])
            steps.append((copy, stage, slot, dst, k * rows, rows))
    steps[0][0].start()
    for n, (copy, stage, slot, dst, r0, rows) in enumerate(steps):
        if n + 1 < len(steps):
            steps[n + 1][0].start()
        copy.wait()
        dst[r0:r0 + rows, :] = stage[slot].astype(BF16)


def _ffn_kernel(x_ref, sh_ref, sc_ref, g2_ref, wg_hbm, wu_hbm, wd_hbm, lg_ref, lb_ref, out_ref,
                wg_ref, wu_ref, wd_ref, stage_wide, stage_tall, sem):
    @pl.when(pl.program_id(0) == 0)
    def _():
        _round_weights([(wg_hbm, wg_ref, stage_wide, 0), (wu_hbm, wu_ref, stage_wide, 0),
                        (wd_hbm, wd_ref, stage_tall, 2)], sem)

    x = x_ref[...]
    h = (x * (1.0 + sc_ref[...]) + sh_ref[...]).astype(BF16)
    half = D_FF // 2
    ffn = None
    for s in (0, half):
        gate = _dot(h, wg_ref[:, s:s + half])
        upv = _dot(h, wu_ref[:, s:s + half])
        part = _dot((_silu(gate) * upv).astype(BF16), wd_ref[s:s + half, :])
        ffn = part if ffn is None else ffn + part
    out_ref[...] = _layer_norm(DEEPNORM_ALPHA * x + g2_ref[...] * ffn, lg_ref[...], lb_ref[...])


def _ffn_call(x, sh, sc, g2, wg, wu, wd, lg, lb, *, tm):
    L = x.shape[0]
    const = lambda i: (0, 0)
    row = lambda i: (i, 0)
    vec = pl.BlockSpec((1, D_MODEL), const)
    hbm = pl.BlockSpec(memory_space=pl.ANY)
    return pl.pallas_call(
        _ffn_kernel,
        grid=(L // tm,),
        in_specs=[pl.BlockSpec((tm, D_MODEL), row), vec, vec, vec, hbm, hbm, hbm, vec, vec],
        out_specs=pl.BlockSpec((tm, D_MODEL), row),
        out_shape=jax.ShapeDtypeStruct((L, D_MODEL), F32),
        scratch_shapes=[pltpu.VMEM(wg.shape[1:], BF16), pltpu.VMEM(wu.shape[1:], BF16),
                        pltpu.VMEM(wd.shape[1:], BF16),
                        pltpu.VMEM((2, D_MODEL // FFN_STAGE_CHUNKS, D_FF), F32),
                        pltpu.VMEM((2, D_FF // FFN_STAGE_CHUNKS, D_MODEL), F32),
                        pltpu.SemaphoreType.DMA((4,))],
        compiler_params=pltpu.CompilerParams(dimension_semantics=("arbitrary",),
                                             vmem_limit_bytes=VMEM_LIMIT),
        name="ffn",
    )(x, sh, sc, g2, wg, wu, wd, lg, lb)


def kernel(x, c, ctx, c_ctx, emb_ln_g, emb_ln_b, w_ada, b_ada, in_proj, conv_w, conv_b, dt_bias, a_log,
           d_skip, ssd_norm_g, pool_w, pool_scale, w_out, ln1_g, ln1_b, w_gate, w_up, w_down, ln2_g, ln2_b):
    assert x.shape[0] == 1 and w_ada.shape[0] == DEPTH == 1
    xl, xc = x[0], ctx[0]
    rowv = lambda v: v.reshape(1, -1)
    elg, elb = rowv(emb_ln_g), rowv(emb_ln_b)

    mod = _mod_call(jnp.stack([c[0], c_ctx], axis=1), w_ada, rowv(b_ada[0]))
    sh1, sc1, g1, sh2, sc2, g2 = [mod[0:1, k * D_MODEL:(k + 1) * D_MODEL] for k in range(6)]
    sh1c, sc1c = mod[1:2, 0:D_MODEL], mod[1:2, D_MODEL:2 * D_MODEL]

    conv_args = (conv_w, rowv(conv_b[0]), dt_bias[0].reshape(N_DH, 1), a_log[0].reshape(N_DH, 1))

    dskip = rowv(jnp.repeat(d_skip[0], SSD_HEAD_DIM))
    h_zero = jnp.zeros((D_STATE, D_SSD), F32)

    _, xbc_c, _, _, rp_c, cs_c, wz, wxd, wp = _inproj_call(xc, elg, elb, sh1c, sc1c, jnp.swapaxes(in_proj, 1, 2),
                                                           *conv_args, tm=xc.shape[0])
    _, _, hf_ctx, hb_ctx = _ssd_call(xbc_c, rp_c, cs_c, dskip, h_zero, h_zero)

    z, xbc, up, xln, rp, cs = _inproj_call(xl, elg, elb, sh1, sc1, (wz, wxd, wp), *conv_args, tm=1024)
    yf, yb, _, _ = _ssd_call(xbc, rp, cs, dskip, hf_ctx, hb_ctx)
    x1 = _merge_call(yf, yb, z, up, xln, rowv(ssd_norm_g[0]), pool_w, rowv(pool_scale[0]),
                     w_out, g1, rowv(ln1_g[0]), rowv(ln1_b[0]), tm=512)
    x2 = _ffn_call(x1, sh2, sc2, g2, w_gate, w_up, w_down, rowv(ln2_g[0]), rowv(ln2_b[0]), tm=512)
    return x2[None]
```

```python
import functools

import jax
import jax.numpy as jnp
import numpy as np
from jax import lax
from jax.experimental import pallas as pl
from jax.experimental.pallas import tpu as pltpu

F32 = jnp.float32
BF16 = jnp.bfloat16

D_MODEL = 1024
SSD_HEADS = 16
SSD_HEAD_DIM = 64
D_SSD = SSD_HEADS * SSD_HEAD_DIM
D_STATE = 128
D_CONV = 5
CHUNK = 128
D_POOL = 1024
POOL_WINDOWS = (2, 4, 8, 16)
POOL_GROUP_DIM = D_POOL // len(POOL_WINDOWS)
GRID_W = 64
D_XBC = D_SSD + 2 * D_STATE
D_XD = D_XBC + 128
D_FF = 2816
DEPTH = 1
DEEPNORM_ALPHA = (2 * DEPTH) ** 0.25
LN_EPS = 1e-5
LOG2_E = 1.4426950408889634

HALO = 16
POOL_HALO = 512
POOL_SUB = 256
SSD_BLOCK = 1024
PROJ_BLOCK = 256
CONV_BLOCK = 256
MOD_KB = 128
FFN_STAGE_CHUNKS = 8
N_DH = 2 * SSD_HEADS
VMEM_LIMIT = 56 * 1024 * 1024


def _dot(a, b):
    return jnp.dot(a, b, preferred_element_type=F32)


def _dot_nt(a, b):
    return lax.dot_general(a, b, (((1,), (1,)), ((), ())), preferred_element_type=F32)


def _silu(x):
    hx = 0.5 * x
    return hx + hx * jnp.tanh(hx)


def _layer_norm(x, g, b):
    mu = jnp.mean(x, axis=-1, keepdims=True)
    xc = x - mu
    var = jnp.mean(xc * xc, axis=-1, keepdims=True)
    return xc * lax.rsqrt(var + LN_EPS) * g + b


def _mod_kernel(cc_ref, w_ref, b_ref, out_ref):
    k = pl.program_id(0)

    @pl.when(k == 0)
    def _():
        out_ref[...] = jnp.broadcast_to(b_ref[...], out_ref.shape)

    s = _silu(cc_ref[...])
    w = w_ref[...]
    out_ref[0:1, :] += jnp.sum(w * s[:, 0:1], axis=0, keepdims=True)
    out_ref[1:2, :] += jnp.sum(w * s[:, 1:2], axis=0, keepdims=True)


def _mod_call(cc, w_ada, b_ada):
    n = w_ada.shape[-1]
    return pl.pallas_call(
        _mod_kernel,
        grid=(D_MODEL // MOD_KB,),
        in_specs=[pl.BlockSpec((MOD_KB, 2), lambda k: (k, 0)),
                  pl.BlockSpec((None, MOD_KB, n), lambda k: (0, k, 0)),
                  pl.BlockSpec((1, n), lambda k: (0, 0))],
        out_specs=pl.BlockSpec((8, n), lambda k: (0, 0)),
        out_shape=jax.ShapeDtypeStruct((8, n), F32),
        compiler_params=pltpu.CompilerParams(dimension_semantics=("arbitrary",),
                                             vmem_limit_bytes=VMEM_LIMIT),
        name="mod",
    )(cc, w_ada, b_ada)


def _inproj_kernel(x_ref, xp_ref, xn_ref, lng_ref, lnb_ref, sh_ref, sc_ref, wz_ref, wxd_ref, wp_ref,
                   cw_ref, cb_ref, dtb_ref, alog_ref,
                   z_ref, xbc_ref, up_ref, xln_ref, rowpack_ref, colsplit_ref, h_scr, *, tm):
    i = pl.program_id(0)
    n = pl.num_programs(0)
    lng, lnb, sh, sc = lng_ref[...], lnb_ref[...], sh_ref[...], sc_ref[...]

    def modulated(xn):
        return xn * (1.0 + sc) + sh

    xln = _layer_norm(x_ref[...], lng, lnb)
    xln_ref[...] = xln
    h = modulated(xln).astype(BF16)
    hp = (modulated(_layer_norm(xp_ref[...], lng, lnb)) * (i > 0).astype(F32)).astype(BF16)
    hn = (modulated(_layer_norm(xn_ref[...], lng, lnb)) * (i < n - 1).astype(F32)).astype(BF16)

    h_scr[0:HALO, :] = hp
    h_scr[HALO:HALO + tm, :] = h
    h_scr[HALO + tm:HALO + tm + HALO, :] = hn
    dt_raw = _dot(h_scr[HALO:HALO + tm, :], wxd_ref[:, D_XBC:D_XD]).T[0:N_DH, :] + dtb_ref[...]
    dt = jnp.maximum(dt_raw, 0.0) + jnp.log(1.0 + jnp.exp(-jnp.abs(dt_raw)))
    a = dt * (-jnp.exp(alog_ref[...]))
    lane = lax.broadcasted_iota(jnp.int32, (N_DH, tm), 1) & (CHUNK - 1)
    row = lax.broadcasted_iota(jnp.int32, (N_DH, tm), 0)
    cf, cr = a, a
    k = 1
    while k < CHUNK:
        cf = cf + jnp.where(lane >= k, pltpu.roll(cf, k, 1), 0.0)
        cr = cr + jnp.where(lane < CHUNK - k, pltpu.roll(cr, tm - k, 1), 0.0)
        k *= 2
    is_fwd = row < SSD_HEADS
    acum = jnp.where(is_fwd, cf, cr)
    e = jnp.exp(acum)
    w_end = jnp.exp(jnp.where(is_fwd, cr, cf) - a) * dt
    p = acum * LOG2_E
    src = p - jnp.log2(dt)
    diag = jnp.log2(dt[0:SSD_HEADS] + dt[SSD_HEADS:])
    rowpack_ref[...] = jnp.concatenate([src, diag, jnp.zeros_like(diag), e, w_end], axis=0)
    p1 = p.astype(BF16).astype(F32)
    p2 = (p - p1).astype(BF16).astype(F32)
    p3 = (p - p1 - p2).astype(BF16).astype(F32)
    colsplit_ref[...] = jnp.concatenate([p1, p2, p3, jnp.zeros_like(p1)], axis=0).T.astype(BF16)

    ne = tm + 2 * HALO
    nb, zb = CONV_BLOCK, PROJ_BLOCK
    zu_blocks = [(w_ref, o_ref, c) for w_ref, o_ref in ((wz_ref, z_ref), (wp_ref, up_ref))
                 for c in range(0, D_SSD, zb)]
    for j, c0 in enumerate(range(0, D_XBC, nb)):
        cols = slice(c0, c0 + nb)
        u = _dot(h_scr[...], wxd_ref[:, cols])
        take = -(-len(zu_blocks) // (D_XBC // nb - j))
        for w_ref, o_ref, c in zu_blocks[:take]:
            o_ref[:, c:c + zb] = _dot(h_scr[HALO:HALO + tm, :], w_ref[:, c:c + zb]).astype(BF16)
        zu_blocks = zu_blocks[take:]
        f = [cw_ref[k:k + 1, cols] * u for k in range(D_CONV)]
        after = pltpu.roll(f[3] + pltpu.roll(f[4], ne - 1, 0), ne - 1, 0)
        before = pltpu.roll(f[1] + pltpu.roll(f[0], 1, 0), 1, 0)
        acc = (f[2] + after + before)[HALO:HALO + tm] + cb_ref[:, cols]
        xbc_ref[:, cols] = _silu(acc).astype(BF16)


def _inproj_convert_kernel(x_ref, xp_ref, xn_ref, lng_ref, lnb_ref, sh_ref, sc_ref, wf_ref,
                           cw_ref, cb_ref, dtb_ref, alog_ref,
                           z_ref, xbc_ref, up_ref, xln_ref, rowpack_ref, colsplit_ref,
                           wz_ref, wxd_ref, wp_ref, h_scr, *, tm):
    @pl.when(pl.program_id(0) == 0)
    def _():
        wz_ref[...] = wf_ref[0:D_SSD, :].T.astype(BF16)
        wxd_ref[...] = wf_ref[D_SSD:D_SSD + D_XD, :].T.astype(BF16)
        o3 = D_SSD + D_XBC + N_DH
        wp_ref[...] = wf_ref[o3:o3 + D_POOL, :].T.astype(BF16)

    _inproj_kernel(x_ref, xp_ref, xn_ref, lng_ref, lnb_ref, sh_ref, sc_ref, wz_ref, wxd_ref, wp_ref,
                   cw_ref, cb_ref, dtb_ref, alog_ref,
                   z_ref, xbc_ref, up_ref, xln_ref, rowpack_ref, colsplit_ref, h_scr, tm=tm)


def _inproj_call(x, ln_g, ln_b, sh, sc, weights, conv_w, conv_b, dt_bias, a_log, *, tm):
    L = x.shape[0]
    nt = L // tm
    hb = tm // HALO
    const = lambda i: (0, 0)
    row = lambda i: (i, 0)
    w_shapes = [(D_MODEL, D_SSD), (D_MODEL, D_XD), (D_MODEL, D_POOL)]
    w_specs = [pl.BlockSpec(s, const) for s in w_shapes]
    convert = not isinstance(weights, tuple)
    if convert:
        body = _inproj_convert_kernel
        w_in_specs = [pl.BlockSpec((None,) + weights.shape[1:], lambda i: (0, 0, 0),
                                   pipeline_mode=pl.Buffered(1))]
        w_args = (weights,)
    else:
        body, w_args = _inproj_kernel, weights
        w_in_specs = [pl.BlockSpec(s, const, pipeline_mode=pl.Buffered(1)) for s in w_shapes]
    return pl.pallas_call(
        functools.partial(body, tm=tm),
        grid=(nt,),
        in_specs=[pl.BlockSpec((tm, D_MODEL), row),
                  pl.BlockSpec((HALO, D_MODEL), lambda i: (jnp.maximum(i * hb - 1, 0), 0)),
                  pl.BlockSpec((HALO, D_MODEL), lambda i: (jnp.minimum((i + 1) * hb, L // HALO - 1), 0)),
                  pl.BlockSpec((1, D_MODEL), const), pl.BlockSpec((1, D_MODEL), const),
                  pl.BlockSpec((1, D_MODEL), const), pl.BlockSpec((1, D_MODEL), const)]
                 + w_in_specs
                 + [pl.BlockSpec((None, D_CONV, D_XBC), lambda i: (0, 0, 0)), pl.BlockSpec((1, D_XBC), const),
                    pl.BlockSpec((N_DH, 1), const), pl.BlockSpec((N_DH, 1), const)],
        out_specs=[pl.BlockSpec((tm, D_SSD), row),
                   pl.BlockSpec((tm, D_XBC), row),
                   pl.BlockSpec((tm, D_POOL), row),
                   pl.BlockSpec((tm, D_MODEL), row),
                   pl.BlockSpec((4 * N_DH, tm), lambda i: (0, i)),
                   pl.BlockSpec((tm, 4 * N_DH), row)] + (w_specs if convert else []),
        out_shape=[jax.ShapeDtypeStruct((L, D_SSD), BF16),
                   jax.ShapeDtypeStruct((L, D_XBC), BF16),
                   jax.ShapeDtypeStruct((L, D_POOL), BF16),
                   jax.ShapeDtypeStruct((L, D_MODEL), F32),
                   jax.ShapeDtypeStruct((4 * N_DH, L), F32),
                   jax.ShapeDtypeStruct((L, 4 * N_DH), BF16)]
                  + ([jax.ShapeDtypeStruct(s, BF16) for s in w_shapes] if convert else []),
        scratch_shapes=[pltpu.VMEM((tm + 2 * HALO, D_MODEL), BF16)],
        compiler_params=pltpu.CompilerParams(dimension_semantics=("arbitrary",),
                                             vmem_limit_bytes=VMEM_LIMIT),
        name="inproj",
    )(x, x, x, ln_g, ln_b, sh, sc, *w_args, conv_w, conv_b, dt_bias, a_log)


def _across_lanes(rows, width):
    q = rows.shape[1]
    tall = jnp.concatenate([jnp.broadcast_to(rows[k:k + 1, :], (width, q)) for k in range(rows.shape[0])],
                           axis=0)
    return tall.T


def _lane_tile_matrix():
    m = np.zeros((4 * N_DH, N_DH * CHUNK), np.float32)
    for dh in range(N_DH):
        for piece in range(3):
            m[piece * N_DH + dh, dh * CHUNK:(dh + 1) * CHUNK] = 1.0
    return jnp.asarray(m, BF16)


def _ssd_kernel(xbcf_ref, xbcb_ref, rpf_ref, rpb_ref, cs_ref, xa_ref, dskip_ref, h0f_ref, h0b_ref,
                yf_ref, yb_ref, hf_ref, hb_ref):
    q = CHUNK
    nh = SSD_HEADS
    nsub = xbcf_ref.shape[0] // CHUNK

    @pl.when(pl.program_id(0) == 0)
    def _():
        hf_ref[...] = h0f_ref[...]
        hb_ref[...] = h0b_ref[...]

    def stream(xbc_ref, rp_ref, rows, d, h_ref, exit_row, emit):
        xs = xbc_ref[rows, 0:D_SSD]
        bm = xbc_ref[rows, D_SSD:D_SSD + D_STATE]
        cm = xbc_ref[rows, D_SSD + D_STATE:D_XBC]
        e = _across_lanes(rp_ref[(4 + d) * nh:(5 + d) * nh, rows], SSD_HEAD_DIM)
        w_end = _across_lanes(rp_ref[(6 + d) * nh:(7 + d) * nh, rows], SSD_HEAD_DIM)
        yield
        st = h_ref[...]
        emit(_dot(cm, st.astype(BF16)) * e)
        yield
        xw = (xs.astype(F32) * w_end).astype(BF16)
        upd = lax.dot_general(bm, xw, (((0,), (0,)), ((), ())), preferred_element_type=F32)
        h_ref[...] = st * e[exit_row:exit_row + 1, :] + upd
        yield

    def lead(sub, out):
        fr = slice(sub * q, (sub + 1) * q)
        br = slice((nsub - 1 - sub) * q, (nsub - sub) * q)

        def store_back(v):
            yb_ref[br, :] = v.astype(BF16)

        yield from stream(xbcf_ref, rpf_ref, fr, 0, hf_ref, q - 1, lambda v: out.update(y_off_f=v))
        yield from stream(xbcb_ref, rpb_ref, br, 1, hb_ref, 0, store_back)
        out.update(acol_f=_dot(cs_ref[fr, :], xa_ref[:, 0:nh * q]))
        yield
        out.update(acol_b=_dot(cs_ref[fr, :], xa_ref[:, nh * q:]),
                   cb=_dot_nt(xbcf_ref[fr, D_SSD + D_STATE:D_XBC], xbcf_ref[fr, D_SSD:D_SSD + D_STATE]))
        yield

    ti = lax.broadcasted_iota(jnp.int32, (q, q), 0)
    ui = lax.broadcasted_iota(jnp.int32, (q, q), 1)
    lane = lax.broadcasted_iota(jnp.int32, (q, 2 * SSD_HEAD_DIM), 1)

    def intra(sub, j, pre):
        fr = slice(sub * q, (sub + 1) * q)
        rp = rpf_ref[0:3 * nh, fr]
        cb = pre["cb"]
        ms = []
        for hh in (2 * j, 2 * j + 1):
            seg_f = pre["acol_f"][:, hh * q:(hh + 1) * q] - rp[hh:hh + 1, :]
            seg_b = pre["acol_b"][:, hh * q:(hh + 1) * q] - rp[nh + hh:nh + hh + 1, :]
            both = rp[2 * nh + hh:2 * nh + hh + 1, :]
            power = jnp.where(ui < ti, seg_f, jnp.where(ui > ti, seg_b, both))
            ms.append((cb * jnp.exp2(power)).astype(BF16))
        cols = slice(j * 2 * SSD_HEAD_DIM, (j + 1) * 2 * SSD_HEAD_DIM)
        xp = xbcf_ref[fr, cols]
        zero = jnp.zeros_like(xp)
        rhs = jnp.concatenate([jnp.where(lane < SSD_HEAD_DIM, xp, zero),
                               jnp.where(lane >= SSD_HEAD_DIM, xp, zero)], axis=0)
        y = (_dot(jnp.concatenate(ms, axis=1), rhs) + pre["y_off_f"][:, cols]
             + dskip_ref[:, cols] * xp.astype(F32))
        yf_ref[fr, cols] = y.astype(BF16)

    pre = [dict() for _ in range(nsub)]
    for _ in lead(0, pre[0]):
        pass
    for sub in range(nsub):
        nxt = lead(sub + 1, pre[sub + 1]) if sub + 1 < nsub else iter(())
        for j in range(SSD_HEADS // 2):
            intra(sub, j, pre[sub])
            next(nxt, None)
        for _ in nxt:
            pass


def _ssd_call(xbc, rowpack, colsplit, dskip, h0f, h0b):
    L = xbc.shape[0]
    blk = min(SSD_BLOCK, L)
    nc = L // blk
    const = lambda s: (0, 0)
    fwd = lambda s: (s, 0)
    bwd = lambda s: (nc - 1 - s, 0)
    st_shape = jax.ShapeDtypeStruct((D_STATE, D_SSD), F32)
    xa = _lane_tile_matrix()
    return pl.pallas_call(
        _ssd_kernel,
        grid=(nc,),
        in_specs=[pl.BlockSpec((blk, D_XBC), fwd), pl.BlockSpec((blk, D_XBC), bwd),
                  pl.BlockSpec((4 * N_DH, blk), lambda s: (0, s)),
                  pl.BlockSpec((4 * N_DH, blk), lambda s: (0, nc - 1 - s)),
                  pl.BlockSpec((blk, 4 * N_DH), fwd), pl.BlockSpec(xa.shape, const),
                  pl.BlockSpec((1, D_SSD), const),
                  pl.BlockSpec((D_STATE, D_SSD), const), pl.BlockSpec((D_STATE, D_SSD), const)],
        out_specs=[pl.BlockSpec((blk, D_SSD), fwd), pl.BlockSpec((blk, D_SSD), bwd),
                   pl.BlockSpec((D_STATE, D_SSD), const), pl.BlockSpec((D_STATE, D_SSD), const)],
        out_shape=[jax.ShapeDtypeStruct((L, D_SSD), BF16), jax.ShapeDtypeStruct((L, D_SSD), BF16),
                   st_shape, st_shape],
        compiler_params=pltpu.CompilerParams(dimension_semantics=("arbitrary",),
                                             vmem_limit_bytes=VMEM_LIMIT),
        name="ssd",
    )(xbc, xbc, rowpack, rowpack, colsplit, xa, dskip, h0f, h0b)


def _pool_constants():
    bands, inv_cols = [], []
    t = np.arange(POOL_SUB)
    rt, ct = t // GRID_W, t % GRID_W
    for w in POOL_WINDOWS:
        hw = w // 2
        k = np.arange(POOL_SUB + GRID_W * w)
        rk, ck = k // GRID_W, k % GRID_W
        band = ((rk[None, :] >= rt[:, None]) & (rk[None, :] < rt[:, None] + w)
                & (ck[None, :] >= ct[:, None] - hw) & (ck[None, :] < ct[:, None] + hw))
        bands.append(jnp.asarray(band, BF16))
        cnt_c = np.minimum(ct + hw, GRID_W) - np.maximum(ct - hw, 0)
        inv_cols.append(np.broadcast_to((1.0 / cnt_c)[:, None], (POOL_SUB, 128)))
    return bands, jnp.asarray(np.stack(inv_cols), F32)


def _merge_kernel(yf_ref, yb_ref, z_ref, up_ref, upp_ref, upn_ref, xln_ref, ng_ref, pw_ref, ps_ref, wo_ref,
                  g1_ref, l1g_ref, l1b_ref, band0_ref, band1_ref, band2_ref, band3_ref, invc_ref,
                  out_ref, pw_scr, wo_scr, *, tm, rows_total):
    i = pl.program_id(0)
    n = pl.num_programs(0)

    @pl.when(i == 0)
    def _():
        pw_scr[...] = pw_ref[...].astype(BF16)
        wo_scr[...] = wo_ref[...].astype(BF16)

    y = yf_ref[...].astype(F32) + yb_ref[...].astype(F32)
    g = y * _silu(z_ref[...].astype(F32))
    yn = (g * lax.rsqrt(jnp.mean(g * g, axis=-1, keepdims=True) + LN_EPS) * ng_ref[...]).astype(BF16)

    c = POOL_GROUP_DIM
    band_refs = (band0_ref, band1_ref, band2_ref, band3_ref)
    keep_p = (i > 0).astype(BF16)
    keep_n = (i < n - 1).astype(BF16)
    sub_row = lax.broadcasted_iota(jnp.int32, (POOL_SUB, 128), 0) // GRID_W
    nsub = tm // POOL_SUB
    sums = []
    for gi, w in enumerate(POOL_WINDOWS):
        hw = w // 2
        cols = slice(gi * c, (gi + 1) * c)
        ext = jnp.concatenate([upp_ref[:, cols] * keep_p, up_ref[:, cols], upn_ref[:, cols] * keep_n], axis=0)
        starts = [POOL_HALO + b * POOL_SUB - hw * GRID_W for b in range(nsub)]
        sums.append([_dot(band_refs[gi][...], ext[s:s + POOL_SUB + GRID_W * w]) for s in starts])
    diffs = []
    for gi, w in enumerate(POOL_WINDOWS):
        hw = w // 2
        cols = slice(gi * c, (gi + 1) * c)
        parts = []
        for b in range(nsub):
            row = sub_row + (i * tm + b * POOL_SUB) // GRID_W
            cnt_r = jnp.minimum(row + hw, rows_total) - jnp.maximum(row - hw, 0)
            inv = invc_ref[gi] / cnt_r.astype(F32)
            u = up_ref[b * POOL_SUB:(b + 1) * POOL_SUB, cols].astype(F32)
            parts.append((sums[gi][b] * jnp.concatenate([inv, inv], axis=1) - u).astype(BF16))
        diffs.append(jnp.concatenate(parts, axis=0))
    p = [(_dot(diffs[gi], pw_scr[gi]) * ps_ref[:, gi * c:(gi + 1) * c]).astype(BF16)
         for gi in range(len(POOL_WINDOWS))]
    lhs = jnp.concatenate([yn] + p, axis=1)
    hr = tm // 2
    for r in range(0, tm, hr):
        mix = _dot(lhs[r:r + hr], wo_scr[...])
        out_ref[r:r + hr, :] = _layer_norm(DEEPNORM_ALPHA * xln_ref[r:r + hr, :] + g1_ref[...] * mix,
                                           l1g_ref[...], l1b_ref[...])


def _merge_call(yf, yb, z, up, xln, norm_g, pool_w, pool_scale, w_out, g1, l1g, l1b, *, tm):
    L = xln.shape[0]
    nt = L // tm
    hb = tm // POOL_HALO
    const = lambda i: (0, 0)
    const3 = lambda i: (0, 0, 0)
    row = lambda i: (i, 0)
    vec = pl.BlockSpec((1, D_MODEL), const)
    bands, inv_cols = _pool_constants()
    return pl.pallas_call(
        functools.partial(_merge_kernel, tm=tm, rows_total=L // GRID_W),
        grid=(nt,),
        in_specs=[pl.BlockSpec((tm, D_SSD), row), pl.BlockSpec((tm, D_SSD), row),
                  pl.BlockSpec((tm, D_SSD), row),
                  pl.BlockSpec((tm, D_POOL), row),
                  pl.BlockSpec((POOL_HALO, D_POOL), lambda i: (jnp.maximum(i * hb - 1, 0), 0)),
                  pl.BlockSpec((POOL_HALO, D_POOL),
                               lambda i: (jnp.minimum((i + 1) * hb, L // POOL_HALO - 1), 0)),
                  pl.BlockSpec((tm, D_MODEL), row),
                  vec,
                  pl.BlockSpec((None,) + pool_w.shape[1:], lambda i: (0, 0, 0, 0), pipeline_mode=pl.Buffered(1)),
                  vec,
                  pl.BlockSpec((None,) + w_out.shape[1:], const3, pipeline_mode=pl.Buffered(1)),
                  vec, vec, vec]
                 + [pl.BlockSpec(bm.shape, const) for bm in bands]
                 + [pl.BlockSpec(inv_cols.shape, const3)],
        out_specs=pl.BlockSpec((tm, D_MODEL), row),
        out_shape=jax.ShapeDtypeStruct((L, D_MODEL), F32),
        scratch_shapes=[pltpu.VMEM(pool_w.shape[1:], BF16), pltpu.VMEM(w_out.shape[1:], BF16)],
        compiler_params=pltpu.CompilerParams(dimension_semantics=("arbitrary",),
                                             vmem_limit_bytes=VMEM_LIMIT),
        name="merge",
    )(yf, yb, z, up, up, up, xln, norm_g, pool_w, pool_scale, w_out, g1, l1g, l1b, *bands, inv_cols)


def _round_weights(jobs, sem):
    steps = []
    for src, dst, stage, sem0 in jobs:
        rows = stage.shape[1]
        for k in range(dst.shape[0] // rows):
            slot = k % 2
            copy = pltpu.make_async_copy(src.at[0, pl.ds(k * rows, rows)], stage.at[slot], sem.at[sem0 + slot])
            steps.append((copy, stage, slot, dst, k * rows, rows))
    steps[0][0].start()
    for n, (copy, stage, slot, dst, r0, rows) in enumerate(steps):
        if n + 1 < len(steps):
            steps[n + 1][0].start()
        copy.wait()
        dst[r0:r0 + rows, :] = stage[slot].astype(BF16)


def _ffn_kernel(x_ref, sh_ref, sc_ref, g2_ref, wg_hbm, wu_hbm, wd_hbm, lg_ref, lb_ref, out_ref,
                wg_ref, wu_ref, wd_ref, stage_wide, stage_tall, sem):
    @pl.when(pl.program_id(0) == 0)
    def _():
        _round_weights([(wg_hbm, wg_ref, stage_wide, 0), (wu_hbm, wu_ref, stage_wide, 0),
                        (wd_hbm, wd_ref, stage_tall, 2)], sem)

    x = x_ref[...]
    h = (x * (1.0 + sc_ref[...]) + sh_ref[...]).astype(BF16)
    half = D_FF // 2
    ffn = None
    for s in (0, half):
        gate = _dot(h, wg_ref[:, s:s + half])
        upv = _dot(h, wu_ref[:, s:s + half])
        part = _dot((_silu(gate) * upv).astype(BF16), wd_ref[s:s + half, :])
        ffn = part if ffn is None else ffn + part
    out_ref[...] = _layer_norm(DEEPNORM_ALPHA * x + g2_ref[...] * ffn, lg_ref[...], lb_ref[...])


def _ffn_call(x, sh, sc, g2, wg, wu, wd, lg, lb, *, tm):
    L = x.shape[0]
    const = lambda i: (0, 0)
    row = lambda i: (i, 0)
    vec = pl.BlockSpec((1, D_MODEL), const)
    hbm = pl.BlockSpec(memory_space=pl.ANY)
    return pl.pallas_call(
        _ffn_kernel,
        grid=(L // tm,),
        in_specs=[pl.BlockSpec((tm, D_MODEL), row), vec, vec, vec, hbm, hbm, hbm, vec, vec],
        out_specs=pl.BlockSpec((tm, D_MODEL), row),
        out_shape=jax.ShapeDtypeStruct((L, D_MODEL), F32),
        scratch_shapes=[pltpu.VMEM(wg.shape[1:], BF16), pltpu.VMEM(wu.shape[1:], BF16),
                        pltpu.VMEM(wd.shape[1:], BF16),
                        pltpu.VMEM((2, D_MODEL // FFN_STAGE_CHUNKS, D_FF), F32),
                        pltpu.VMEM((2, D_FF // FFN_STAGE_CHUNKS, D_MODEL), F32),
                        pltpu.SemaphoreType.DMA((4,))],
        compiler_params=pltpu.CompilerParams(dimension_semantics=("arbitrary",),
                                             vmem_limit_bytes=VMEM_LIMIT),
        name="ffn",
    )(x, sh, sc, g2, wg, wu, wd, lg, lb)


def kernel(x, c, ctx, c_ctx, emb_ln_g, emb_ln_b, w_ada, b_ada, in_proj, conv_w, conv_b, dt_bias, a_log,
           d_skip, ssd_norm_g, pool_w, pool_scale, w_out, ln1_g, ln1_b, w_gate, w_up, w_down, ln2_g, ln2_b):
    assert x.shape[0] == 1 and w_ada.shape[0] == DEPTH == 1
    xl, xc = x[0], ctx[0]
    rowv = lambda v: v.reshape(1, -1)
    elg, elb = rowv(emb_ln_g), rowv(emb_ln_b)

    mod = _mod_call(jnp.stack([c[0], c_ctx], axis=1), w_ada, rowv(b_ada[0]))
    sh1, sc1, g1, sh2, sc2, g2 = [mod[0:1, k * D_MODEL:(k + 1) * D_MODEL] for k in range(6)]
    sh1c, sc1c = mod[1:2, 0:D_MODEL], mod[1:2, D_MODEL:2 * D_MODEL]

    conv_args = (conv_w, rowv(conv_b[0]), dt_bias[0].reshape(N_DH, 1), a_log[0].reshape(N_DH, 1))

    dskip = rowv(jnp.repeat(d_skip[0], SSD_HEAD_DIM))
    h_zero = jnp.zeros((D_STATE, D_SSD), F32)

    _, xbc_c, _, _, rp_c, cs_c, wz, wxd, wp = _inproj_call(xc, elg, elb, sh1c, sc1c, jnp.swapaxes(in_proj, 1, 2),
                                                           *conv_args, tm=xc.shape[0])
    _, _, hf_ctx, hb_ctx = _ssd_call(xbc_c, rp_c, cs_c, dskip, h_zero, h_zero)

    z, xbc, up, xln, rp, cs = _inproj_call(xl, elg, elb, sh1, sc1, (wz, wxd, wp), *conv_args, tm=1024)
    yf, yb, _, _ = _ssd_call(xbc, rp, cs, dskip, hf_ctx, hb_ctx)
    x1 = _merge_call(yf, yb, z, up, xln, rowv(ssd_norm_g[0]), pool_w, rowv(pool_scale[0]),
                     w_out, g1, rowv(ln1_g[0]), rowv(ln1_b[0]), tm=512)
    x2 = _ffn_call(x1, sh2, sc2, g2, w_gate, w_up, w_down, rowv(ln2_g[0]), rowv(ln2_b[0]), tm=512)
    return x2[None]
```

```python
import functools

import jax
import jax.numpy as jnp
import numpy as np
from jax import lax
from jax.experimental import pallas as pl
from jax.experimental.pallas import tpu as pltpu

F32 = jnp.float32
BF16 = jnp.bfloat16

D_MODEL = 1024
SSD_HEADS = 16
SSD_HEAD_DIM = 64
D_SSD = SSD_HEADS * SSD_HEAD_DIM
D_STATE = 128
D_CONV = 5
CHUNK = 128
D_POOL = 1024
POOL_WINDOWS = (2, 4, 8, 16)
POOL_GROUP_DIM = D_POOL // len(POOL_WINDOWS)
GRID_W = 64
D_XBC = D_SSD + 2 * D_STATE
D_XD = D_XBC + 128
D_FF = 2816
DEPTH = 1
DEEPNORM_ALPHA = (2 * DEPTH) ** 0.25
LN_EPS = 1e-5
LOG2_E = 1.4426950408889634

HALO = 16
POOL_HALO = 512
POOL_SUB = 256
SSD_BLOCK = 1024
PROJ_BLOCK = 256
CONV_BLOCK = 256
MOD_KB = 256
FFN_STAGE_CHUNKS = 16
FFN_ROWS = 512
MERGE_ROWS = 512
MERGE_STAGE_ROWS = 128
N_DH = 2 * SSD_HEADS
VMEM_LIMIT = 56 * 1024 * 1024


def _dot(a, b):
    return jnp.dot(a, b, preferred_element_type=F32)


def _dot_nt(a, b):
    return lax.dot_general(a, b, (((1,), (1,)), ((), ())), preferred_element_type=F32)


def _silu(x):
    hx = 0.5 * x
    return hx + hx * jnp.tanh(hx)


def _layer_norm(x, g, b):
    mu = jnp.mean(x, axis=-1, keepdims=True)
    xc = x - mu
    var = jnp.mean(xc * xc, axis=-1, keepdims=True)
    return xc * lax.rsqrt(var + LN_EPS) * g + b


def _round_weights(jobs, sem):
    steps = []
    for src, dst, stage, sem0 in jobs:
        rows = stage.shape[1]
        for k in range(dst.shape[0] // rows):
            slot = k % 2
            copy = pltpu.make_async_copy(src.at[0, pl.ds(k * rows, rows)], stage.at[slot], sem.at[sem0 + slot])
            steps.append((copy, stage, slot, dst, k * rows, rows))
    steps[0][0].start()
    for n, (copy, stage, slot, dst, r0, rows) in enumerate(steps):
        if n + 1 < len(steps):
            steps[n + 1][0].start()
        copy.wait()
        dst[r0:r0 + rows, :] = stage[slot].astype(BF16)


def _mod_kernel(cc_ref, w_ref, b_ref, out_ref):
    k = pl.program_id(0)

    @pl.when(k == 0)
    def _():
        out_ref[...] = jnp.broadcast_to(b_ref[...], out_ref.shape)

    s = _silu(cc_ref[...])
    w = w_ref[...]
    out_ref[0:1, :] += jnp.sum(w * s[:, 0:1], axis=0, keepdims=True)
    out_ref[1:2, :] += jnp.sum(w * s[:, 1:2], axis=0, keepdims=True)


def _mod_call(cc, w_ada, b_ada):
    n = w_ada.shape[-1]
    return pl.pallas_call(
        _mod_kernel,
        grid=(D_MODEL // MOD_KB,),
        in_specs=[pl.BlockSpec((MOD_KB, 2), lambda k: (k, 0)),
                  pl.BlockSpec((None, MOD_KB, n), lambda k: (0, k, 0)),
                  pl.BlockSpec((1, n), lambda k: (0, 0))],
        out_specs=pl.BlockSpec((8, n), lambda k: (0, 0)),
        out_shape=jax.ShapeDtypeStruct((8, n), F32),
        compiler_params=pltpu.CompilerParams(dimension_semantics=("arbitrary",),
                                             vmem_limit_bytes=VMEM_LIMIT),
        name="mod",
    )(cc, w_ada, b_ada)


def _inproj_kernel(x_ref, xp_ref, xn_ref, lng_ref, lnb_ref, sh_ref, sc_ref, wz_ref, wxd_ref, wp_ref,
                   cw_ref, cb_ref, dtb_ref, alog_ref,
                   z_ref, xbc_ref, up_ref, xln_ref, rowpack_ref, colsplit_ref, h_scr, *, tm):
    i = pl.program_id(0)
    n = pl.num_programs(0)
    lng, lnb, sh, sc = lng_ref[...], lnb_ref[...], sh_ref[...], sc_ref[...]

    def modulated(xn):
        return xn * (1.0 + sc) + sh

    xln = _layer_norm(x_ref[...], lng, lnb)
    xln_ref[...] = xln
    h = modulated(xln).astype(BF16)
    hp = (modulated(_layer_norm(xp_ref[...], lng, lnb)) * (i > 0).astype(F32)).astype(BF16)
    hn = (modulated(_layer_norm(xn_ref[...], lng, lnb)) * (i < n - 1).astype(F32)).astype(BF16)

    h_scr[0:HALO, :] = hp
    h_scr[HALO:HALO + tm, :] = h
    h_scr[HALO + tm:HALO + tm + HALO, :] = hn
    dt_raw = _dot(h_scr[HALO:HALO + tm, :], wxd_ref[:, D_XBC:D_XD]).T[0:N_DH, :] + dtb_ref[...]
    dt = jnp.maximum(dt_raw, 0.0) + jnp.log(1.0 + jnp.exp(-jnp.abs(dt_raw)))
    a = dt * (-jnp.exp(alog_ref[...]))
    lane = lax.broadcasted_iota(jnp.int32, (N_DH, tm), 1) & (CHUNK - 1)
    row = lax.broadcasted_iota(jnp.int32, (N_DH, tm), 0)
    cf, cr = a, a
    k = 1
    while k < CHUNK:
        cf = cf + jnp.where(lane >= k, pltpu.roll(cf, k, 1), 0.0)
        cr = cr + jnp.where(lane < CHUNK - k, pltpu.roll(cr, tm - k, 1), 0.0)
        k *= 2
    is_fwd = row < SSD_HEADS
    acum = jnp.where(is_fwd, cf, cr)
    e = jnp.exp(acum)
    w_end = jnp.exp(jnp.where(is_fwd, cr, cf) - a) * dt
    p = acum * LOG2_E
    src = p - jnp.log2(dt)
    diag = jnp.log2(dt[0:SSD_HEADS] + dt[SSD_HEADS:])
    rowpack_ref[...] = jnp.concatenate([src, diag, jnp.zeros_like(diag), e, w_end], axis=0)
    p1 = p.astype(BF16).astype(F32)
    p2 = (p - p1).astype(BF16).astype(F32)
    p3 = (p - p1 - p2).astype(BF16).astype(F32)
    colsplit_ref[...] = jnp.concatenate([p1, p2, p3, jnp.zeros_like(p1)], axis=0).T.astype(BF16)

    ne = tm + 2 * HALO
    nb, zb = CONV_BLOCK, PROJ_BLOCK
    zu_blocks = [(w_ref, o_ref, c) for w_ref, o_ref in ((wz_ref, z_ref), (wp_ref, up_ref))
                 for c in range(0, D_SSD, zb)]
    for j, c0 in enumerate(range(0, D_XBC, nb)):
        cols = slice(c0, c0 + nb)
        u = _dot(h_scr[...], wxd_ref[:, cols])
        take = -(-len(zu_blocks) // (D_XBC // nb - j))
        for w_ref, o_ref, c in zu_blocks[:take]:
            o_ref[:, c:c + zb] = _dot(h_scr[HALO:HALO + tm, :], w_ref[:, c:c + zb]).astype(BF16)
        zu_blocks = zu_blocks[take:]
        f = [cw_ref[k:k + 1, cols] * u for k in range(D_CONV)]
        after = pltpu.roll(f[3] + pltpu.roll(f[4], ne - 1, 0), ne - 1, 0)
        before = pltpu.roll(f[1] + pltpu.roll(f[0], 1, 0), 1, 0)
        acc = (f[2] + after + before)[HALO:HALO + tm] + cb_ref[:, cols]
        xbc_ref[:, cols] = _silu(acc).astype(BF16)


def _inproj_convert_kernel(x_ref, xp_ref, xn_ref, lng_ref, lnb_ref, sh_ref, sc_ref, wf_ref,
                           cw_ref, cb_ref, dtb_ref, alog_ref,
                           z_ref, xbc_ref, up_ref, xln_ref, rowpack_ref, colsplit_ref,
                           wz_ref, wxd_ref, wp_ref, h_scr, *, tm):
    @pl.when(pl.program_id(0) == 0)
    def _():
        wz_ref[...] = wf_ref[0:D_SSD, :].T.astype(BF16)
        wxd_ref[...] = wf_ref[D_SSD:D_SSD + D_XD, :].T.astype(BF16)
        o3 = D_SSD + D_XBC + N_DH
        wp_ref[...] = wf_ref[o3:o3 + D_POOL, :].T.astype(BF16)

    _inproj_kernel(x_ref, xp_ref, xn_ref, lng_ref, lnb_ref, sh_ref, sc_ref, wz_ref, wxd_ref, wp_ref,
                   cw_ref, cb_ref, dtb_ref, alog_ref,
                   z_ref, xbc_ref, up_ref, xln_ref, rowpack_ref, colsplit_ref, h_scr, tm=tm)


def _inproj_call(x, ln_g, ln_b, sh, sc, weights, conv_w, conv_b, dt_bias, a_log, *, tm):
    L = x.shape[0]
    nt = L // tm
    hb = tm // HALO
    const = lambda i: (0, 0)
    row = lambda i: (i, 0)
    w_shapes = [(D_MODEL, D_SSD), (D_MODEL, D_XD), (D_MODEL, D_POOL)]
    w_specs = [pl.BlockSpec(s, const) for s in w_shapes]
    convert = not isinstance(weights, tuple)
    if convert:
        body = _inproj_convert_kernel
        w_in_specs = [pl.BlockSpec((None,) + weights.shape[1:], lambda i: (0, 0, 0),
                                   pipeline_mode=pl.Buffered(1))]
        w_args = (weights,)
    else:
        body, w_args = _inproj_kernel, weights
        w_in_specs = [pl.BlockSpec(s, const, pipeline_mode=pl.Buffered(1)) for s in w_shapes]
    return pl.pallas_call(
        functools.partial(body, tm=tm),
        grid=(nt,),
        in_specs=[pl.BlockSpec((tm, D_MODEL), row),
                  pl.BlockSpec((HALO, D_MODEL), lambda i: (jnp.maximum(i * hb - 1, 0), 0)),
                  pl.BlockSpec((HALO, D_MODEL), lambda i: (jnp.minimum((i + 1) * hb, L // HALO - 1), 0)),
                  pl.BlockSpec((1, D_MODEL), const), pl.BlockSpec((1, D_MODEL), const),
                  pl.BlockSpec((1, D_MODEL), const), pl.BlockSpec((1, D_MODEL), const)]
                 + w_in_specs
                 + [pl.BlockSpec((None, D_CONV, D_XBC), lambda i: (0, 0, 0)), pl.BlockSpec((1, D_XBC), const),
                    pl.BlockSpec((N_DH, 1), const), pl.BlockSpec((N_DH, 1), const)],
        out_specs=[pl.BlockSpec((tm, D_SSD), row),
                   pl.BlockSpec((tm, D_XBC), row),
                   pl.BlockSpec((tm, D_POOL), row),
                   pl.BlockSpec((tm, D_MODEL), row),
                   pl.BlockSpec((4 * N_DH, tm), lambda i: (0, i)),
                   pl.BlockSpec((tm, 4 * N_DH), row)] + (w_specs if convert else []),
        out_shape=[jax.ShapeDtypeStruct((L, D_SSD), BF16),
                   jax.ShapeDtypeStruct((L, D_XBC), BF16),
                   jax.ShapeDtypeStruct((L, D_POOL), BF16),
                   jax.ShapeDtypeStruct((L, D_MODEL), F32),
                   jax.ShapeDtypeStruct((4 * N_DH, L), F32),
                   jax.ShapeDtypeStruct((L, 4 * N_DH), BF16)]
                  + ([jax.ShapeDtypeStruct(s, BF16) for s in w_shapes] if convert else []),
        scratch_shapes=[pltpu.VMEM((tm + 2 * HALO, D_MODEL), BF16)],
        compiler_params=pltpu.CompilerParams(dimension_semantics=("arbitrary",),
                                             vmem_limit_bytes=VMEM_LIMIT),
        name="inproj",
    )(x, x, x, ln_g, ln_b, sh, sc, *w_args, conv_w, conv_b, dt_bias, a_log)


def _across_lanes(rows, width):
    q = rows.shape[1]
    tall = jnp.concatenate([jnp.broadcast_to(rows[k:k + 1, :], (width, q)) for k in range(rows.shape[0])],
                           axis=0)
    return tall.T


def _lane_tile_matrix():
    m = np.zeros((4 * N_DH, N_DH * CHUNK), np.float32)
    for dh in range(N_DH):
        for piece in range(3):
            m[piece * N_DH + dh, dh * CHUNK:(dh + 1) * CHUNK] = 1.0
    return jnp.asarray(m, BF16)


def _ssd_kernel(xbcf_ref, xbcb_ref, rpf_ref, rpb_ref, cs_ref, xa_ref, dskip_ref, h0f_ref, h0b_ref,
                yf_ref, yb_ref, hf_ref, hb_ref):
    q = CHUNK
    nh = SSD_HEADS
    nsub = xbcf_ref.shape[0] // CHUNK

    @pl.when(pl.program_id(0) == 0)
    def _():
        hf_ref[...] = h0f_ref[...]
        hb_ref[...] = h0b_ref[...]

    def stream(xbc_ref, rp_ref, rows, d, h_ref, exit_row, emit):
        xs = xbc_ref[rows, 0:D_SSD]
        bm = xbc_ref[rows, D_SSD:D_SSD + D_STATE]
        cm = xbc_ref[rows, D_SSD + D_STATE:D_XBC]
        e = _across_lanes(rp_ref[(4 + d) * nh:(5 + d) * nh, rows], SSD_HEAD_DIM)
        w_end = _across_lanes(rp_ref[(6 + d) * nh:(7 + d) * nh, rows], SSD_HEAD_DIM)
        yield
        st = h_ref[...]
        emit(_dot(cm, st.astype(BF16)) * e)
        yield
        xw = (xs.astype(F32) * w_end).astype(BF16)
        upd = lax.dot_general(bm, xw, (((0,), (0,)), ((), ())), preferred_element_type=F32)
        h_ref[...] = st * e[exit_row:exit_row + 1, :] + upd
        yield

    def lead(sub, out):
        fr = slice(sub * q, (sub + 1) * q)
        br = slice((nsub - 1 - sub) * q, (nsub - sub) * q)

        def store_back(v):
            yb_ref[br, :] = v.astype(BF16)

        yield from stream(xbcf_ref, rpf_ref, fr, 0, hf_ref, q - 1, lambda v: out.update(y_off_f=v))
        yield from stream(xbcb_ref, rpb_ref, br, 1, hb_ref, 0, store_back)
        out.update(acol_f=_dot(cs_ref[fr, :], xa_ref[:, 0:nh * q]))
        yield
        out.update(acol_b=_dot(cs_ref[fr, :], xa_ref[:, nh * q:]),
                   cb=_dot_nt(xbcf_ref[fr, D_SSD + D_STATE:D_XBC], xbcf_ref[fr, D_SSD:D_SSD + D_STATE]))
        yield

    ti = lax.broadcasted_iota(jnp.int32, (q, q), 0)
    ui = lax.broadcasted_iota(jnp.int32, (q, q), 1)
    lane = lax.broadcasted_iota(jnp.int32, (q, 2 * SSD_HEAD_DIM), 1)

    def intra(sub, j, pre):
        fr = slice(sub * q, (sub + 1) * q)
        rp = rpf_ref[0:3 * nh, fr]
        cb = pre["cb"]
        ms = []
        for hh in (2 * j, 2 * j + 1):
            seg_f = pre["acol_f"][:, hh * q:(hh + 1) * q] - rp[hh:hh + 1, :]
            seg_b = pre["acol_b"][:, hh * q:(hh + 1) * q] - rp[nh + hh:nh + hh + 1, :]
            both = rp[2 * nh + hh:2 * nh + hh + 1, :]
            power = jnp.where(ui < ti, seg_f, jnp.where(ui > ti, seg_b, both))
            ms.append((cb * jnp.exp2(power)).astype(BF16))
        cols = slice(j * 2 * SSD_HEAD_DIM, (j + 1) * 2 * SSD_HEAD_DIM)
        xp = xbcf_ref[fr, cols]
        zero = jnp.zeros_like(xp)
        rhs = jnp.concatenate([jnp.where(lane < SSD_HEAD_DIM, xp, zero),
                               jnp.where(lane >= SSD_HEAD_DIM, xp, zero)], axis=0)
        y = (_dot(jnp.concatenate(ms, axis=1), rhs) + pre["y_off_f"][:, cols]
             + dskip_ref[:, cols] * xp.astype(F32))
        yf_ref[fr, cols] = y.astype(BF16)

    pre = [dict() for _ in range(nsub)]
    for _ in lead(0, pre[0]):
        pass
    for sub in range(nsub):
        nxt = lead(sub + 1, pre[sub + 1]) if sub + 1 < nsub else iter(())
        for j in range(SSD_HEADS // 2):
            intra(sub, j, pre[sub])
            next(nxt, None)
        for _ in nxt:
            pass


def _ssd_call(xbc, rowpack, colsplit, dskip, h0f, h0b):
    L = xbc.shape[0]
    blk = min(SSD_BLOCK, L)
    nc = L // blk
    const = lambda s: (0, 0)
    fwd = lambda s: (s, 0)
    bwd = lambda s: (nc - 1 - s, 0)
    st_shape = jax.ShapeDtypeStruct((D_STATE, D_SSD), F32)
    xa = _lane_tile_matrix()
    return pl.pallas_call(
        _ssd_kernel,
        grid=(nc,),
        in_specs=[pl.BlockSpec((blk, D_XBC), fwd), pl.BlockSpec((blk, D_XBC), bwd),
                  pl.BlockSpec((4 * N_DH, blk), lambda s: (0, s)),
                  pl.BlockSpec((4 * N_DH, blk), lambda s: (0, nc - 1 - s)),
                  pl.BlockSpec((blk, 4 * N_DH), fwd), pl.BlockSpec(xa.shape, const),
                  pl.BlockSpec((1, D_SSD), const),
                  pl.BlockSpec((D_STATE, D_SSD), const), pl.BlockSpec((D_STATE, D_SSD), const)],
        out_specs=[pl.BlockSpec((blk, D_SSD), fwd), pl.BlockSpec((blk, D_SSD), bwd),
                   pl.BlockSpec((D_STATE, D_SSD), const), pl.BlockSpec((D_STATE, D_SSD), const)],
        out_shape=[jax.ShapeDtypeStruct((L, D_SSD), BF16), jax.ShapeDtypeStruct((L, D_SSD), BF16),
                   st_shape, st_shape],
        compiler_params=pltpu.CompilerParams(dimension_semantics=("arbitrary",),
                                             vmem_limit_bytes=VMEM_LIMIT),
        name="ssd",
    )(xbc, xbc, rowpack, rowpack, colsplit, xa, dskip, h0f, h0b)


def _pool_constants():
    bands, inv_cols = [], []
    t = np.arange(POOL_SUB)
    rt, ct = t // GRID_W, t % GRID_W
    for w in POOL_WINDOWS:
        hw = w // 2
        k = np.arange(POOL_SUB + GRID_W * w)
        rk, ck = k // GRID_W, k % GRID_W
        band = ((rk[None, :] >= rt[:, None]) & (rk[None, :] < rt[:, None] + w)
                & (ck[None, :] >= ct[:, None] - hw) & (ck[None, :] < ct[:, None] + hw))
        bands.append(jnp.asarray(band, BF16))
        cnt_c = np.minimum(ct + hw, GRID_W) - np.maximum(ct - hw, 0)
        inv_cols.append(np.broadcast_to((1.0 / cnt_c)[:, None], (POOL_SUB, 128)))
    return bands, jnp.asarray(np.stack(inv_cols), F32)


def _merge_kernel(yf_ref, yb_ref, z_ref, up_ref, upp_ref, upn_ref, xln_ref, ng_ref, pw_hbm, ps_ref, wo_hbm,
                  g1_ref, l1g_ref, l1b_ref, band0_ref, band1_ref, band2_ref, band3_ref, invc_ref,
                  out_ref, pw_scr, wo_scr, stage_pw, stage_wo, sem, *, tm, rows_total):
    i = pl.program_id(0)
    n = pl.num_programs(0)

    @pl.when(i == 0)
    def _():
        _round_weights([(pw_hbm, pw_scr, stage_pw, 0), (wo_hbm, wo_scr, stage_wo, 2)], sem)

    c = POOL_GROUP_DIM
    band_refs = (band0_ref, band1_ref, band2_ref, band3_ref)
    keep_p = (i > 0).astype(BF16)
    keep_n = (i < n - 1).astype(BF16)
    sub_row = lax.broadcasted_iota(jnp.int32, (POOL_SUB, 128), 0) // GRID_W
    nsub = MERGE_ROWS // POOL_SUB
    exts = [jnp.concatenate([upp_ref[:, gi * c:(gi + 1) * c] * keep_p, up_ref[:, gi * c:(gi + 1) * c],
                             upn_ref[:, gi * c:(gi + 1) * c] * keep_n], axis=0)
            for gi in range(len(POOL_WINDOWS))]

    for t0 in range(0, tm, MERGE_ROWS):
        rows = slice(t0, t0 + MERGE_ROWS)
        y = yf_ref[rows, :].astype(F32) + yb_ref[rows, :].astype(F32)
        g = y * _silu(z_ref[rows, :].astype(F32))
        yn = (g * lax.rsqrt(jnp.mean(g * g, axis=-1, keepdims=True) + LN_EPS) * ng_ref[...]).astype(BF16)

        sums = []
        for gi, w in enumerate(POOL_WINDOWS):
            starts = [POOL_HALO + t0 + b * POOL_SUB - (w // 2) * GRID_W for b in range(nsub)]
            sums.append([_dot(band_refs[gi][...], exts[gi][s:s + POOL_SUB + GRID_W * w]) for s in starts])
        diffs = []
        for gi, w in enumerate(POOL_WINDOWS):
            hw = w // 2
            cols = slice(gi * c, (gi + 1) * c)
            parts = []
            for b in range(nsub):
                r0 = t0 + b * POOL_SUB
                row = sub_row + (i * tm + r0) // GRID_W
                cnt_r = jnp.minimum(row + hw, rows_total) - jnp.maximum(row - hw, 0)
                inv = invc_ref[gi] / cnt_r.astype(F32)
                u = up_ref[r0:r0 + POOL_SUB, cols].astype(F32)
                parts.append((sums[gi][b] * jnp.concatenate([inv, inv], axis=1) - u).astype(BF16))
            diffs.append(jnp.concatenate(parts, axis=0))
        p = [(_dot(diffs[gi], pw_scr[gi * c:(gi + 1) * c, :]) * ps_ref[:, gi * c:(gi + 1) * c]).astype(BF16)
             for gi in range(len(POOL_WINDOWS))]
        lhs = jnp.concatenate([yn] + p, axis=1)
        hr = MERGE_ROWS // 2
        for r in range(0, MERGE_ROWS, hr):
            mix = _dot(lhs[r:r + hr], wo_scr[...])
            out_ref[t0 + r:t0 + r + hr, :] = _layer_norm(
                DEEPNORM_ALPHA * xln_ref[t0 + r:t0 + r + hr, :] + g1_ref[...] * mix, l1g_ref[...], l1b_ref[...])


def _merge_call(yf, yb, z, up, xln, norm_g, pool_w, pool_scale, w_out, g1, l1g, l1b, *, tm):
    L = xln.shape[0]
    nt = L // tm
    hb = tm // POOL_HALO
    const = lambda i: (0, 0)
    const3 = lambda i: (0, 0, 0)
    row = lambda i: (i, 0)
    vec = pl.BlockSpec((1, D_MODEL), const)
    hbm = pl.BlockSpec(memory_space=pl.ANY)
    bands, inv_cols = _pool_constants()
    pool_w2 = pool_w.reshape(1, D_POOL, POOL_GROUP_DIM)
    return pl.pallas_call(
        functools.partial(_merge_kernel, tm=tm, rows_total=L // GRID_W),
        grid=(nt,),
        in_specs=[pl.BlockSpec((tm, D_SSD), row), pl.BlockSpec((tm, D_SSD), row),
                  pl.BlockSpec((tm, D_SSD), row),
                  pl.BlockSpec((tm, D_POOL), row),
                  pl.BlockSpec((POOL_HALO, D_POOL), lambda i: (jnp.maximum(i * hb - 1, 0), 0)),
                  pl.BlockSpec((POOL_HALO, D_POOL),
                               lambda i: (jnp.minimum((i + 1) * hb, L // POOL_HALO - 1), 0)),
                  pl.BlockSpec((tm, D_MODEL), row),
                  vec, hbm, vec, hbm, vec, vec, vec]
                 + [pl.BlockSpec(bm.shape, const) for bm in bands]
                 + [pl.BlockSpec(inv_cols.shape, const3)],
        out_specs=pl.BlockSpec((tm, D_MODEL), row),
        out_shape=jax.ShapeDtypeStruct((L, D_MODEL), F32),
        scratch_shapes=[pltpu.VMEM(pool_w2.shape[1:], BF16), pltpu.VMEM(w_out.shape[1:], BF16),
                        pltpu.VMEM((2, MERGE_STAGE_ROWS, pool_w2.shape[2]), F32),
                        pltpu.VMEM((2, MERGE_STAGE_ROWS, w_out.shape[2]), F32),
                        pltpu.SemaphoreType.DMA((4,))],
        compiler_params=pltpu.CompilerParams(dimension_semantics=("arbitrary",),
                                             vmem_limit_bytes=VMEM_LIMIT),
        name="merge",
    )(yf, yb, z, up, up, up, xln, norm_g, pool_w2, pool_scale, w_out, g1, l1g, l1b, *bands, inv_cols)


def _ffn_kernel(x_ref, sh_ref, sc_ref, g2_ref, wg_hbm, wu_hbm, wd_hbm, lg_ref, lb_ref, out_ref,
                wg_ref, wu_ref, wd_ref, stage_wide, stage_tall, sem):
    @pl.when(pl.program_id(0) == 0)
    def _():
        _round_weights([(wg_hbm, wg_ref, stage_wide, 0), (wu_hbm, wu_ref, stage_wide, 0),
                        (wd_hbm, wd_ref, stage_tall, 2)], sem)

    half = D_FF // 2
    for r in range(0, x_ref.shape[0], FFN_ROWS):
        x = x_ref[r:r + FFN_ROWS, :]
        h = (x * (1.0 + sc_ref[...]) + sh_ref[...]).astype(BF16)
        ffn = None
        for s in (0, half):
            gate = _dot(h, wg_ref[:, s:s + half])
            upv = _dot(h, wu_ref[:, s:s + half])
            part = _dot((_silu(gate) * upv).astype(BF16), wd_ref[s:s + half, :])
            ffn = part if ffn is None else ffn + part
        out_ref[r:r + FFN_ROWS, :] = _layer_norm(DEEPNORM_ALPHA * x + g2_ref[...] * ffn, lg_ref[...], lb_ref[...])


def _ffn_call(x, sh, sc, g2, wg, wu, wd, lg, lb, *, tm):
    L = x.shape[0]
    const = lambda i: (0, 0)
    row = lambda i: (i, 0)
    vec = pl.BlockSpec((1, D_MODEL), const)
    hbm = pl.BlockSpec(memory_space=pl.ANY)
    return pl.pallas_call(
        _ffn_kernel,
        grid=(L // tm,),
        in_specs=[pl.BlockSpec((tm, D_MODEL), row), vec, vec, vec, hbm, hbm, hbm, vec, vec],
        out_specs=pl.BlockSpec((tm, D_MODEL), row),
        out_shape=jax.ShapeDtypeStruct((L, D_MODEL), F32),
        scratch_shapes=[pltpu.VMEM(wg.shape[1:], BF16), pltpu.VMEM(wu.shape[1:], BF16),
                        pltpu.VMEM(wd.shape[1:], BF16),
                        pltpu.VMEM((2, D_MODEL // FFN_STAGE_CHUNKS, D_FF), F32),
                        pltpu.VMEM((2, D_FF // FFN_STAGE_CHUNKS, D_MODEL), F32),
                        pltpu.SemaphoreType.DMA((4,))],
        compiler_params=pltpu.CompilerParams(dimension_semantics=("arbitrary",),
                                             vmem_limit_bytes=VMEM_LIMIT),
        name="ffn",
    )(x, sh, sc, g2, wg, wu, wd, lg, lb)


def kernel(x, c, ctx, c_ctx, emb_ln_g, emb_ln_b, w_ada, b_ada, in_proj, conv_w, conv_b, dt_bias, a_log,
           d_skip, ssd_norm_g, pool_w, pool_scale, w_out, ln1_g, ln1_b, w_gate, w_up, w_down, ln2_g, ln2_b):
    assert x.shape[0] == 1 and w_ada.shape[0] == DEPTH == 1
    xl, xc = x[0], ctx[0]
    rowv = lambda v: v.reshape(1, -1)
    elg, elb = rowv(emb_ln_g), rowv(emb_ln_b)

    mod = _mod_call(jnp.stack([c[0], c_ctx], axis=1), w_ada, rowv(b_ada[0]))
    sh1, sc1, g1, sh2, sc2, g2 = [mod[0:1, k * D_MODEL:(k + 1) * D_MODEL] for k in range(6)]
    sh1c, sc1c = mod[1:2, 0:D_MODEL], mod[1:2, D_MODEL:2 * D_MODEL]

    conv_args = (conv_w, rowv(conv_b[0]), dt_bias[0].reshape(N_DH, 1), a_log[0].reshape(N_DH, 1))

    dskip = rowv(jnp.repeat(d_skip[0], SSD_HEAD_DIM))
    h_zero = jnp.zeros((D_STATE, D_SSD), F32)

    _, xbc_c, _, _, rp_c, cs_c, wz, wxd, wp = _inproj_call(xc, elg, elb, sh1c, sc1c, jnp.swapaxes(in_proj, 1, 2),
                                                           *conv_args, tm=xc.shape[0])
    _, _, hf_ctx, hb_ctx = _ssd_call(xbc_c, rp_c, cs_c, dskip, h_zero, h_zero)

    z, xbc, up, xln, rp, cs = _inproj_call(xl, elg, elb, sh1, sc1, (wz, wxd, wp), *conv_args, tm=1024)
    yf, yb, _, _ = _ssd_call(xbc, rp, cs, dskip, hf_ctx, hb_ctx)
    x1 = _merge_call(yf, yb, z, up, xln, rowv(ssd_norm_g[0]), pool_w, rowv(pool_scale[0]),
                     w_out, g1, rowv(ln1_g[0]), rowv(ln1_b[0]), tm=1024)
    x2 = _ffn_call(x1, sh2, sc2, g2, w_gate, w_up, w_down, rowv(ln2_g[0]), rowv(ln2_b[0]), tm=1024)
    return x2[None]
```

```python
import functools

import jax
import jax.numpy as jnp
import numpy as np
from jax import lax
from jax.experimental import pallas as pl
from jax.experimental.pallas import tpu as pltpu

F32 = jnp.float32
BF16 = jnp.bfloat16

D_MODEL = 1024
SSD_HEADS = 16
SSD_HEAD_DIM = 64
D_SSD = SSD_HEADS * SSD_HEAD_DIM
D_STATE = 128
D_CONV = 5
CHUNK = 128
D_POOL = 1024
POOL_WINDOWS = (2, 4, 8, 16)
POOL_GROUP_DIM = D_POOL // len(POOL_WINDOWS)
GRID_W = 64
D_XBC = D_SSD + 2 * D_STATE
D_XD = D_XBC + 128
D_FF = 2816
DEPTH = 1
DEEPNORM_ALPHA = (2 * DEPTH) ** 0.25
LN_EPS = 1e-5
LOG2_E = 1.4426950408889634

HALO = 16
POOL_HALO = 512
POOL_SUB = 256
SSD_BLOCK = 1024
PROJ_BLOCK = 256
CONV_BLOCK = 256
MOD_KB = 256
FFN_STAGE_CHUNKS = 8
FFN_TILE = 512
FFN_ROWS = 256
MERGE_ROWS = 512
MERGE_STAGE_ROWS = 128
N_DH = 2 * SSD_HEADS
VMEM_LIMIT = 56 * 1024 * 1024


def _dot(a, b):
    return jnp.dot(a, b, preferred_element_type=F32)


def _dot_nt(a, b):
    return lax.dot_general(a, b, (((1,), (1,)), ((), ())), preferred_element_type=F32)


def _silu(x):
    hx = 0.5 * x
    return hx + hx * jnp.tanh(hx)


def _layer_norm(x, g, b):
    mu = jnp.mean(x, axis=-1, keepdims=True)
    xc = x - mu
    var = jnp.mean(xc * xc, axis=-1, keepdims=True)
    return xc * lax.rsqrt(var + LN_EPS) * g + b


def _round_weights(jobs, sem):
    steps = []
    for src, dst, stage, sem0 in jobs:
        rows = stage.shape[1]
        for k in range(dst.shape[0] // rows):
            slot = k % 2
            copy = pltpu.make_async_copy(src.at[0, pl.ds(k * rows, rows)], stage.at[slot], sem.at[sem0 + slot])
            steps.append((copy, stage, slot, dst, k * rows, rows))
    steps[0][0].start()
    for n, (copy, stage, slot, dst, r0, rows) in enumerate(steps):
        if n + 1 < len(steps):
            steps[n + 1][0].start()
        copy.wait()
        dst[r0:r0 + rows, :] = stage[slot].astype(BF16)


def _mod_kernel(cc_ref, w_ref, b_ref, out_ref):
    k = pl.program_id(0)

    @pl.when(k == 0)
    def _():
        out_ref[...] = jnp.broadcast_to(b_ref[...], out_ref.shape)

    s = _silu(cc_ref[...])
    w = w_ref[...]
    out_ref[0:1, :] += jnp.sum(w * s[:, 0:1], axis=0, keepdims=True)
    out_ref[1:2, :] += jnp.sum(w * s[:, 1:2], axis=0, keepdims=True)


def _mod_call(cc, w_ada, b_ada):
    n = w_ada.shape[-1]
    return pl.pallas_call(
        _mod_kernel,
        grid=(D_MODEL // MOD_KB,),
        in_specs=[pl.BlockSpec((MOD_KB, 2), lambda k: (k, 0)),
                  pl.BlockSpec((None, MOD_KB, n), lambda k: (0, k, 0)),
                  pl.BlockSpec((1, n), lambda k: (0, 0))],
        out_specs=pl.BlockSpec((8, n), lambda k: (0, 0)),
        out_shape=jax.ShapeDtypeStruct((8, n), F32),
        compiler_params=pltpu.CompilerParams(dimension_semantics=("arbitrary",),
                                             vmem_limit_bytes=VMEM_LIMIT),
        name="mod",
    )(cc, w_ada, b_ada)


def _inproj_kernel(x_ref, xp_ref, xn_ref, lng_ref, lnb_ref, sh_ref, sc_ref, wz_ref, wxd_ref, wp_ref,
                   cw_ref, cb_ref, dtb_ref, alog_ref,
                   z_ref, xbc_ref, up_ref, xln_ref, rowpack_ref, colsplit_ref, h_scr, *, tm):
    i = pl.program_id(0)
    n = pl.num_programs(0)
    lng, lnb, sh, sc = lng_ref[...], lnb_ref[...], sh_ref[...], sc_ref[...]

    def modulated(xn):
        return xn * (1.0 + sc) + sh

    xln = _layer_norm(x_ref[...], lng, lnb)
    xln_ref[...] = xln
    h = modulated(xln).astype(BF16)
    hp = (modulated(_layer_norm(xp_ref[...], lng, lnb)) * (i > 0).astype(F32)).astype(BF16)
    hn = (modulated(_layer_norm(xn_ref[...], lng, lnb)) * (i < n - 1).astype(F32)).astype(BF16)

    h_scr[0:HALO, :] = hp
    h_scr[HALO:HALO + tm, :] = h
    h_scr[HALO + tm:HALO + tm + HALO, :] = hn
    dt_raw = _dot(h_scr[HALO:HALO + tm, :], wxd_ref[:, D_XBC:D_XD]).T[0:N_DH, :] + dtb_ref[...]
    dt = jnp.maximum(dt_raw, 0.0) + jnp.log(1.0 + jnp.exp(-jnp.abs(dt_raw)))
    a = dt * (-jnp.exp(alog_ref[...]))
    lane = lax.broadcasted_iota(jnp.int32, (N_DH, tm), 1) & (CHUNK - 1)
    row = lax.broadcasted_iota(jnp.int32, (N_DH, tm), 0)
    cf, cr = a, a
    k = 1
    while k < CHUNK:
        cf = cf + jnp.where(lane >= k, pltpu.roll(cf, k, 1), 0.0)
        cr = cr + jnp.where(lane < CHUNK - k, pltpu.roll(cr, tm - k, 1), 0.0)
        k *= 2
    is_fwd = row < SSD_HEADS
    acum = jnp.where(is_fwd, cf, cr)
    e = jnp.exp(acum)
    w_end = jnp.exp(jnp.where(is_fwd, cr, cf) - a) * dt
    p = acum * LOG2_E
    src = p - jnp.log2(dt)
    diag = jnp.log2(dt[0:SSD_HEADS] + dt[SSD_HEADS:])
    rowpack_ref[...] = jnp.concatenate([src, diag, jnp.zeros_like(diag), e, w_end], axis=0)
    p1 = p.astype(BF16).astype(F32)
    p2 = (p - p1).astype(BF16).astype(F32)
    p3 = (p - p1 - p2).astype(BF16).astype(F32)
    colsplit_ref[...] = jnp.concatenate([p1, p2, p3, jnp.zeros_like(p1)], axis=0).T.astype(BF16)

    ne = tm + 2 * HALO
    nb, zb = CONV_BLOCK, PROJ_BLOCK
    zu_blocks = [(w_ref, o_ref, c) for w_ref, o_ref in ((wz_ref, z_ref), (wp_ref, up_ref))
                 for c in range(0, D_SSD, zb)]
    for j, c0 in enumerate(range(0, D_XBC, nb)):
        cols = slice(c0, c0 + nb)
        u = _dot(h_scr[...], wxd_ref[:, cols])
        take = -(-len(zu_blocks) // (D_XBC // nb - j))
        for w_ref, o_ref, c in zu_blocks[:take]:
            o_ref[:, c:c + zb] = _dot(h_scr[HALO:HALO + tm, :], w_ref[:, c:c + zb]).astype(BF16)
        zu_blocks = zu_blocks[take:]
        f = [cw_ref[k:k + 1, cols] * u for k in range(D_CONV)]
        after = pltpu.roll(f[3] + pltpu.roll(f[4], ne - 1, 0), ne - 1, 0)
        before = pltpu.roll(f[1] + pltpu.roll(f[0], 1, 0), 1, 0)
        acc = (f[2] + after + before)[HALO:HALO + tm] + cb_ref[:, cols]
        xbc_ref[:, cols] = _silu(acc).astype(BF16)


def _inproj_convert_kernel(x_ref, xp_ref, xn_ref, lng_ref, lnb_ref, sh_ref, sc_ref, wf_ref,
                           cw_ref, cb_ref, dtb_ref, alog_ref,
                           z_ref, xbc_ref, up_ref, xln_ref, rowpack_ref, colsplit_ref,
                           wz_ref, wxd_ref, wp_ref, h_scr, *, tm):
    @pl.when(pl.program_id(0) == 0)
    def _():
        wz_ref[...] = wf_ref[0:D_SSD, :].T.astype(BF16)
        wxd_ref[...] = wf_ref[D_SSD:D_SSD + D_XD, :].T.astype(BF16)
        o3 = D_SSD + D_XBC + N_DH
        wp_ref[...] = wf_ref[o3:o3 + D_POOL, :].T.astype(BF16)

    _inproj_kernel(x_ref, xp_ref, xn_ref, lng_ref, lnb_ref, sh_ref, sc_ref, wz_ref, wxd_ref, wp_ref,
                   cw_ref, cb_ref, dtb_ref, alog_ref,
                   z_ref, xbc_ref, up_ref, xln_ref, rowpack_ref, colsplit_ref, h_scr, tm=tm)


def _inproj_call(x, ln_g, ln_b, sh, sc, weights, conv_w, conv_b, dt_bias, a_log, *, tm):
    L = x.shape[0]
    nt = L // tm
    hb = tm // HALO
    const = lambda i: (0, 0)
    row = lambda i: (i, 0)
    w_shapes = [(D_MODEL, D_SSD), (D_MODEL, D_XD), (D_MODEL, D_POOL)]
    w_specs = [pl.BlockSpec(s, const) for s in w_shapes]
    convert = not isinstance(weights, tuple)
    if convert:
        body = _inproj_convert_kernel
        w_in_specs = [pl.BlockSpec((None,) + weights.shape[1:], lambda i: (0, 0, 0),
                                   pipeline_mode=pl.Buffered(1))]
        w_args = (weights,)
    else:
        body, w_args = _inproj_kernel, weights
        w_in_specs = [pl.BlockSpec(s, const, pipeline_mode=pl.Buffered(1)) for s in w_shapes]
    return pl.pallas_call(
        functools.partial(body, tm=tm),
        grid=(nt,),
        in_specs=[pl.BlockSpec((tm, D_MODEL), row),
                  pl.BlockSpec((HALO, D_MODEL), lambda i: (jnp.maximum(i * hb - 1, 0), 0)),
                  pl.BlockSpec((HALO, D_MODEL), lambda i: (jnp.minimum((i + 1) * hb, L // HALO - 1), 0)),
                  pl.BlockSpec((1, D_MODEL), const), pl.BlockSpec((1, D_MODEL), const),
                  pl.BlockSpec((1, D_MODEL), const), pl.BlockSpec((1, D_MODEL), const)]
                 + w_in_specs
                 + [pl.BlockSpec((None, D_CONV, D_XBC), lambda i: (0, 0, 0)), pl.BlockSpec((1, D_XBC), const),
                    pl.BlockSpec((N_DH, 1), const), pl.BlockSpec((N_DH, 1), const)],
        out_specs=[pl.BlockSpec((tm, D_SSD), row),
                   pl.BlockSpec((tm, D_XBC), row),
                   pl.BlockSpec((tm, D_POOL), row),
                   pl.BlockSpec((tm, D_MODEL), row),
                   pl.BlockSpec((4 * N_DH, tm), lambda i: (0, i)),
                   pl.BlockSpec((tm, 4 * N_DH), row)] + (w_specs if convert else []),
        out_shape=[jax.ShapeDtypeStruct((L, D_SSD), BF16),
                   jax.ShapeDtypeStruct((L, D_XBC), BF16),
                   jax.ShapeDtypeStruct((L, D_POOL), BF16),
                   jax.ShapeDtypeStruct((L, D_MODEL), F32),
                   jax.ShapeDtypeStruct((4 * N_DH, L), F32),
                   jax.ShapeDtypeStruct((L, 4 * N_DH), BF16)]
                  + ([jax.ShapeDtypeStruct(s, BF16) for s in w_shapes] if convert else []),
        scratch_shapes=[pltpu.VMEM((tm + 2 * HALO, D_MODEL), BF16)],
        compiler_params=pltpu.CompilerParams(dimension_semantics=("arbitrary",),
                                             vmem_limit_bytes=VMEM_LIMIT),
        name="inproj",
    )(x, x, x, ln_g, ln_b, sh, sc, *w_args, conv_w, conv_b, dt_bias, a_log)


def _across_lanes(rows, width):
    q = rows.shape[1]
    tall = jnp.concatenate([jnp.broadcast_to(rows[k:k + 1, :], (width, q)) for k in range(rows.shape[0])],
                           axis=0)
    return tall.T


def _lane_tile_matrix():
    m = np.zeros((4 * N_DH, N_DH * CHUNK), np.float32)
    for dh in range(N_DH):
        for piece in range(3):
            m[piece * N_DH + dh, dh * CHUNK:(dh + 1) * CHUNK] = 1.0
    return jnp.asarray(m, BF16)


def _ssd_kernel(xbcf_ref, xbcb_ref, rpf_ref, rpb_ref, cs_ref, xa_ref, dskip_ref, h0f_ref, h0b_ref,
                yf_ref, yb_ref, hf_ref, hb_ref):
    q = CHUNK
    nh = SSD_HEADS
    nsub = xbcf_ref.shape[0] // CHUNK

    @pl.when(pl.program_id(0) == 0)
    def _():
        hf_ref[...] = h0f_ref[...]
        hb_ref[...] = h0b_ref[...]

    def stream(xbc_ref, rp_ref, rows, d, h_ref, exit_row, emit):
        xs = xbc_ref[rows, 0:D_SSD]
        bm = xbc_ref[rows, D_SSD:D_SSD + D_STATE]
        cm = xbc_ref[rows, D_SSD + D_STATE:D_XBC]
        e = _across_lanes(rp_ref[(4 + d) * nh:(5 + d) * nh, rows], SSD_HEAD_DIM)
        w_end = _across_lanes(rp_ref[(6 + d) * nh:(7 + d) * nh, rows], SSD_HEAD_DIM)
        yield
        st = h_ref[...]
        emit(_dot(cm, st.astype(BF16)) * e)
        yield
        xw = (xs.astype(F32) * w_end).astype(BF16)
        upd = lax.dot_general(bm, xw, (((0,), (0,)), ((), ())), preferred_element_type=F32)
        h_ref[...] = st * e[exit_row:exit_row + 1, :] + upd
        yield

    def lead(sub, out):
        fr = slice(sub * q, (sub + 1) * q)
        br = slice((nsub - 1 - sub) * q, (nsub - sub) * q)

        def store_back(v):
            yb_ref[br, :] = v.astype(BF16)

        yield from stream(xbcf_ref, rpf_ref, fr, 0, hf_ref, q - 1, lambda v: out.update(y_off_f=v))
        yield from stream(xbcb_ref, rpb_ref, br, 1, hb_ref, 0, store_back)
        out.update(acol_f=_dot(cs_ref[fr, :], xa_ref[:, 0:nh * q]))
        yield
        out.update(acol_b=_dot(cs_ref[fr, :], xa_ref[:, nh * q:]),
                   cb=_dot_nt(xbcf_ref[fr, D_SSD + D_STATE:D_XBC], xbcf_ref[fr, D_SSD:D_SSD + D_STATE]))
        yield

    ti = lax.broadcasted_iota(jnp.int32, (q, q), 0)
    ui = lax.broadcasted_iota(jnp.int32, (q, q), 1)
    lane = lax.broadcasted_iota(jnp.int32, (q, 2 * SSD_HEAD_DIM), 1)

    def intra(sub, j, pre):
        fr = slice(sub * q, (sub + 1) * q)
        rp = rpf_ref[0:3 * nh, fr]
        cb = pre["cb"]
        ms = []
        for hh in (2 * j, 2 * j + 1):
            seg_f = pre["acol_f"][:, hh * q:(hh + 1) * q] - rp[hh:hh + 1, :]
            seg_b = pre["acol_b"][:, hh * q:(hh + 1) * q] - rp[nh + hh:nh + hh + 1, :]
            both = rp[2 * nh + hh:2 * nh + hh + 1, :]
            power = jnp.where(ui < ti, seg_f, jnp.where(ui > ti, seg_b, both))
            ms.append((cb * jnp.exp2(power)).astype(BF16))
        cols = slice(j * 2 * SSD_HEAD_DIM, (j + 1) * 2 * SSD_HEAD_DIM)
        xp = xbcf_ref[fr, cols]
        zero = jnp.zeros_like(xp)
        rhs = jnp.concatenate([jnp.where(lane < SSD_HEAD_DIM, xp, zero),
                               jnp.where(lane >= SSD_HEAD_DIM, xp, zero)], axis=0)
        y = (_dot(jnp.concatenate(ms, axis=1), rhs) + pre["y_off_f"][:, cols]
             + dskip_ref[:, cols] * xp.astype(F32))
        yf_ref[fr, cols] = y.astype(BF16)

    pre = [dict() for _ in range(nsub)]
    for _ in lead(0, pre[0]):
        pass
    for sub in range(nsub):
        nxt = lead(sub + 1, pre[sub + 1]) if sub + 1 < nsub else iter(())
        for j in range(SSD_HEADS // 2):
            intra(sub, j, pre[sub])
            next(nxt, None)
        for _ in nxt:
            pass


def _ssd_call(xbc, rowpack, colsplit, dskip, h0f, h0b):
    L = xbc.shape[0]
    blk = min(SSD_BLOCK, L)
    nc = L // blk
    const = lambda s: (0, 0)
    fwd = lambda s: (s, 0)
    bwd = lambda s: (nc - 1 - s, 0)
    st_shape = jax.ShapeDtypeStruct((D_STATE, D_SSD), F32)
    xa = _lane_tile_matrix()
    return pl.pallas_call(
        _ssd_kernel,
        grid=(nc,),
        in_specs=[pl.BlockSpec((blk, D_XBC), fwd), pl.BlockSpec((blk, D_XBC), bwd),
                  pl.BlockSpec((4 * N_DH, blk), lambda s: (0, s)),
                  pl.BlockSpec((4 * N_DH, blk), lambda s: (0, nc - 1 - s)),
                  pl.BlockSpec((blk, 4 * N_DH), fwd), pl.BlockSpec(xa.shape, const),
                  pl.BlockSpec((1, D_SSD), const),
                  pl.BlockSpec((D_STATE, D_SSD), const), pl.BlockSpec((D_STATE, D_SSD), const)],
        out_specs=[pl.BlockSpec((blk, D_SSD), fwd), pl.BlockSpec((blk, D_SSD), bwd),
                   pl.BlockSpec((D_STATE, D_SSD), const), pl.BlockSpec((D_STATE, D_SSD), const)],
        out_shape=[jax.ShapeDtypeStruct((L, D_SSD), BF16), jax.ShapeDtypeStruct((L, D_SSD), BF16),
                   st_shape, st_shape],
        compiler_params=pltpu.CompilerParams(dimension_semantics=("arbitrary",),
                                             vmem_limit_bytes=VMEM_LIMIT),
        name="ssd",
    )(xbc, xbc, rowpack, rowpack, colsplit, xa, dskip, h0f, h0b)


def _pool_constants():
    bands, inv_cols = [], []
    t = np.arange(POOL_SUB)
    rt, ct = t // GRID_W, t % GRID_W
    for w in POOL_WINDOWS:
        hw = w // 2
        k = np.arange(POOL_SUB + GRID_W * w)
        rk, ck = k // GRID_W, k % GRID_W
        band = ((rk[None, :] >= rt[:, None]) & (rk[None, :] < rt[:, None] + w)
                & (ck[None, :] >= ct[:, None] - hw) & (ck[None, :] < ct[:, None] + hw))
        bands.append(jnp.asarray(band, BF16))
        cnt_c = np.minimum(ct + hw, GRID_W) - np.maximum(ct - hw, 0)
        inv_cols.append(np.broadcast_to((1.0 / cnt_c)[:, None], (POOL_SUB, 128)))
    return bands, jnp.asarray(np.stack(inv_cols), F32)


def _merge_kernel(yf_ref, yb_ref, z_ref, up_ref, upp_ref, upn_ref, xln_ref, ng_ref, pw_hbm, ps_ref, wo_hbm,
                  g1_ref, l1g_ref, l1b_ref, band0_ref, band1_ref, band2_ref, band3_ref, invc_ref,
                  out_ref, pw_scr, wo_scr, stage_pw, stage_wo, sem, *, tm, rows_total):
    i = pl.program_id(0)
    n = pl.num_programs(0)

    @pl.when(i == 0)
    def _():
        _round_weights([(pw_hbm, pw_scr, stage_pw, 0), (wo_hbm, wo_scr, stage_wo, 2)], sem)

    c = POOL_GROUP_DIM
    band_refs = (band0_ref, band1_ref, band2_ref, band3_ref)
    keep_p = (i > 0).astype(BF16)
    keep_n = (i < n - 1).astype(BF16)
    sub_row = lax.broadcasted_iota(jnp.int32, (POOL_SUB, 128), 0) // GRID_W
    nsub = MERGE_ROWS // POOL_SUB
    exts = [jnp.concatenate([upp_ref[:, gi * c:(gi + 1) * c] * keep_p, up_ref[:, gi * c:(gi + 1) * c],
                             upn_ref[:, gi * c:(gi + 1) * c] * keep_n], axis=0)
            for gi in range(len(POOL_WINDOWS))]

    for t0 in range(0, tm, MERGE_ROWS):
        rows = slice(t0, t0 + MERGE_ROWS)
        y = yf_ref[rows, :].astype(F32) + yb_ref[rows, :].astype(F32)
        g = y * _silu(z_ref[rows, :].astype(F32))
        yn = (g * lax.rsqrt(jnp.mean(g * g, axis=-1, keepdims=True) + LN_EPS) * ng_ref[...]).astype(BF16)

        sums = []
        for gi, w in enumerate(POOL_WINDOWS):
            starts = [POOL_HALO + t0 + b * POOL_SUB - (w // 2) * GRID_W for b in range(nsub)]
            sums.append([_dot(band_refs[gi][...], exts[gi][s:s + POOL_SUB + GRID_W * w]) for s in starts])
        diffs = []
        for gi, w in enumerate(POOL_WINDOWS):
            hw = w // 2
            cols = slice(gi * c, (gi + 1) * c)
            parts = []
            for b in range(nsub):
                r0 = t0 + b * POOL_SUB
                row = sub_row + (i * tm + r0) // GRID_W
                cnt_r = jnp.minimum(row + hw, rows_total) - jnp.maximum(row - hw, 0)
                inv = invc_ref[gi] / cnt_r.astype(F32)
                u = up_ref[r0:r0 + POOL_SUB, cols].astype(F32)
                parts.append((sums[gi][b] * jnp.concatenate([inv, inv], axis=1) - u).astype(BF16))
            diffs.append(jnp.concatenate(parts, axis=0))
        p = [(_dot(diffs[gi], pw_scr[gi * c:(gi + 1) * c, :]) * ps_ref[:, gi * c:(gi + 1) * c]).astype(BF16)
             for gi in range(len(POOL_WINDOWS))]
        lhs = jnp.concatenate([yn] + p, axis=1)
        hr = MERGE_ROWS // 2
        for r in range(0, MERGE_ROWS, hr):
            mix = _dot(lhs[r:r + hr], wo_scr[...])
            out_ref[t0 + r:t0 + r + hr, :] = _layer_norm(
                DEEPNORM_ALPHA * xln_ref[t0 + r:t0 + r + hr, :] + g1_ref[...] * mix, l1g_ref[...], l1b_ref[...])


def _merge_call(yf, yb, z, up, xln, norm_g, pool_w, pool_scale, w_out, g1, l1g, l1b, *, tm):
    L = xln.shape[0]
    nt = L // tm
    hb = tm // POOL_HALO
    const = lambda i: (0, 0)
    const3 = lambda i: (0, 0, 0)
    row = lambda i: (i, 0)
    vec = pl.BlockSpec((1, D_MODEL), const)
    hbm = pl.BlockSpec(memory_space=pl.ANY)
    bands, inv_cols = _pool_constants()
    pool_w2 = pool_w.reshape(1, D_POOL, POOL_GROUP_DIM)
    return pl.pallas_call(
        functools.partial(_merge_kernel, tm=tm, rows_total=L // GRID_W),
        grid=(nt,),
        in_specs=[pl.BlockSpec((tm, D_SSD), row), pl.BlockSpec((tm, D_SSD), row),
                  pl.BlockSpec((tm, D_SSD), row),
                  pl.BlockSpec((tm, D_POOL), row),
                  pl.BlockSpec((POOL_HALO, D_POOL), lambda i: (jnp.maximum(i * hb - 1, 0), 0)),
                  pl.BlockSpec((POOL_HALO, D_POOL),
                               lambda i: (jnp.minimum((i + 1) * hb, L // POOL_HALO - 1), 0)),
                  pl.BlockSpec((tm, D_MODEL), row),
                  vec, hbm, vec, hbm, vec, vec, vec]
                 + [pl.BlockSpec(bm.shape, const) for bm in bands]
                 + [pl.BlockSpec(inv_cols.shape, const3)],
        out_specs=pl.BlockSpec((tm, D_MODEL), row),
        out_shape=jax.ShapeDtypeStruct((L, D_MODEL), F32),
        scratch_shapes=[pltpu.VMEM(pool_w2.shape[1:], BF16), pltpu.VMEM(w_out.shape[1:], BF16),
                        pltpu.VMEM((2, MERGE_STAGE_ROWS, pool_w2.shape[2]), F32),
                        pltpu.VMEM((2, MERGE_STAGE_ROWS, w_out.shape[2]), F32),
                        pltpu.SemaphoreType.DMA((4,))],
        compiler_params=pltpu.CompilerParams(dimension_semantics=("arbitrary",),
                                             vmem_limit_bytes=VMEM_LIMIT),
        name="merge",
    )(yf, yb, z, up, up, up, xln, norm_g, pool_w2, pool_scale, w_out, g1, l1g, l1b, *bands, inv_cols)


def _ffn_kernel(x_ref, sh_ref, sc_ref, g2_ref, wg_hbm, wu_hbm, wd_hbm, lg_ref, lb_ref, out_ref,
                wg_ref, wu_ref, wd_ref, stage_wide, stage_tall, sem):
    @pl.when(pl.program_id(0) == 0)
    def _():
        _round_weights([(wg_hbm, wg_ref, stage_wide, 0), (wu_hbm, wu_ref, stage_wide, 0),
                        (wd_hbm, wd_ref, stage_tall, 2)], sem)

    half = D_FF // 2
    for r in range(0, x_ref.shape[0], FFN_ROWS):
        x = x_ref[r:r + FFN_ROWS, :]
        h = (x * (1.0 + sc_ref[...]) + sh_ref[...]).astype(BF16)
        ffn = None
        for s in (0, half):
            gate = _dot(h, wg_ref[:, s:s + half])
            upv = _dot(h, wu_ref[:, s:s + half])
            part = _dot((_silu(gate) * upv).astype(BF16), wd_ref[s:s + half, :])
            ffn = part if ffn is None else ffn + part
        out_ref[r:r + FFN_ROWS, :] = _layer_norm(DEEPNORM_ALPHA * x + g2_ref[...] * ffn, lg_ref[...], lb_ref[...])


def _ffn_call(x, sh, sc, g2, wg, wu, wd, lg, lb, *, tm):
    L = x.shape[0]
    const = lambda i: (0, 0)
    row = lambda i: (i, 0)
    vec = pl.BlockSpec((1, D_MODEL), const)
    hbm = pl.BlockSpec(memory_space=pl.ANY)
    return pl.pallas_call(
        _ffn_kernel,
        grid=(L // tm,),
        in_specs=[pl.BlockSpec((tm, D_MODEL), row), vec, vec, vec, hbm, hbm, hbm, vec, vec],
        out_specs=pl.BlockSpec((tm, D_MODEL), row),
        out_shape=jax.ShapeDtypeStruct((L, D_MODEL), F32),
        scratch_shapes=[pltpu.VMEM(wg.shape[1:], BF16), pltpu.VMEM(wu.shape[1:], BF16),
                        pltpu.VMEM(wd.shape[1:], BF16),
                        pltpu.VMEM((2, D_MODEL // FFN_STAGE_CHUNKS, D_FF), F32),
                        pltpu.VMEM((2, D_FF // FFN_STAGE_CHUNKS, D_MODEL), F32),
                        pltpu.SemaphoreType.DMA((4,))],
        compiler_params=pltpu.CompilerParams(dimension_semantics=("arbitrary",),
                                             vmem_limit_bytes=VMEM_LIMIT),
        name="ffn",
    )(x, sh, sc, g2, wg, wu, wd, lg, lb)


def kernel(x, c, ctx, c_ctx, emb_ln_g, emb_ln_b, w_ada, b_ada, in_proj, conv_w, conv_b, dt_bias, a_log,
           d_skip, ssd_norm_g, pool_w, pool_scale, w_out, ln1_g, ln1_b, w_gate, w_up, w_down, ln2_g, ln2_b):
    assert x.shape[0] == 1 and w_ada.shape[0] == DEPTH == 1
    xl, xc = x[0], ctx[0]
    rowv = lambda v: v.reshape(1, -1)
    elg, elb = rowv(emb_ln_g), rowv(emb_ln_b)

    mod = _mod_call(jnp.stack([c[0], c_ctx], axis=1), w_ada, rowv(b_ada[0]))
    sh1, sc1, g1, sh2, sc2, g2 = [mod[0:1, k * D_MODEL:(k + 1) * D_MODEL] for k in range(6)]
    sh1c, sc1c = mod[1:2, 0:D_MODEL], mod[1:2, D_MODEL:2 * D_MODEL]

    conv_args = (conv_w, rowv(conv_b[0]), dt_bias[0].reshape(N_DH, 1), a_log[0].reshape(N_DH, 1))

    dskip = rowv(jnp.repeat(d_skip[0], SSD_HEAD_DIM))
    h_zero = jnp.zeros((D_STATE, D_SSD), F32)

    _, xbc_c, _, _, rp_c, cs_c, wz, wxd, wp = _inproj_call(xc, elg, elb, sh1c, sc1c, jnp.swapaxes(in_proj, 1, 2),
                                                           *conv_args, tm=xc.shape[0])
    _, _, hf_ctx, hb_ctx = _ssd_call(xbc_c, rp_c, cs_c, dskip, h_zero, h_zero)

    z, xbc, up, xln, rp, cs = _inproj_call(xl, elg, elb, sh1, sc1, (wz, wxd, wp), *conv_args, tm=1024)
    yf, yb, _, _ = _ssd_call(xbc, rp, cs, dskip, hf_ctx, hb_ctx)
    x1 = _merge_call(yf, yb, z, up, xln, rowv(ssd_norm_g[0]), pool_w, rowv(pool_scale[0]),
                     w_out, g1, rowv(ln1_g[0]), rowv(ln1_b[0]), tm=MERGE_ROWS)
    x2 = _ffn_call(x1, sh2, sc2, g2, w_gate, w_up, w_down, rowv(ln2_g[0]), rowv(ln2_b[0]), tm=FFN_TILE)
    return x2[None]
```

```python
import functools

import jax
import jax.numpy as jnp
import numpy as np
from jax import lax
from jax.experimental import pallas as pl
from jax.experimental.pallas import tpu as pltpu

F32 = jnp.float32
BF16 = jnp.bfloat16

D_MODEL = 1024
SSD_HEADS = 16
SSD_HEAD_DIM = 64
D_SSD = SSD_HEADS * SSD_HEAD_DIM
D_STATE = 128
D_CONV = 5
CHUNK = 128
D_POOL = 1024
POOL_WINDOWS = (2, 4, 8, 16)
POOL_GROUP_DIM = D_POOL // len(POOL_WINDOWS)
GRID_W = 64
D_XBC = D_SSD + 2 * D_STATE
D_XD = D_XBC + 128
D_FF = 2816
DEPTH = 1
DEEPNORM_ALPHA = (2 * DEPTH) ** 0.25
LN_EPS = 1e-5
LOG2_E = 1.4426950408889634

HALO = 16
POOL_HALO = 512
POOL_SUB = 256
SSD_BLOCK = 1024
PROJ_BLOCK = 256
CONV_BLOCK = 256
MOD_KB = 256
FFN_STAGE_CHUNKS = 8
FFN_TILE = 512
MERGE_TILE = 512
INPROJ_TILE = 1024
N_DH = 2 * SSD_HEADS
VMEM_LIMIT = 56 * 1024 * 1024


def _dot(a, b):
    return jnp.dot(a, b, preferred_element_type=F32)


def _dot_nt(a, b):
    return lax.dot_general(a, b, (((1,), (1,)), ((), ())), preferred_element_type=F32)


def _silu(x):
    hx = 0.5 * x
    return hx + hx * jnp.tanh(hx)


def _layer_norm(x, g, b):
    mu = jnp.mean(x, axis=-1, keepdims=True)
    xc = x - mu
    var = jnp.mean(xc * xc, axis=-1, keepdims=True)
    return xc * lax.rsqrt(var + LN_EPS) * g + b


def _round_weights(jobs, sem):
    steps = []
    for src, dst, stage, sem0 in jobs:
        rows = stage.shape[1]
        for k in range(dst.shape[0] // rows):
            slot = k % 2
            copy = pltpu.make_async_copy(src.at[0, pl.ds(k * rows, rows)], stage.at[slot], sem.at[sem0 + slot])
            steps.append((copy, stage, slot, dst, k * rows, rows))
    steps[0][0].start()
    for n, (copy, stage, slot, dst, r0, rows) in enumerate(steps):
        if n + 1 < len(steps):
            steps[n + 1][0].start()
        copy.wait()
        dst[r0:r0 + rows, :] = stage[slot].astype(BF16)


def _mod_kernel(cc_ref, w_ref, b_ref, out_ref):
    k = pl.program_id(0)

    @pl.when(k == 0)
    def _():
        out_ref[...] = jnp.broadcast_to(b_ref[...], out_ref.shape)

    s = _silu(cc_ref[...])
    w = w_ref[...]
    out_ref[0:1, :] += jnp.sum(w * s[:, 0:1], axis=0, keepdims=True)
    out_ref[1:2, :] += jnp.sum(w * s[:, 1:2], axis=0, keepdims=True)


def _mod_call(cc, w_ada, b_ada):
    n = w_ada.shape[-1]
    return pl.pallas_call(
        _mod_kernel,
        grid=(D_MODEL // MOD_KB,),
        in_specs=[pl.BlockSpec((MOD_KB, 2), lambda k: (k, 0)),
                  pl.BlockSpec((None, MOD_KB, n), lambda k: (0, k, 0)),
                  pl.BlockSpec((1, n), lambda k: (0, 0))],
        out_specs=pl.BlockSpec((8, n), lambda k: (0, 0)),
        out_shape=jax.ShapeDtypeStruct((8, n), F32),
        compiler_params=pltpu.CompilerParams(dimension_semantics=("arbitrary",),
                                             vmem_limit_bytes=VMEM_LIMIT),
        name="mod",
    )(cc, w_ada, b_ada)


def _inproj_kernel(x_ref, xp_ref, xn_ref, lng_ref, lnb_ref, sh_ref, sc_ref, wz_ref, wxd_ref, wp_ref,
                   cw_ref, cb_ref, dtb_ref, alog_ref,
                   z_ref, xbc_ref, up_ref, xln_ref, rowpack_ref, colsplit_ref, h_scr, *, tm):
    i = pl.program_id(0)
    n = pl.num_programs(0)
    lng, lnb, sh, sc = lng_ref[...], lnb_ref[...], sh_ref[...], sc_ref[...]

    def modulated(xn):
        return xn * (1.0 + sc) + sh

    xln = _layer_norm(x_ref[...], lng, lnb)
    xln_ref[...] = xln
    h = modulated(xln).astype(BF16)
    hp = (modulated(_layer_norm(xp_ref[...], lng, lnb)) * (i > 0).astype(F32)).astype(BF16)
    hn = (modulated(_layer_norm(xn_ref[...], lng, lnb)) * (i < n - 1).astype(F32)).astype(BF16)

    h_scr[0:HALO, :] = hp
    h_scr[HALO:HALO + tm, :] = h
    h_scr[HALO + tm:HALO + tm + HALO, :] = hn
    dt_raw = _dot(h_scr[HALO:HALO + tm, :], wxd_ref[:, D_XBC:D_XD]).T[0:N_DH, :] + dtb_ref[...]
    dt = jnp.maximum(dt_raw, 0.0) + jnp.log(1.0 + jnp.exp(-jnp.abs(dt_raw)))
    a = dt * (-jnp.exp(alog_ref[...]))
    lane = lax.broadcasted_iota(jnp.int32, (N_DH, tm), 1) & (CHUNK - 1)
    row = lax.broadcasted_iota(jnp.int32, (N_DH, tm), 0)
    cf, cr = a, a
    k = 1
    while k < CHUNK:
        cf = cf + jnp.where(lane >= k, pltpu.roll(cf, k, 1), 0.0)
        cr = cr + jnp.where(lane < CHUNK - k, pltpu.roll(cr, tm - k, 1), 0.0)
        k *= 2
    is_fwd = row < SSD_HEADS
    acum = jnp.where(is_fwd, cf, cr)
    e = jnp.exp(acum)
    w_end = jnp.exp(jnp.where(is_fwd, cr, cf) - a) * dt
    p = acum * LOG2_E
    src = p - jnp.log2(dt)
    diag = jnp.log2(dt[0:SSD_HEADS] + dt[SSD_HEADS:])
    rowpack_ref[...] = jnp.concatenate([src, diag, jnp.zeros_like(diag), e, w_end], axis=0)
    p1 = p.astype(BF16).astype(F32)
    p2 = (p - p1).astype(BF16).astype(F32)
    p3 = (p - p1 - p2).astype(BF16).astype(F32)
    colsplit_ref[...] = jnp.concatenate([p1, p2, p3, jnp.zeros_like(p1)], axis=0).T.astype(BF16)

    ne = tm + 2 * HALO
    nb, zb = CONV_BLOCK, PROJ_BLOCK
    zu_blocks = [(w_ref, o_ref, c) for w_ref, o_ref in ((wz_ref, z_ref), (wp_ref, up_ref))
                 for c in range(0, D_SSD, zb)]
    for j, c0 in enumerate(range(0, D_XBC, nb)):
        cols = slice(c0, c0 + nb)
        u = _dot(h_scr[...], wxd_ref[:, cols])
        take = -(-len(zu_blocks) // (D_XBC // nb - j))
        for w_ref, o_ref, c in zu_blocks[:take]:
            o_ref[:, c:c + zb] = _dot(h_scr[HALO:HALO + tm, :], w_ref[:, c:c + zb]).astype(BF16)
        zu_blocks = zu_blocks[take:]
        f = [cw_ref[k:k + 1, cols] * u for k in range(D_CONV)]
        after = pltpu.roll(f[3] + pltpu.roll(f[4], ne - 1, 0), ne - 1, 0)
        before = pltpu.roll(f[1] + pltpu.roll(f[0], 1, 0), 1, 0)
        acc = (f[2] + after + before)[HALO:HALO + tm] + cb_ref[:, cols]
        xbc_ref[:, cols] = _silu(acc).astype(BF16)


def _inproj_convert_kernel(x_ref, xp_ref, xn_ref, lng_ref, lnb_ref, sh_ref, sc_ref, wf_ref,
                           cw_ref, cb_ref, dtb_ref, alog_ref,
                           z_ref, xbc_ref, up_ref, xln_ref, rowpack_ref, colsplit_ref,
                           wz_ref, wxd_ref, wp_ref, h_scr, *, tm):
    @pl.when(pl.program_id(0) == 0)
    def _():
        wz_ref[...] = wf_ref[0:D_SSD, :].T.astype(BF16)
        wxd_ref[...] = wf_ref[D_SSD:D_SSD + D_XD, :].T.astype(BF16)
        o3 = D_SSD + D_XBC + N_DH
        wp_ref[...] = wf_ref[o3:o3 + D_POOL, :].T.astype(BF16)

    _inproj_kernel(x_ref, xp_ref, xn_ref, lng_ref, lnb_ref, sh_ref, sc_ref, wz_ref, wxd_ref, wp_ref,
                   cw_ref, cb_ref, dtb_ref, alog_ref,
                   z_ref, xbc_ref, up_ref, xln_ref, rowpack_ref, colsplit_ref, h_scr, tm=tm)


def _inproj_call(x, ln_g, ln_b, sh, sc, weights, conv_w, conv_b, dt_bias, a_log, *, tm):
    L = x.shape[0]
    nt = L // tm
    hb = tm // HALO
    const = lambda i: (0, 0)
    row = lambda i: (i, 0)
    w_shapes = [(D_MODEL, D_SSD), (D_MODEL, D_XD), (D_MODEL, D_POOL)]
    w_specs = [pl.BlockSpec(s, const) for s in w_shapes]
    convert = not isinstance(weights, tuple)
    if convert:
        body = _inproj_convert_kernel
        w_in_specs = [pl.BlockSpec((None,) + weights.shape[1:], lambda i: (0, 0, 0),
                                   pipeline_mode=pl.Buffered(1))]
        w_args = (weights,)
    else:
        body, w_args = _inproj_kernel, weights
        w_in_specs = [pl.BlockSpec(s, const, pipeline_mode=pl.Buffered(1)) for s in w_shapes]
    return pl.pallas_call(
        functools.partial(body, tm=tm),
        grid=(nt,),
        in_specs=[pl.BlockSpec((tm, D_MODEL), row),
                  pl.BlockSpec((HALO, D_MODEL), lambda i: (jnp.maximum(i * hb - 1, 0), 0)),
                  pl.BlockSpec((HALO, D_MODEL), lambda i: (jnp.minimum((i + 1) * hb, L // HALO - 1), 0)),
                  pl.BlockSpec((1, D_MODEL), const), pl.BlockSpec((1, D_MODEL), const),
                  pl.BlockSpec((1, D_MODEL), const), pl.BlockSpec((1, D_MODEL), const)]
                 + w_in_specs
                 + [pl.BlockSpec((None, D_CONV, D_XBC), lambda i: (0, 0, 0)), pl.BlockSpec((1, D_XBC), const),
                    pl.BlockSpec((N_DH, 1), const), pl.BlockSpec((N_DH, 1), const)],
        out_specs=[pl.BlockSpec((tm, D_SSD), row),
                   pl.BlockSpec((tm, D_XBC), row),
                   pl.BlockSpec((tm, D_POOL), row),
                   pl.BlockSpec((tm, D_MODEL), row),
                   pl.BlockSpec((4 * N_DH, tm), lambda i: (0, i)),
                   pl.BlockSpec((tm, 4 * N_DH), row)] + (w_specs if convert else []),
        out_shape=[jax.ShapeDtypeStruct((L, D_SSD), BF16),
                   jax.ShapeDtypeStruct((L, D_XBC), BF16),
                   jax.ShapeDtypeStruct((L, D_POOL), BF16),
                   jax.ShapeDtypeStruct((L, D_MODEL), F32),
                   jax.ShapeDtypeStruct((4 * N_DH, L), F32),
                   jax.ShapeDtypeStruct((L, 4 * N_DH), BF16)]
                  + ([jax.ShapeDtypeStruct(s, BF16) for s in w_shapes] if convert else []),
        scratch_shapes=[pltpu.VMEM((tm + 2 * HALO, D_MODEL), BF16)],
        compiler_params=pltpu.CompilerParams(dimension_semantics=("arbitrary",),
                                             vmem_limit_bytes=VMEM_LIMIT),
        name="inproj",
    )(x, x, x, ln_g, ln_b, sh, sc, *w_args, conv_w, conv_b, dt_bias, a_log)


def _across_lanes(rows, width):
    q = rows.shape[1]
    tall = jnp.concatenate([jnp.broadcast_to(rows[k:k + 1, :], (width, q)) for k in range(rows.shape[0])],
                           axis=0)
    return tall.T


def _lane_tile_matrix():
    m = np.zeros((4 * N_DH, N_DH * CHUNK), np.float32)
    for dh in range(N_DH):
        for piece in range(3):
            m[piece * N_DH + dh, dh * CHUNK:(dh + 1) * CHUNK] = 1.0
    return jnp.asarray(m, BF16)


def _ssd_kernel(xbcf_ref, xbcb_ref, rpf_ref, rpb_ref, cs_ref, xa_ref, dskip_ref, h0f_ref, h0b_ref,
                yf_ref, yb_ref, hf_ref, hb_ref):
    q = CHUNK
    nh = SSD_HEADS
    nsub = xbcf_ref.shape[0] // CHUNK

    @pl.when(pl.program_id(0) == 0)
    def _():
        hf_ref[...] = h0f_ref[...]
        hb_ref[...] = h0b_ref[...]

    def stream(xbc_ref, rp_ref, rows, d, h_ref, exit_row, emit):
        xs = xbc_ref[rows, 0:D_SSD]
        bm = xbc_ref[rows, D_SSD:D_SSD + D_STATE]
        cm = xbc_ref[rows, D_SSD + D_STATE:D_XBC]
        e = _across_lanes(rp_ref[(4 + d) * nh:(5 + d) * nh, rows], SSD_HEAD_DIM)
        w_end = _across_lanes(rp_ref[(6 + d) * nh:(7 + d) * nh, rows], SSD_HEAD_DIM)
        yield
        st = h_ref[...]
        emit(_dot(cm, st.astype(BF16)) * e)
        yield
        xw = (xs.astype(F32) * w_end).astype(BF16)
        upd = lax.dot_general(bm, xw, (((0,), (0,)), ((), ())), preferred_element_type=F32)
        h_ref[...] = st * e[exit_row:exit_row + 1, :] + upd
        yield

    def lead(sub, out):
        fr = slice(sub * q, (sub + 1) * q)
        br = slice((nsub - 1 - sub) * q, (nsub - sub) * q)

        def store_back(v):
            yb_ref[br, :] = v.astype(BF16)

        yield from stream(xbcf_ref, rpf_ref, fr, 0, hf_ref, q - 1, lambda v: out.update(y_off_f=v))
        yield from stream(xbcb_ref, rpb_ref, br, 1, hb_ref, 0, store_back)
        out.update(acol_f=_dot(cs_ref[fr, :], xa_ref[:, 0:nh * q]))
        yield
        out.update(acol_b=_dot(cs_ref[fr, :], xa_ref[:, nh * q:]),
                   cb=_dot_nt(xbcf_ref[fr, D_SSD + D_STATE:D_XBC], xbcf_ref[fr, D_SSD:D_SSD + D_STATE]))
        yield

    ti = lax.broadcasted_iota(jnp.int32, (q, q), 0)
    ui = lax.broadcasted_iota(jnp.int32, (q, q), 1)
    lane = lax.broadcasted_iota(jnp.int32, (q, 2 * SSD_HEAD_DIM), 1)

    def intra(sub, j, pre):
        fr = slice(sub * q, (sub + 1) * q)
        rp = rpf_ref[0:3 * nh, fr]
        cb = pre["cb"]
        ms = []
        for hh in (2 * j, 2 * j + 1):
            seg_f = pre["acol_f"][:, hh * q:(hh + 1) * q] - rp[hh:hh + 1, :]
            seg_b = pre["acol_b"][:, hh * q:(hh + 1) * q] - rp[nh + hh:nh + hh + 1, :]
            both = rp[2 * nh + hh:2 * nh + hh + 1, :]
            power = jnp.where(ui < ti, seg_f, jnp.where(ui > ti, seg_b, both))
            ms.append((cb * jnp.exp2(power)).astype(BF16))
        cols = slice(j * 2 * SSD_HEAD_DIM, (j + 1) * 2 * SSD_HEAD_DIM)
        xp = xbcf_ref[fr, cols]
        zero = jnp.zeros_like(xp)
        rhs = jnp.concatenate([jnp.where(lane < SSD_HEAD_DIM, xp, zero),
                               jnp.where(lane >= SSD_HEAD_DIM, xp, zero)], axis=0)
        y = (_dot(jnp.concatenate(ms, axis=1), rhs) + pre["y_off_f"][:, cols]
             + dskip_ref[:, cols] * xp.astype(F32))
        yf_ref[fr, cols] = y.astype(BF16)

    pre = [dict() for _ in range(nsub)]
    for _ in lead(0, pre[0]):
        pass
    for sub in range(nsub):
        nxt = lead(sub + 1, pre[sub + 1]) if sub + 1 < nsub else iter(())
        for j in range(SSD_HEADS // 2):
            intra(sub, j, pre[sub])
            next(nxt, None)
        for _ in nxt:
            pass


def _ssd_call(xbc, rowpack, colsplit, dskip, h0f, h0b):
    L = xbc.shape[0]
    blk = min(SSD_BLOCK, L)
    nc = L // blk
    const = lambda s: (0, 0)
    fwd = lambda s: (s, 0)
    bwd = lambda s: (nc - 1 - s, 0)
    st_shape = jax.ShapeDtypeStruct((D_STATE, D_SSD), F32)
    xa = _lane_tile_matrix()
    return pl.pallas_call(
        _ssd_kernel,
        grid=(nc,),
        in_specs=[pl.BlockSpec((blk, D_XBC), fwd), pl.BlockSpec((blk, D_XBC), bwd),
                  pl.BlockSpec((4 * N_DH, blk), lambda s: (0, s)),
                  pl.BlockSpec((4 * N_DH, blk), lambda s: (0, nc - 1 - s)),
                  pl.BlockSpec((blk, 4 * N_DH), fwd), pl.BlockSpec(xa.shape, const),
                  pl.BlockSpec((1, D_SSD), const),
                  pl.BlockSpec((D_STATE, D_SSD), const), pl.BlockSpec((D_STATE, D_SSD), const)],
        out_specs=[pl.BlockSpec((blk, D_SSD), fwd), pl.BlockSpec((blk, D_SSD), bwd),
                   pl.BlockSpec((D_STATE, D_SSD), const), pl.BlockSpec((D_STATE, D_SSD), const)],
        out_shape=[jax.ShapeDtypeStruct((L, D_SSD), BF16), jax.ShapeDtypeStruct((L, D_SSD), BF16),
                   st_shape, st_shape],
        compiler_params=pltpu.CompilerParams(dimension_semantics=("arbitrary",),
                                             vmem_limit_bytes=VMEM_LIMIT),
        name="ssd",
    )(xbc, xbc, rowpack, rowpack, colsplit, xa, dskip, h0f, h0b)


def _pool_constants():
    bands, inv_cols = [], []
    t = np.arange(POOL_SUB)
    rt, ct = t // GRID_W, t % GRID_W
    for w in POOL_WINDOWS:
        hw = w // 2
        k = np.arange(POOL_SUB + GRID_W * w)
        rk, ck = k // GRID_W, k % GRID_W
        band = ((rk[None, :] >= rt[:, None]) & (rk[None, :] < rt[:, None] + w)
                & (ck[None, :] >= ct[:, None] - hw) & (ck[None, :] < ct[:, None] + hw))
        bands.append(jnp.asarray(band, BF16))
        cnt_c = np.minimum(ct + hw, GRID_W) - np.maximum(ct - hw, 0)
        inv_cols.append(np.broadcast_to((1.0 / cnt_c)[:, None], (POOL_SUB, 128)))
    return bands, jnp.asarray(np.stack(inv_cols), F32)


def _merge_kernel(yf_ref, yb_ref, z_ref, up_ref, upp_ref, upn_ref, xln_ref, ng_ref, pw_ref, ps_ref, wo_ref,
                  g1_ref, l1g_ref, l1b_ref, band0_ref, band1_ref, band2_ref, band3_ref, invc_ref,
                  out_ref, pw_scr, wo_scr, *, tm, rows_total):
    i = pl.program_id(0)
    n = pl.num_programs(0)

    @pl.when(i == 0)
    def _():
        pw_scr[...] = pw_ref[...].astype(BF16)
        wo_scr[...] = wo_ref[...].astype(BF16)

    y = yf_ref[...].astype(F32) + yb_ref[...].astype(F32)
    g = y * _silu(z_ref[...].astype(F32))
    yn = (g * lax.rsqrt(jnp.mean(g * g, axis=-1, keepdims=True) + LN_EPS) * ng_ref[...]).astype(BF16)

    c = POOL_GROUP_DIM
    band_refs = (band0_ref, band1_ref, band2_ref, band3_ref)
    keep_p = (i > 0).astype(BF16)
    keep_n = (i < n - 1).astype(BF16)
    sub_row = lax.broadcasted_iota(jnp.int32, (POOL_SUB, 128), 0) // GRID_W
    nsub = tm // POOL_SUB
    sums = []
    for gi, w in enumerate(POOL_WINDOWS):
        hw = w // 2
        cols = slice(gi * c, (gi + 1) * c)
        ext = jnp.concatenate([upp_ref[:, cols] * keep_p, up_ref[:, cols], upn_ref[:, cols] * keep_n], axis=0)
        starts = [POOL_HALO + b * POOL_SUB - hw * GRID_W for b in range(nsub)]
        sums.append([_dot(band_refs[gi][...], ext[s:s + POOL_SUB + GRID_W * w]) for s in starts])
    diffs = []
    for gi, w in enumerate(POOL_WINDOWS):
        hw = w // 2
        cols = slice(gi * c, (gi + 1) * c)
        parts = []
        for b in range(nsub):
            row = sub_row + (i * tm + b * POOL_SUB) // GRID_W
            cnt_r = jnp.minimum(row + hw, rows_total) - jnp.maximum(row - hw, 0)
            inv = invc_ref[gi] / cnt_r.astype(F32)
            u = up_ref[b * POOL_SUB:(b + 1) * POOL_SUB, cols].astype(F32)
            parts.append((sums[gi][b] * jnp.concatenate([inv, inv], axis=1) - u).astype(BF16))
        diffs.append(jnp.concatenate(parts, axis=0))
    p = [(_dot(diffs[gi], pw_scr[gi]) * ps_ref[:, gi * c:(gi + 1) * c]).astype(BF16)
         for gi in range(len(POOL_WINDOWS))]
    lhs = jnp.concatenate([yn] + p, axis=1)
    hr = tm // 2
    for r in range(0, tm, hr):
        mix = _dot(lhs[r:r + hr], wo_scr[...])
        out_ref[r:r + hr, :] = _layer_norm(DEEPNORM_ALPHA * xln_ref[r:r + hr, :] + g1_ref[...] * mix,
                                           l1g_ref[...], l1b_ref[...])


def _merge_call(yf, yb, z, up, xln, norm_g, pool_w, pool_scale, w_out, g1, l1g, l1b, *, tm):
    L = xln.shape[0]
    nt = L // tm
    hb = tm // POOL_HALO
    const = lambda i: (0, 0)
    const3 = lambda i: (0, 0, 0)
    row = lambda i: (i, 0)
    vec = pl.BlockSpec((1, D_MODEL), const)
    bands, inv_cols = _pool_constants()
    return pl.pallas_call(
        functools.partial(_merge_kernel, tm=tm, rows_total=L // GRID_W),
        grid=(nt,),
        in_specs=[pl.BlockSpec((tm, D_SSD), row), pl.BlockSpec((tm, D_SSD), row),
                  pl.BlockSpec((tm, D_SSD), row),
                  pl.BlockSpec((tm, D_POOL), row),
                  pl.BlockSpec((POOL_HALO, D_POOL), lambda i: (jnp.maximum(i * hb - 1, 0), 0)),
                  pl.BlockSpec((POOL_HALO, D_POOL),
                               lambda i: (jnp.minimum((i + 1) * hb, L // POOL_HALO - 1), 0)),
                  pl.BlockSpec((tm, D_MODEL), row),
                  vec,
                  pl.BlockSpec((None,) + pool_w.shape[1:], lambda i: (0, 0, 0, 0), pipeline_mode=pl.Buffered(1)),
                  vec,
                  pl.BlockSpec((None,) + w_out.shape[1:], const3, pipeline_mode=pl.Buffered(1)),
                  vec, vec, vec]
                 + [pl.BlockSpec(bm.shape, const) for bm in bands]
                 + [pl.BlockSpec(inv_cols.shape, const3)],
        out_specs=pl.BlockSpec((tm, D_MODEL), row),
        out_shape=jax.ShapeDtypeStruct((L, D_MODEL), F32),
        scratch_shapes=[pltpu.VMEM(pool_w.shape[1:], BF16), pltpu.VMEM(w_out.shape[1:], BF16)],
        compiler_params=pltpu.CompilerParams(dimension_semantics=("arbitrary",),
                                             vmem_limit_bytes=VMEM_LIMIT),
        name="merge",
    )(yf, yb, z, up, up, up, xln, norm_g, pool_w, pool_scale, w_out, g1, l1g, l1b, *bands, inv_cols)


def _ffn_kernel(x_ref, sh_ref, sc_ref, g2_ref, wg_hbm, wu_hbm, wd_hbm, lg_ref, lb_ref, out_ref,
                wg_ref, wu_ref, wd_ref, stage_wide, stage_tall, sem):
    @pl.when(pl.program_id(0) == 0)
    def _():
        _round_weights([(wg_hbm, wg_ref, stage_wide, 0), (wu_hbm, wu_ref, stage_wide, 0),
                        (wd_hbm, wd_ref, stage_tall, 2)], sem)

    x = x_ref[...]
    h = (x * (1.0 + sc_ref[...]) + sh_ref[...]).astype(BF16)
    half = D_FF // 2
    ffn = None
    for s in (0, half):
        gate = _dot(h, wg_ref[:, s:s + half])
        upv = _dot(h, wu_ref[:, s:s + half])
        part = _dot((_silu(gate) * upv).astype(BF16), wd_ref[s:s + half, :])
        ffn = part if ffn is None else ffn + part
    out_ref[...] = _layer_norm(DEEPNORM_ALPHA * x + g2_ref[...] * ffn, lg_ref[...], lb_ref[...])


def _ffn_call(x, sh, sc, g2, wg, wu, wd, lg, lb, *, tm):
    L = x.shape[0]
    const = lambda i: (0, 0)
    row = lambda i: (i, 0)
    vec = pl.BlockSpec((1, D_MODEL), const)
    hbm = pl.BlockSpec(memory_space=pl.ANY)
    return pl.pallas_call(
        _ffn_kernel,
        grid=(L // tm,),
        in_specs=[pl.BlockSpec((tm, D_MODEL), row), vec, vec, vec, hbm, hbm, hbm, vec, vec],
        out_specs=pl.BlockSpec((tm, D_MODEL), row),
        out_shape=jax.ShapeDtypeStruct((L, D_MODEL), F32),
        scratch_shapes=[pltpu.VMEM(wg.shape[1:], BF16), pltpu.VMEM(wu.shape[1:], BF16),
                        pltpu.VMEM(wd.shape[1:], BF16),
                        pltpu.VMEM((2, D_MODEL // FFN_STAGE_CHUNKS, D_FF), F32),
                        pltpu.VMEM((2, D_FF // FFN_STAGE_CHUNKS, D_MODEL), F32),
                        pltpu.SemaphoreType.DMA((4,))],
        compiler_params=pltpu.CompilerParams(dimension_semantics=("arbitrary",),
                                             vmem_limit_bytes=VMEM_LIMIT),
        name="ffn",
    )(x, sh, sc, g2, wg, wu, wd, lg, lb)


def kernel(x, c, ctx, c_ctx, emb_ln_g, emb_ln_b, w_ada, b_ada, in_proj, conv_w, conv_b, dt_bias, a_log,
           d_skip, ssd_norm_g, pool_w, pool_scale, w_out, ln1_g, ln1_b, w_gate, w_up, w_down, ln2_g, ln2_b):
    assert x.shape[0] == 1 and w_ada.shape[0] == DEPTH == 1
    xl, xc = x[0], ctx[0]
    rowv = lambda v: v.reshape(1, -1)
    elg, elb = rowv(emb_ln_g), rowv(emb_ln_b)

    mod = _mod_call(jnp.stack([c[0], c_ctx], axis=1), w_ada, rowv(b_ada[0]))
    sh1, sc1, g1, sh2, sc2, g2 = [mod[0:1, k * D_MODEL:(k + 1) * D_MODEL] for k in range(6)]
    sh1c, sc1c = mod[1:2, 0:D_MODEL], mod[1:2, D_MODEL:2 * D_MODEL]

    conv_args = (conv_w, rowv(conv_b[0]), dt_bias[0].reshape(N_DH, 1), a_log[0].reshape(N_DH, 1))

    dskip = rowv(jnp.repeat(d_skip[0], SSD_HEAD_DIM))
    h_zero = jnp.zeros((D_STATE, D_SSD), F32)

    _, xbc_c, _, _, rp_c, cs_c, wz, wxd, wp = _inproj_call(xc, elg, elb, sh1c, sc1c, jnp.swapaxes(in_proj, 1, 2),
                                                           *conv_args, tm=xc.shape[0])
    _, _, hf_ctx, hb_ctx = _ssd_call(xbc_c, rp_c, cs_c, dskip, h_zero, h_zero)

    z, xbc, up, xln, rp, cs = _inproj_call(xl, elg, elb, sh1, sc1, (wz, wxd, wp), *conv_args, tm=INPROJ_TILE)
    yf, yb, _, _ = _ssd_call(xbc, rp, cs, dskip, hf_ctx, hb_ctx)
    x1 = _merge_call(yf, yb, z, up, xln, rowv(ssd_norm_g[0]), pool_w, rowv(pool_scale[0]),
                     w_out, g1, rowv(ln1_g[0]), rowv(ln1_b[0]), tm=MERGE_TILE)
    x2 = _ffn_call(x1, sh2, sc2, g2, w_gate, w_up, w_down, rowv(ln2_g[0]), rowv(ln2_b[0]), tm=FFN_TILE)
    return x2[None]
```

```python
import functools

import jax
import jax.numpy as jnp
import numpy as np
from jax import lax
from jax.experimental import pallas as pl
from jax.experimental.pallas import tpu as pltpu

F32 = jnp.float32
BF16 = jnp.bfloat16

D_MODEL = 1024
SSD_HEADS = 16
SSD_HEAD_DIM = 64
D_SSD = SSD_HEADS * SSD_HEAD_DIM
D_STATE = 128
D_CONV = 5
CHUNK = 128
D_POOL = 1024
POOL_WINDOWS = (2, 4, 8, 16)
POOL_GROUP_DIM = D_POOL // len(POOL_WINDOWS)
GRID_W = 64
D_XBC = D_SSD + 2 * D_STATE
D_XD = D_XBC + 128
D_FF = 2816
DEPTH = 1
DEEPNORM_ALPHA = (2 * DEPTH) ** 0.25
LN_EPS = 1e-5
LOG2_E = 1.4426950408889634

HALO = 16
POOL_HALO = 512
POOL_SUB = 256
SSD_BLOCK = 1024
PROJ_BLOCK = 256
CONV_BLOCK = 256
CONV_PHASES = 4
MOD_KB = 256
FFN_STAGE_CHUNKS = 8
FFN_TILE = 512
MERGE_TILE = 512
INPROJ_TILE = 1024
N_DH = 2 * SSD_HEADS
VMEM_LIMIT = 56 * 1024 * 1024


def _dot(a, b):
    return jnp.dot(a, b, preferred_element_type=F32)


def _dot_nt(a, b):
    return lax.dot_general(a, b, (((1,), (1,)), ((), ())), preferred_element_type=F32)


def _silu(x):
    hx = 0.5 * x
    return hx + hx * jnp.tanh(hx)


def _layer_norm(x, g, b):
    mu = jnp.mean(x, axis=-1, keepdims=True)
    xc = x - mu
    var = jnp.mean(xc * xc, axis=-1, keepdims=True)
    return xc * lax.rsqrt(var + LN_EPS) * g + b


def _round_weights(jobs, sem):
    steps = []
    for src, dst, stage, sem0 in jobs:
        rows = stage.shape[1]
        for k in range(dst.shape[0] // rows):
            slot = k % 2
            copy = pltpu.make_async_copy(src.at[0, pl.ds(k * rows, rows)], stage.at[slot], sem.at[sem0 + slot])
            steps.append((copy, stage, slot, dst, k * rows, rows))
    steps[0][0].start()
    for n, (copy, stage, slot, dst, r0, rows) in enumerate(steps):
        if n + 1 < len(steps):
            steps[n + 1][0].start()
        copy.wait()
        dst[r0:r0 + rows, :] = stage[slot].astype(BF16)


def _mod_kernel(cc_ref, w_ref, b_ref, out_ref):
    k = pl.program_id(0)

    @pl.when(k == 0)
    def _():
        out_ref[...] = jnp.broadcast_to(b_ref[...], out_ref.shape)

    s = _silu(cc_ref[...])
    w = w_ref[...]
    out_ref[0:1, :] += jnp.sum(w * s[:, 0:1], axis=0, keepdims=True)
    out_ref[1:2, :] += jnp.sum(w * s[:, 1:2], axis=0, keepdims=True)


def _mod_call(cc, w_ada, b_ada):
    n = w_ada.shape[-1]
    return pl.pallas_call(
        _mod_kernel,
        grid=(D_MODEL // MOD_KB,),
        in_specs=[pl.BlockSpec((MOD_KB, 2), lambda k: (k, 0)),
                  pl.BlockSpec((None, MOD_KB, n), lambda k: (0, k, 0)),
                  pl.BlockSpec((1, n), lambda k: (0, 0))],
        out_specs=pl.BlockSpec((8, n), lambda k: (0, 0)),
        out_shape=jax.ShapeDtypeStruct((8, n), F32),
        compiler_params=pltpu.CompilerParams(dimension_semantics=("arbitrary",),
                                             vmem_limit_bytes=VMEM_LIMIT),
        name="mod",
    )(cc, w_ada, b_ada)


def _inproj_kernel(x_ref, xp_ref, xn_ref, lng_ref, lnb_ref, sh_ref, sc_ref, wz_ref, wxd_ref, wp_ref,
                   cw_ref, cb_ref, dtb_ref, alog_ref,
                   z_ref, xbc_ref, up_ref, xln_ref, rowpack_ref, colsplit_ref, h_scr, u_scr, o_scr, *, tm):
    i = pl.program_id(0)
    n = pl.num_programs(0)
    lng, lnb, sh, sc = lng_ref[...], lnb_ref[...], sh_ref[...], sc_ref[...]

    def modulated(xn):
        return xn * (1.0 + sc) + sh

    xln = _layer_norm(x_ref[...], lng, lnb)
    xln_ref[...] = xln
    h = modulated(xln).astype(BF16)
    hp = (modulated(_layer_norm(xp_ref[...], lng, lnb)) * (i > 0).astype(F32)).astype(BF16)
    hn = (modulated(_layer_norm(xn_ref[...], lng, lnb)) * (i < n - 1).astype(F32)).astype(BF16)

    h_scr[0:HALO, :] = hp
    h_scr[HALO:HALO + tm, :] = h
    h_scr[HALO + tm:HALO + tm + HALO, :] = hn
    dt_raw = _dot(h_scr[HALO:HALO + tm, :], wxd_ref[:, D_XBC:D_XD]).T[0:N_DH, :] + dtb_ref[...]
    dt = jnp.maximum(dt_raw, 0.0) + jnp.log(1.0 + jnp.exp(-jnp.abs(dt_raw)))
    a = dt * (-jnp.exp(alog_ref[...]))
    lane = lax.broadcasted_iota(jnp.int32, (N_DH, tm), 1) & (CHUNK - 1)
    row = lax.broadcasted_iota(jnp.int32, (N_DH, tm), 0)
    cf, cr = a, a
    k = 1
    while k < CHUNK:
        cf = cf + jnp.where(lane >= k, pltpu.roll(cf, k, 1), 0.0)
        cr = cr + jnp.where(lane < CHUNK - k, pltpu.roll(cr, tm - k, 1), 0.0)
        k *= 2
    is_fwd = row < SSD_HEADS
    acum = jnp.where(is_fwd, cf, cr)
    e = jnp.exp(acum)
    w_end = jnp.exp(jnp.where(is_fwd, cr, cf) - a) * dt
    p = acum * LOG2_E
    src = p - jnp.log2(dt)
    diag = jnp.log2(dt[0:SSD_HEADS] + dt[SSD_HEADS:])
    rowpack_ref[...] = jnp.concatenate([src, diag, jnp.zeros_like(diag), e, w_end], axis=0)
    p1 = p.astype(BF16).astype(F32)
    p2 = (p - p1).astype(BF16).astype(F32)
    p3 = (p - p1 - p2).astype(BF16).astype(F32)
    colsplit_ref[...] = jnp.concatenate([p1, p2, p3, jnp.zeros_like(p1)], axis=0).T.astype(BF16)

    ne = tm + 2 * HALO
    nb, zb = CONV_BLOCK, PROJ_BLOCK
    zu_blocks = [(w_ref, o_ref, c) for w_ref, o_ref in ((wz_ref, z_ref), (wp_ref, up_ref))
                 for c in range(0, D_SSD, zb)]
    for j, c0 in enumerate(range(0, D_XBC, nb)):
        cols = slice(c0, c0 + nb)
        u = _dot(h_scr[...], wxd_ref[:, cols])
        take = -(-len(zu_blocks) // (D_XBC // nb - j))
        for w_ref, o_ref, c in zu_blocks[:take]:
            o_ref[:, c:c + zb] = _dot(h_scr[HALO:HALO + tm, :], w_ref[:, c:c + zb]).astype(BF16)
        zu_blocks = zu_blocks[take:]
        for s in range(nb // 128):
            lanes = slice(c0 + s * 128, c0 + (s + 1) * 128)
            slab = (c0 // 128) + s
            u_scr[slab] = u[:, s * 128:(s + 1) * 128]
            for b in range(CONV_PHASES):
                acc = cb_ref[:, lanes]
                for k in range(D_CONV):
                    rows = pl.ds(HALO + b + k - D_CONV // 2, tm // CONV_PHASES, stride=CONV_PHASES)
                    acc = acc + cw_ref[k:k + 1, lanes] * u_scr[slab, rows, :]
                o_scr[slab % 2, pl.ds(b, tm // CONV_PHASES, stride=CONV_PHASES), :] = _silu(acc)
            xbc_ref[:, lanes] = o_scr[slab % 2].astype(BF16)


def _inproj_convert_kernel(x_ref, xp_ref, xn_ref, lng_ref, lnb_ref, sh_ref, sc_ref, wf_ref,
                           cw_ref, cb_ref, dtb_ref, alog_ref,
                           z_ref, xbc_ref, up_ref, xln_ref, rowpack_ref, colsplit_ref,
                           wz_ref, wxd_ref, wp_ref, h_scr, u_scr, o_scr, *, tm):
    @pl.when(pl.program_id(0) == 0)
    def _():
        wz_ref[...] = wf_ref[0:D_SSD, :].T.astype(BF16)
        wxd_ref[...] = wf_ref[D_SSD:D_SSD + D_XD, :].T.astype(BF16)
        o3 = D_SSD + D_XBC + N_DH
        wp_ref[...] = wf_ref[o3:o3 + D_POOL, :].T.astype(BF16)

    _inproj_kernel(x_ref, xp_ref, xn_ref, lng_ref, lnb_ref, sh_ref, sc_ref, wz_ref, wxd_ref, wp_ref,
                   cw_ref, cb_ref, dtb_ref, alog_ref,
                   z_ref, xbc_ref, up_ref, xln_ref, rowpack_ref, colsplit_ref, h_scr, u_scr, o_scr, tm=tm)


def _inproj_call(x, ln_g, ln_b, sh, sc, weights, conv_w, conv_b, dt_bias, a_log, *, tm):
    L = x.shape[0]
    nt = L // tm
    hb = tm // HALO
    const = lambda i: (0, 0)
    row = lambda i: (i, 0)
    w_shapes = [(D_MODEL, D_SSD), (D_MODEL, D_XD), (D_MODEL, D_POOL)]
    w_specs = [pl.BlockSpec(s, const) for s in w_shapes]
    convert = not isinstance(weights, tuple)
    if convert:
        body = _inproj_convert_kernel
        w_in_specs = [pl.BlockSpec((None,) + weights.shape[1:], lambda i: (0, 0, 0),
                                   pipeline_mode=pl.Buffered(1))]
        w_args = (weights,)
    else:
        body, w_args = _inproj_kernel, weights
        w_in_specs = [pl.BlockSpec(s, const, pipeline_mode=pl.Buffered(1)) for s in w_shapes]
    return pl.pallas_call(
        functools.partial(body, tm=tm),
        grid=(nt,),
        in_specs=[pl.BlockSpec((tm, D_MODEL), row),
                  pl.BlockSpec((HALO, D_MODEL), lambda i: (jnp.maximum(i * hb - 1, 0), 0)),
                  pl.BlockSpec((HALO, D_MODEL), lambda i: (jnp.minimum((i + 1) * hb, L // HALO - 1), 0)),
                  pl.BlockSpec((1, D_MODEL), const), pl.BlockSpec((1, D_MODEL), const),
                  pl.BlockSpec((1, D_MODEL), const), pl.BlockSpec((1, D_MODEL), const)]
                 + w_in_specs
                 + [pl.BlockSpec((None, D_CONV, D_XBC), lambda i: (0, 0, 0)), pl.BlockSpec((1, D_XBC), const),
                    pl.BlockSpec((N_DH, 1), const), pl.BlockSpec((N_DH, 1), const)],
        out_specs=[pl.BlockSpec((tm, D_SSD), row),
                   pl.BlockSpec((tm, D_XBC), row),
                   pl.BlockSpec((tm, D_POOL), row),
                   pl.BlockSpec((tm, D_MODEL), row),
                   pl.BlockSpec((4 * N_DH, tm), lambda i: (0, i)),
                   pl.BlockSpec((tm, 4 * N_DH), row)] + (w_specs if convert else []),
        out_shape=[jax.ShapeDtypeStruct((L, D_SSD), BF16),
                   jax.ShapeDtypeStruct((L, D_XBC), BF16),
                   jax.ShapeDtypeStruct((L, D_POOL), BF16),
                   jax.ShapeDtypeStruct((L, D_MODEL), F32),
                   jax.ShapeDtypeStruct((4 * N_DH, L), F32),
                   jax.ShapeDtypeStruct((L, 4 * N_DH), BF16)]
                  + ([jax.ShapeDtypeStruct(s, BF16) for s in w_shapes] if convert else []),
        scratch_shapes=[pltpu.VMEM((tm + 2 * HALO, D_MODEL), BF16),
                        pltpu.VMEM((D_XBC // 128, tm + 2 * HALO, 128), F32),
                        pltpu.VMEM((2, tm, 128), F32)],
        compiler_params=pltpu.CompilerParams(dimension_semantics=("arbitrary",),
                                             vmem_limit_bytes=VMEM_LIMIT),
        name="inproj",
    )(x, x, x, ln_g, ln_b, sh, sc, *w_args, conv_w, conv_b, dt_bias, a_log)


def _across_lanes(rows, width):
    q = rows.shape[1]
    tall = jnp.concatenate([jnp.broadcast_to(rows[k:k + 1, :], (width, q)) for k in range(rows.shape[0])],
                           axis=0)
    return tall.T


def _lane_tile_matrix():
    m = np.zeros((4 * N_DH, N_DH * CHUNK), np.float32)
    for dh in range(N_DH):
        for piece in range(3):
            m[piece * N_DH + dh, dh * CHUNK:(dh + 1) * CHUNK] = 1.0
    return jnp.asarray(m, BF16)


def _ssd_kernel(xbcf_ref, xbcb_ref, rpf_ref, rpb_ref, cs_ref, xa_ref, dskip_ref, h0f_ref, h0b_ref,
                yf_ref, yb_ref, hf_ref, hb_ref):
    q = CHUNK
    nh = SSD_HEADS
    nsub = xbcf_ref.shape[0] // CHUNK

    @pl.when(pl.program_id(0) == 0)
    def _():
        hf_ref[...] = h0f_ref[...]
        hb_ref[...] = h0b_ref[...]

    def stream(xbc_ref, rp_ref, rows, d, h_ref, exit_row, emit):
        xs = xbc_ref[rows, 0:D_SSD]
        bm = xbc_ref[rows, D_SSD:D_SSD + D_STATE]
        cm = xbc_ref[rows, D_SSD + D_STATE:D_XBC]
        e = _across_lanes(rp_ref[(4 + d) * nh:(5 + d) * nh, rows], SSD_HEAD_DIM)
        w_end = _across_lanes(rp_ref[(6 + d) * nh:(7 + d) * nh, rows], SSD_HEAD_DIM)
        yield
        st = h_ref[...]
        emit(_dot(cm, st.astype(BF16)) * e)
        yield
        xw = (xs.astype(F32) * w_end).astype(BF16)
        upd = lax.dot_general(bm, xw, (((0,), (0,)), ((), ())), preferred_element_type=F32)
        h_ref[...] = st * e[exit_row:exit_row + 1, :] + upd
        yield

    def lead(sub, out):
        fr = slice(sub * q, (sub + 1) * q)
        br = slice((nsub - 1 - sub) * q, (nsub - sub) * q)

        def store_back(v):
            yb_ref[br, :] = v.astype(BF16)

        yield from stream(xbcf_ref, rpf_ref, fr, 0, hf_ref, q - 1, lambda v: out.update(y_off_f=v))
        yield from stream(xbcb_ref, rpb_ref, br, 1, hb_ref, 0, store_back)
        out.update(acol_f=_dot(cs_ref[fr, :], xa_ref[:, 0:nh * q]))
        yield
        out.update(acol_b=_dot(cs_ref[fr, :], xa_ref[:, nh * q:]),
                   cb=_dot_nt(xbcf_ref[fr, D_SSD + D_STATE:D_XBC], xbcf_ref[fr, D_SSD:D_SSD + D_STATE]))
        yield

    ti = lax.broadcasted_iota(jnp.int32, (q, q), 0)
    ui = lax.broadcasted_iota(jnp.int32, (q, q), 1)
    lane = lax.broadcasted_iota(jnp.int32, (q, 2 * SSD_HEAD_DIM), 1)

    def intra(sub, j, pre):
        fr = slice(sub * q, (sub + 1) * q)
        rp = rpf_ref[0:3 * nh, fr]
        cb = pre["cb"]
        ms = []
        for hh in (2 * j, 2 * j + 1):
            seg_f = pre["acol_f"][:, hh * q:(hh + 1) * q] - rp[hh:hh + 1, :]
            seg_b = pre["acol_b"][:, hh * q:(hh + 1) * q] - rp[nh + hh:nh + hh + 1, :]
            both = rp[2 * nh + hh:2 * nh + hh + 1, :]
            power = jnp.where(ui < ti, seg_f, jnp.where(ui > ti, seg_b, both))
            ms.append((cb * jnp.exp2(power)).astype(BF16))
        cols = slice(j * 2 * SSD_HEAD_DIM, (j + 1) * 2 * SSD_HEAD_DIM)
        xp = xbcf_ref[fr, cols]
        zero = jnp.zeros_like(xp)
        rhs = jnp.concatenate([jnp.where(lane < SSD_HEAD_DIM, xp, zero),
                               jnp.where(lane >= SSD_HEAD_DIM, xp, zero)], axis=0)
        y = (_dot(jnp.concatenate(ms, axis=1), rhs) + pre["y_off_f"][:, cols]
             + dskip_ref[:, cols] * xp.astype(F32))
        yf_ref[fr, cols] = y.astype(BF16)

    pre = [dict() for _ in range(nsub)]
    for _ in lead(0, pre[0]):
        pass
    for sub in range(nsub):
        nxt = lead(sub + 1, pre[sub + 1]) if sub + 1 < nsub else iter(())
        for j in range(SSD_HEADS // 2):
            intra(sub, j, pre[sub])
            next(nxt, None)
        for _ in nxt:
            pass


def _ssd_call(xbc, rowpack, colsplit, dskip, h0f, h0b):
    L = xbc.shape[0]
    blk = min(SSD_BLOCK, L)
    nc = L // blk
    const = lambda s: (0, 0)
    fwd = lambda s: (s, 0)
    bwd = lambda s: (nc - 1 - s, 0)
    st_shape = jax.ShapeDtypeStruct((D_STATE, D_SSD), F32)
    xa = _lane_tile_matrix()
    return pl.pallas_call(
        _ssd_kernel,
        grid=(nc,),
        in_specs=[pl.BlockSpec((blk, D_XBC), fwd), pl.BlockSpec((blk, D_XBC), bwd),
                  pl.BlockSpec((4 * N_DH, blk), lambda s: (0, s)),
                  pl.BlockSpec((4 * N_DH, blk), lambda s: (0, nc - 1 - s)),
                  pl.BlockSpec((blk, 4 * N_DH), fwd), pl.BlockSpec(xa.shape, const),
                  pl.BlockSpec((1, D_SSD), const),
                  pl.BlockSpec((D_STATE, D_SSD), const), pl.BlockSpec((D_STATE, D_SSD), const)],
        out_specs=[pl.BlockSpec((blk, D_SSD), fwd), pl.BlockSpec((blk, D_SSD), bwd),
                   pl.BlockSpec((D_STATE, D_SSD), const), pl.BlockSpec((D_STATE, D_SSD), const)],
        out_shape=[jax.ShapeDtypeStruct((L, D_SSD), BF16), jax.ShapeDtypeStruct((L, D_SSD), BF16),
                   st_shape, st_shape],
        compiler_params=pltpu.CompilerParams(dimension_semantics=("arbitrary",),
                                             vmem_limit_bytes=VMEM_LIMIT),
        name="ssd",
    )(xbc, xbc, rowpack, rowpack, colsplit, xa, dskip, h0f, h0b)


def _pool_constants():
    bands, inv_cols = [], []
    t = np.arange(POOL_SUB)
    rt, ct = t // GRID_W, t % GRID_W
    for w in POOL_WINDOWS:
        hw = w // 2
        k = np.arange(POOL_SUB + GRID_W * w)
        rk, ck = k // GRID_W, k % GRID_W
        band = ((rk[None, :] >= rt[:, None]) & (rk[None, :] < rt[:, None] + w)
                & (ck[None, :] >= ct[:, None] - hw) & (ck[None, :] < ct[:, None] + hw))
        bands.append(jnp.asarray(band, BF16))
        cnt_c = np.minimum(ct + hw, GRID_W) - np.maximum(ct - hw, 0)
        inv_cols.append(np.broadcast_to((1.0 / cnt_c)[:, None], (POOL_SUB, 128)))
    return bands, jnp.asarray(np.stack(inv_cols), F32)


def _merge_kernel(yf_ref, yb_ref, z_ref, up_ref, upp_ref, upn_ref, xln_ref, ng_ref, pw_ref, ps_ref, wo_ref,
                  g1_ref, l1g_ref, l1b_ref, band0_ref, band1_ref, band2_ref, band3_ref, invc_ref,
                  out_ref, pw_scr, wo_scr, *, tm, rows_total):
    i = pl.program_id(0)
    n = pl.num_programs(0)

    @pl.when(i == 0)
    def _():
        pw_scr[...] = pw_ref[...].astype(BF16)
        wo_scr[...] = wo_ref[...].astype(BF16)

    y = yf_ref[...].astype(F32) + yb_ref[...].astype(F32)
    g = y * _silu(z_ref[...].astype(F32))
    yn = (g * lax.rsqrt(jnp.mean(g * g, axis=-1, keepdims=True) + LN_EPS) * ng_ref[...]).astype(BF16)

    c = POOL_GROUP_DIM
    band_refs = (band0_ref, band1_ref, band2_ref, band3_ref)
    keep_p = (i > 0).astype(BF16)
    keep_n = (i < n - 1).astype(BF16)
    sub_row = lax.broadcasted_iota(jnp.int32, (POOL_SUB, 128), 0) // GRID_W
    nsub = tm // POOL_SUB
    sums = []
    for gi, w in enumerate(POOL_WINDOWS):
        hw = w // 2
        cols = slice(gi * c, (gi + 1) * c)
        ext = jnp.concatenate([upp_ref[:, cols] * keep_p, up_ref[:, cols], upn_ref[:, cols] * keep_n], axis=0)
        starts = [POOL_HALO + b * POOL_SUB - hw * GRID_W for b in range(nsub)]
        sums.append([_dot(band_refs[gi][...], ext[s:s + POOL_SUB + GRID_W * w]) for s in starts])
    diffs = []
    for gi, w in enumerate(POOL_WINDOWS):
        hw = w // 2
        cols = slice(gi * c, (gi + 1) * c)
        parts = []
        for b in range(nsub):
            row = sub_row + (i * tm + b * POOL_SUB) // GRID_W
            cnt_r = jnp.minimum(row + hw, rows_total) - jnp.maximum(row - hw, 0)
            inv = invc_ref[gi] / cnt_r.astype(F32)
            u = up_ref[b * POOL_SUB:(b + 1) * POOL_SUB, cols].astype(F32)
            parts.append((sums[gi][b] * jnp.concatenate([inv, inv], axis=1) - u).astype(BF16))
        diffs.append(jnp.concatenate(parts, axis=0))
    p = [(_dot(diffs[gi], pw_scr[gi]) * ps_ref[:, gi * c:(gi + 1) * c]).astype(BF16)
         for gi in range(len(POOL_WINDOWS))]
    lhs = jnp.concatenate([yn] + p, axis=1)
    hr = tm // 2
    for r in range(0, tm, hr):
        mix = _dot(lhs[r:r + hr], wo_scr[...])
        out_ref[r:r + hr, :] = _layer_norm(DEEPNORM_ALPHA * xln_ref[r:r + hr, :] + g1_ref[...] * mix,
                                           l1g_ref[...], l1b_ref[...])


def _merge_call(yf, yb, z, up, xln, norm_g, pool_w, pool_scale, w_out, g1, l1g, l1b, *, tm):
    L = xln.shape[0]
    nt = L // tm
    hb = tm // POOL_HALO
    const = lambda i: (0, 0)
    const3 = lambda i: (0, 0, 0)
    row = lambda i: (i, 0)
    vec = pl.BlockSpec((1, D_MODEL), const)
    bands, inv_cols = _pool_constants()
    return pl.pallas_call(
        functools.partial(_merge_kernel, tm=tm, rows_total=L // GRID_W),
        grid=(nt,),
        in_specs=[pl.BlockSpec((tm, D_SSD), row), pl.BlockSpec((tm, D_SSD), row),
                  pl.BlockSpec((tm, D_SSD), row),
                  pl.BlockSpec((tm, D_POOL), row),
                  pl.BlockSpec((POOL_HALO, D_POOL), lambda i: (jnp.maximum(i * hb - 1, 0), 0)),
                  pl.BlockSpec((POOL_HALO, D_POOL),
                               lambda i: (jnp.minimum((i + 1) * hb, L // POOL_HALO - 1), 0)),
                  pl.BlockSpec((tm, D_MODEL), row),
                  vec,
                  pl.BlockSpec((None,) + pool_w.shape[1:], lambda i: (0, 0, 0, 0), pipeline_mode=pl.Buffered(1)),
                  vec,
                  pl.BlockSpec((None,) + w_out.shape[1:], const3, pipeline_mode=pl.Buffered(1)),
                  vec, vec, vec]
                 + [pl.BlockSpec(bm.shape, const) for bm in bands]
                 + [pl.BlockSpec(inv_cols.shape, const3)],
        out_specs=pl.BlockSpec((tm, D_MODEL), row),
        out_shape=jax.ShapeDtypeStruct((L, D_MODEL), F32),
        scratch_shapes=[pltpu.VMEM(pool_w.shape[1:], BF16), pltpu.VMEM(w_out.shape[1:], BF16)],
        compiler_params=pltpu.CompilerParams(dimension_semantics=("arbitrary",),
                                             vmem_limit_bytes=VMEM_LIMIT),
        name="merge",
    )(yf, yb, z, up, up, up, xln, norm_g, pool_w, pool_scale, w_out, g1, l1g, l1b, *bands, inv_cols)


def _ffn_kernel(x_ref, sh_ref, sc_ref, g2_ref, wg_hbm, wu_hbm, wd_hbm, lg_ref, lb_ref, out_ref,
                wg_ref, wu_ref, wd_ref, stage_wide, stage_tall, sem):
    @pl.when(pl.program_id(0) == 0)
    def _():
        _round_weights([(wg_hbm, wg_ref, stage_wide, 0), (wu_hbm, wu_ref, stage_wide, 0),
                        (wd_hbm, wd_ref, stage_tall, 2)], sem)

    x = x_ref[...]
    h = (x * (1.0 + sc_ref[...]) + sh_ref[...]).astype(BF16)
    half = D_FF // 2
    ffn = None
    for s in (0, half):
        gate = _dot(h, wg_ref[:, s:s + half])
        upv = _dot(h, wu_ref[:, s:s + half])
        part = _dot((_silu(gate) * upv).astype(BF16), wd_ref[s:s + half, :])
        ffn = part if ffn is None else ffn + part
    out_ref[...] = _layer_norm(DEEPNORM_ALPHA * x + g2_ref[...] * ffn, lg_ref[...], lb_ref[...])


def _ffn_call(x, sh, sc, g2, wg, wu, wd, lg, lb, *, tm):
    L = x.shape[0]
    const = lambda i: (0, 0)
    row = lambda i: (i, 0)
    vec = pl.BlockSpec((1, D_MODEL), const)
    hbm = pl.BlockSpec(memory_space=pl.ANY)
    return pl.pallas_call(
        _ffn_kernel,
        grid=(L // tm,),
        in_specs=[pl.BlockSpec((tm, D_MODEL), row), vec, vec, vec, hbm, hbm, hbm, vec, vec],
        out_specs=pl.BlockSpec((tm, D_MODEL), row),
        out_shape=jax.ShapeDtypeStruct((L, D_MODEL), F32),
        scratch_shapes=[pltpu.VMEM(wg.shape[1:], BF16), pltpu.VMEM(wu.shape[1:], BF16),
                        pltpu.VMEM(wd.shape[1:], BF16),
                        pltpu.VMEM((2, D_MODEL // FFN_STAGE_CHUNKS, D_FF), F32),
                        pltpu.VMEM((2, D_FF // FFN_STAGE_CHUNKS, D_MODEL), F32),
                        pltpu.SemaphoreType.DMA((4,))],
        compiler_params=pltpu.CompilerParams(dimension_semantics=("arbitrary",),
                                             vmem_limit_bytes=VMEM_LIMIT),
        name="ffn",
    )(x, sh, sc, g2, wg, wu, wd, lg, lb)


def kernel(x, c, ctx, c_ctx, emb_ln_g, emb_ln_b, w_ada, b_ada, in_proj, conv_w, conv_b, dt_bias, a_log,
           d_skip, ssd_norm_g, pool_w, pool_scale, w_out, ln1_g, ln1_b, w_gate, w_up, w_down, ln2_g, ln2_b):
    assert x.shape[0] == 1 and w_ada.shape[0] == DEPTH == 1
    xl, xc = x[0], ctx[0]
    rowv = lambda v: v.reshape(1, -1)
    elg, elb = rowv(emb_ln_g), rowv(emb_ln_b)

    mod = _mod_call(jnp.stack([c[0], c_ctx], axis=1), w_ada, rowv(b_ada[0]))
    sh1, sc1, g1, sh2, sc2, g2 = [mod[0:1, k * D_MODEL:(k + 1) * D_MODEL] for k in range(6)]
    sh1c, sc1c = mod[1:2, 0:D_MODEL], mod[1:2, D_MODEL:2 * D_MODEL]

    conv_args = (conv_w, rowv(conv_b[0]), dt_bias[0].reshape(N_DH, 1), a_log[0].reshape(N_DH, 1))

    dskip = rowv(jnp.repeat(d_skip[0], SSD_HEAD_DIM))
    h_zero = jnp.zeros((D_STATE, D_SSD), F32)

    _, xbc_c, _, _, rp_c, cs_c, wz, wxd, wp = _inproj_call(xc, elg, elb, sh1c, sc1c, jnp.swapaxes(in_proj, 1, 2),
                                                           *conv_args, tm=xc.shape[0])
    _, _, hf_ctx, hb_ctx = _ssd_call(xbc_c, rp_c, cs_c, dskip, h_zero, h_zero)

    z, xbc, up, xln, rp, cs = _inproj_call(xl, elg, elb, sh1, sc1, (wz, wxd, wp), *conv_args, tm=INPROJ_TILE)
    yf, yb, _, _ = _ssd_call(xbc, rp, cs, dskip, hf_ctx, hb_ctx)
    x1 = _merge_call(yf, yb, z, up, xln, rowv(ssd_norm_g[0]), pool_w, rowv(pool_scale[0]),
                     w_out, g1, rowv(ln1_g[0]), rowv(ln1_b[0]), tm=MERGE_TILE)
    x2 = _ffn_call(x1, sh2, sc2, g2, w_gate, w_up, w_down, rowv(ln2_g[0]), rowv(ln2_b[0]), tm=FFN_TILE)
    return x2[None]
```

```python
import functools

import jax
import jax.numpy as jnp
import numpy as np
from jax import lax
from jax.experimental import pallas as pl
from jax.experimental.pallas import tpu as pltpu

F32 = jnp.float32
BF16 = jnp.bfloat16

D_MODEL = 1024
SSD_HEADS = 16
SSD_HEAD_DIM = 64
D_SSD = SSD_HEADS * SSD_HEAD_DIM
D_STATE = 128
D_CONV = 5
CHUNK = 128
D_POOL = 1024
POOL_WINDOWS = (2, 4, 8, 16)
POOL_GROUP_DIM = D_POOL // len(POOL_WINDOWS)
GRID_W = 64
D_XBC = D_SSD + 2 * D_STATE
D_XD = D_XBC + 128
D_FF = 2816
DEPTH = 1
DEEPNORM_ALPHA = (2 * DEPTH) ** 0.25
LN_EPS = 1e-5
LOG2_E = 1.4426950408889634

HALO = 16
POOL_HALO = 512
POOL_SUB = 256
SSD_BLOCK = 1024
PROJ_BLOCK = 256
CONV_BLOCK = 256
CONV_PHASES = 4
MOD_KB = 256
FFN_STAGE_CHUNKS = 8
FFN_TILE = 512
MERGE_TILE = 512
INPROJ_TILE = 1024
N_DH = 2 * SSD_HEADS
VMEM_LIMIT = 56 * 1024 * 1024


def _dot(a, b):
    return jnp.dot(a, b, preferred_element_type=F32)


def _dot_nt(a, b):
    return lax.dot_general(a, b, (((1,), (1,)), ((), ())), preferred_element_type=F32)


def _silu(x):
    hx = 0.5 * x
    return hx + hx * jnp.tanh(hx)


def _layer_norm(x, g, b):
    mu = jnp.mean(x, axis=-1, keepdims=True)
    xc = x - mu
    var = jnp.mean(xc * xc, axis=-1, keepdims=True)
    return xc * lax.rsqrt(var + LN_EPS) * g + b


def _round_weights(jobs, sem):
    steps = []
    for src, dst, stage, sem0 in jobs:
        rows = stage.shape[1]
        for k in range(dst.shape[0] // rows):
            slot = k % 2
            copy = pltpu.make_async_copy(src.at[0, pl.ds(k * rows, rows)], stage.at[slot], sem.at[sem0 + slot])
            steps.append((copy, stage, slot, dst, k * rows, rows))
    steps[0][0].start()
    for n, (copy, stage, slot, dst, r0, rows) in enumerate(steps):
        if n + 1 < len(steps):
            steps[n + 1][0].start()
        copy.wait()
        dst[r0:r0 + rows, :] = stage[slot].astype(BF16)


def _mod_kernel(cc_ref, w_ref, b_ref, out_ref):
    k = pl.program_id(0)

    @pl.when(k == 0)
    def _():
        out_ref[...] = jnp.broadcast_to(b_ref[...], out_ref.shape)

    s = _silu(cc_ref[...])
    w = w_ref[...]
    out_ref[0:1, :] += jnp.sum(w * s[:, 0:1], axis=0, keepdims=True)
    out_ref[1:2, :] += jnp.sum(w * s[:, 1:2], axis=0, keepdims=True)


def _mod_call(cc, w_ada, b_ada):
    n = w_ada.shape[-1]
    return pl.pallas_call(
        _mod_kernel,
        grid=(D_MODEL // MOD_KB,),
        in_specs=[pl.BlockSpec((MOD_KB, 2), lambda k: (k, 0)),
                  pl.BlockSpec((None, MOD_KB, n), lambda k: (0, k, 0)),
                  pl.BlockSpec((1, n), lambda k: (0, 0))],
        out_specs=pl.BlockSpec((8, n), lambda k: (0, 0)),
        out_shape=jax.ShapeDtypeStruct((8, n), F32),
        compiler_params=pltpu.CompilerParams(dimension_semantics=("arbitrary",),
                                             vmem_limit_bytes=VMEM_LIMIT),
        name="mod",
    )(cc, w_ada, b_ada)


def _inproj_kernel(x_ref, xp_ref, xn_ref, lng_ref, lnb_ref, sh_ref, sc_ref, wz_ref, wxd_ref, wp_ref,
                   cw_ref, cb_ref, dtb_ref, alog_ref,
                   z_ref, xbc_ref, up_ref, xln_ref, rowpack_ref, colsplit_ref, h_scr, u_scr, o_scr, *, tm):
    i = pl.program_id(0)
    n = pl.num_programs(0)
    lng, lnb, sh, sc = lng_ref[...], lnb_ref[...], sh_ref[...], sc_ref[...]

    def modulated(xn):
        return xn * (1.0 + sc) + sh

    xln = _layer_norm(x_ref[...], lng, lnb)
    xln_ref[...] = DEEPNORM_ALPHA * xln
    h = modulated(xln).astype(BF16)
    hp = (modulated(_layer_norm(xp_ref[...], lng, lnb)) * (i > 0).astype(F32)).astype(BF16)
    hn = (modulated(_layer_norm(xn_ref[...], lng, lnb)) * (i < n - 1).astype(F32)).astype(BF16)

    h_scr[0:HALO, :] = hp
    h_scr[HALO:HALO + tm, :] = h
    h_scr[HALO + tm:HALO + tm + HALO, :] = hn
    dt_raw = _dot(h_scr[HALO:HALO + tm, :], wxd_ref[:, D_XBC:D_XD]).T[0:N_DH, :] + dtb_ref[...]
    dt = jnp.maximum(dt_raw, 0.0) + jnp.log(1.0 + jnp.exp(-jnp.abs(dt_raw)))
    a = dt * (-jnp.exp(alog_ref[...]))
    lane = lax.broadcasted_iota(jnp.int32, (N_DH, tm), 1) & (CHUNK - 1)
    row = lax.broadcasted_iota(jnp.int32, (N_DH, tm), 0)
    cf, cr = a, a
    k = 1
    while k < CHUNK:
        cf = cf + jnp.where(lane >= k, pltpu.roll(cf, k, 1), 0.0)
        cr = cr + jnp.where(lane < CHUNK - k, pltpu.roll(cr, tm - k, 1), 0.0)
        k *= 2
    is_fwd = row < SSD_HEADS
    acum = jnp.where(is_fwd, cf, cr)
    e = jnp.exp(acum)
    w_end = jnp.exp(jnp.where(is_fwd, cr, cf) - a) * dt
    p = acum * LOG2_E
    src = p - jnp.log2(dt)
    diag = jnp.log2(dt[0:SSD_HEADS] + dt[SSD_HEADS:])
    rowpack_ref[...] = jnp.concatenate([src, diag, jnp.zeros_like(diag), e, w_end], axis=0)
    p1 = p.astype(BF16).astype(F32)
    p2 = (p - p1).astype(BF16).astype(F32)
    p3 = (p - p1 - p2).astype(BF16).astype(F32)
    colsplit_ref[...] = jnp.concatenate([p1, p2, p3, jnp.zeros_like(p1)], axis=0).T.astype(BF16)

    ne = tm + 2 * HALO
    nb, zb = CONV_BLOCK, PROJ_BLOCK
    zu_blocks = [(w_ref, o_ref, c, act) for w_ref, o_ref, act in ((wz_ref, z_ref, _silu), (wp_ref, up_ref, None))
                 for c in range(0, D_SSD, zb)]
    for j, c0 in enumerate(range(0, D_XBC, nb)):
        cols = slice(c0, c0 + nb)
        u = _dot(h_scr[...], wxd_ref[:, cols])
        take = -(-len(zu_blocks) // (D_XBC // nb - j))
        for w_ref, o_ref, c, act in zu_blocks[:take]:
            v = _dot(h_scr[HALO:HALO + tm, :], w_ref[:, c:c + zb])
            o_ref[:, c:c + zb] = (v if act is None else act(v)).astype(BF16)
        zu_blocks = zu_blocks[take:]
        for s in range(nb // 128):
            lanes = slice(c0 + s * 128, c0 + (s + 1) * 128)
            slab = (c0 // 128) + s
            u_scr[slab] = u[:, s * 128:(s + 1) * 128]
            for b in range(CONV_PHASES):
                acc = cb_ref[:, lanes]
                for k in range(D_CONV):
                    rows = pl.ds(HALO + b + k - D_CONV // 2, tm // CONV_PHASES, stride=CONV_PHASES)
                    acc = acc + cw_ref[k:k + 1, lanes] * u_scr[slab, rows, :]
                o_scr[slab % 2, pl.ds(b, tm // CONV_PHASES, stride=CONV_PHASES), :] = _silu(acc)
            xbc_ref[:, lanes] = o_scr[slab % 2].astype(BF16)


def _inproj_convert_kernel(x_ref, xp_ref, xn_ref, lng_ref, lnb_ref, sh_ref, sc_ref, wf_ref,
                           cw_ref, cb_ref, dtb_ref, alog_ref,
                           z_ref, xbc_ref, up_ref, xln_ref, rowpack_ref, colsplit_ref,
                           wz_ref, wxd_ref, wp_ref, h_scr, u_scr, o_scr, *, tm):
    @pl.when(pl.program_id(0) == 0)
    def _():
        wz_ref[...] = wf_ref[0:D_SSD, :].T.astype(BF16)
        wxd_ref[...] = wf_ref[D_SSD:D_SSD + D_XD, :].T.astype(BF16)
        o3 = D_SSD + D_XBC + N_DH
        wp_ref[...] = wf_ref[o3:o3 + D_POOL, :].T.astype(BF16)

    _inproj_kernel(x_ref, xp_ref, xn_ref, lng_ref, lnb_ref, sh_ref, sc_ref, wz_ref, wxd_ref, wp_ref,
                   cw_ref, cb_ref, dtb_ref, alog_ref,
                   z_ref, xbc_ref, up_ref, xln_ref, rowpack_ref, colsplit_ref, h_scr, u_scr, o_scr, tm=tm)


def _inproj_call(x, ln_g, ln_b, sh, sc, weights, conv_w, conv_b, dt_bias, a_log, *, tm):
    L = x.shape[0]
    nt = L // tm
    hb = tm // HALO
    const = lambda i: (0, 0)
    row = lambda i: (i, 0)
    w_shapes = [(D_MODEL, D_SSD), (D_MODEL, D_XD), (D_MODEL, D_POOL)]
    w_specs = [pl.BlockSpec(s, const) for s in w_shapes]
    convert = not isinstance(weights, tuple)
    if convert:
        body = _inproj_convert_kernel
        w_in_specs = [pl.BlockSpec((None,) + weights.shape[1:], lambda i: (0, 0, 0),
                                   pipeline_mode=pl.Buffered(1))]
        w_args = (weights,)
    else:
        body, w_args = _inproj_kernel, weights
        w_in_specs = [pl.BlockSpec(s, const, pipeline_mode=pl.Buffered(1)) for s in w_shapes]
    return pl.pallas_call(
        functools.partial(body, tm=tm),
        grid=(nt,),
        in_specs=[pl.BlockSpec((tm, D_MODEL), row),
                  pl.BlockSpec((HALO, D_MODEL), lambda i: (jnp.maximum(i * hb - 1, 0), 0)),
                  pl.BlockSpec((HALO, D_MODEL), lambda i: (jnp.minimum((i + 1) * hb, L // HALO - 1), 0)),
                  pl.BlockSpec((1, D_MODEL), const), pl.BlockSpec((1, D_MODEL), const),
                  pl.BlockSpec((1, D_MODEL), const), pl.BlockSpec((1, D_MODEL), const)]
                 + w_in_specs
                 + [pl.BlockSpec((None, D_CONV, D_XBC), lambda i: (0, 0, 0)), pl.BlockSpec((1, D_XBC), const),
                    pl.BlockSpec((N_DH, 1), const), pl.BlockSpec((N_DH, 1), const)],
        out_specs=[pl.BlockSpec((tm, D_SSD), row),
                   pl.BlockSpec((tm, D_XBC), row),
                   pl.BlockSpec((tm, D_POOL), row),
                   pl.BlockSpec((tm, D_MODEL), row),
                   pl.BlockSpec((4 * N_DH, tm), lambda i: (0, i)),
                   pl.BlockSpec((tm, 4 * N_DH), row)] + (w_specs if convert else []),
        out_shape=[jax.ShapeDtypeStruct((L, D_SSD), BF16),
                   jax.ShapeDtypeStruct((L, D_XBC), BF16),
                   jax.ShapeDtypeStruct((L, D_POOL), BF16),
                   jax.ShapeDtypeStruct((L, D_MODEL), F32),
                   jax.ShapeDtypeStruct((4 * N_DH, L), F32),
                   jax.ShapeDtypeStruct((L, 4 * N_DH), BF16)]
                  + ([jax.ShapeDtypeStruct(s, BF16) for s in w_shapes] if convert else []),
        scratch_shapes=[pltpu.VMEM((tm + 2 * HALO, D_MODEL), BF16),
                        pltpu.VMEM((D_XBC // 128, tm + 2 * HALO, 128), F32),
                        pltpu.VMEM((2, tm, 128), F32)],
        compiler_params=pltpu.CompilerParams(dimension_semantics=("arbitrary",),
                                             vmem_limit_bytes=VMEM_LIMIT),
        name="inproj",
    )(x, x, x, ln_g, ln_b, sh, sc, *w_args, conv_w, conv_b, dt_bias, a_log)


def _across_lanes(rows, width):
    q = rows.shape[1]
    tall = jnp.concatenate([jnp.broadcast_to(rows[k:k + 1, :], (width, q)) for k in range(rows.shape[0])],
                           axis=0)
    return tall.T


def _lane_tile_matrix():
    m = np.zeros((4 * N_DH, N_DH * CHUNK), np.float32)
    for dh in range(N_DH):
        for piece in range(3):
            m[piece * N_DH + dh, dh * CHUNK:(dh + 1) * CHUNK] = 1.0
    return jnp.asarray(m, BF16)


def _ssd_kernel(xbcf_ref, xbcb_ref, rpf_ref, rpb_ref, cs_ref, xa_ref, dskip_ref, h0f_ref, h0b_ref,
                yf_ref, yb_ref, hf_ref, hb_ref):
    q = CHUNK
    nh = SSD_HEADS
    nsub = xbcf_ref.shape[0] // CHUNK

    @pl.when(pl.program_id(0) == 0)
    def _():
        hf_ref[...] = h0f_ref[...]
        hb_ref[...] = h0b_ref[...]

    def stream(xbc_ref, rp_ref, rows, d, h_ref, exit_row, emit):
        xs = xbc_ref[rows, 0:D_SSD]
        bm = xbc_ref[rows, D_SSD:D_SSD + D_STATE]
        cm = xbc_ref[rows, D_SSD + D_STATE:D_XBC]
        e = _across_lanes(rp_ref[(4 + d) * nh:(5 + d) * nh, rows], SSD_HEAD_DIM)
        w_end = _across_lanes(rp_ref[(6 + d) * nh:(7 + d) * nh, rows], SSD_HEAD_DIM)
        yield
        st = h_ref[...]
        emit(_dot(cm, st.astype(BF16)) * e)
        yield
        xw = (xs.astype(F32) * w_end).astype(BF16)
        upd = lax.dot_general(bm, xw, (((0,), (0,)), ((), ())), preferred_element_type=F32)
        h_ref[...] = st * e[exit_row:exit_row + 1, :] + upd
        yield

    def lead(sub, out):
        fr = slice(sub * q, (sub + 1) * q)
        br = slice((nsub - 1 - sub) * q, (nsub - sub) * q)

        def store_back(v):
            yb_ref[br, :] = v.astype(BF16)

        yield from stream(xbcf_ref, rpf_ref, fr, 0, hf_ref, q - 1, lambda v: out.update(y_off_f=v))
        yield from stream(xbcb_ref, rpb_ref, br, 1, hb_ref, 0, store_back)
        out.update(acol_f=_dot(cs_ref[fr, :], xa_ref[:, 0:nh * q]))
        yield
        out.update(acol_b=_dot(cs_ref[fr, :], xa_ref[:, nh * q:]),
                   cb=_dot_nt(xbcf_ref[fr, D_SSD + D_STATE:D_XBC], xbcf_ref[fr, D_SSD:D_SSD + D_STATE]))
        yield

    ti = lax.broadcasted_iota(jnp.int32, (q, q), 0)
    ui = lax.broadcasted_iota(jnp.int32, (q, q), 1)
    lane = lax.broadcasted_iota(jnp.int32, (q, 2 * SSD_HEAD_DIM), 1)

    def intra(sub, j, pre):
        fr = slice(sub * q, (sub + 1) * q)
        rp = rpf_ref[0:3 * nh, fr]
        cb = pre["cb"]
        ms = []
        for hh in (2 * j, 2 * j + 1):
            seg_f = pre["acol_f"][:, hh * q:(hh + 1) * q] - rp[hh:hh + 1, :]
            seg_b = pre["acol_b"][:, hh * q:(hh + 1) * q] - rp[nh + hh:nh + hh + 1, :]
            both = rp[2 * nh + hh:2 * nh + hh + 1, :]
            power = jnp.where(ui < ti, seg_f, jnp.where(ui > ti, seg_b, both))
            ms.append((cb * jnp.exp2(power)).astype(BF16))
        cols = slice(j * 2 * SSD_HEAD_DIM, (j + 1) * 2 * SSD_HEAD_DIM)
        xp = xbcf_ref[fr, cols]
        zero = jnp.zeros_like(xp)
        rhs = jnp.concatenate([jnp.where(lane < SSD_HEAD_DIM, xp, zero),
                               jnp.where(lane >= SSD_HEAD_DIM, xp, zero)], axis=0)
        y = (_dot(jnp.concatenate(ms, axis=1), rhs) + pre["y_off_f"][:, cols]
             + dskip_ref[:, cols] * xp.astype(F32))
        yf_ref[fr, cols] = y.astype(BF16)

    pre = [dict() for _ in range(nsub)]
    for _ in lead(0, pre[0]):
        pass
    for sub in range(nsub):
        nxt = lead(sub + 1, pre[sub + 1]) if sub + 1 < nsub else iter(())
        for j in range(SSD_HEADS // 2):
            intra(sub, j, pre[sub])
            next(nxt, None)
        for _ in nxt:
            pass


def _ssd_call(xbc, rowpack, colsplit, dskip, h0f, h0b):
    L = xbc.shape[0]
    blk = min(SSD_BLOCK, L)
    nc = L // blk
    const = lambda s: (0, 0)
    fwd = lambda s: (s, 0)
    bwd = lambda s: (nc - 1 - s, 0)
    st_shape = jax.ShapeDtypeStruct((D_STATE, D_SSD), F32)
    xa = _lane_tile_matrix()
    return pl.pallas_call(
        _ssd_kernel,
        grid=(nc,),
        in_specs=[pl.BlockSpec((blk, D_XBC), fwd), pl.BlockSpec((blk, D_XBC), bwd),
                  pl.BlockSpec((4 * N_DH, blk), lambda s: (0, s)),
                  pl.BlockSpec((4 * N_DH, blk), lambda s: (0, nc - 1 - s)),
                  pl.BlockSpec((blk, 4 * N_DH), fwd), pl.BlockSpec(xa.shape, const),
                  pl.BlockSpec((1, D_SSD), const),
                  pl.BlockSpec((D_STATE, D_SSD), const), pl.BlockSpec((D_STATE, D_SSD), const)],
        out_specs=[pl.BlockSpec((blk, D_SSD), fwd), pl.BlockSpec((blk, D_SSD), bwd),
                   pl.BlockSpec((D_STATE, D_SSD), const), pl.BlockSpec((D_STATE, D_SSD), const)],
        out_shape=[jax.ShapeDtypeStruct((L, D_SSD), BF16), jax.ShapeDtypeStruct((L, D_SSD), BF16),
                   st_shape, st_shape],
        compiler_params=pltpu.CompilerParams(dimension_semantics=("arbitrary",),
                                             vmem_limit_bytes=VMEM_LIMIT),
        name="ssd",
    )(xbc, xbc, rowpack, rowpack, colsplit, xa, dskip, h0f, h0b)


def _pool_constants():
    bands, inv_cols = [], []
    t = np.arange(POOL_SUB)
    rt, ct = t // GRID_W, t % GRID_W
    for w in POOL_WINDOWS:
        hw = w // 2
        k = np.arange(POOL_SUB + GRID_W * w)
        rk, ck = k // GRID_W, k % GRID_W
        band = ((rk[None, :] >= rt[:, None]) & (rk[None, :] < rt[:, None] + w)
                & (ck[None, :] >= ct[:, None] - hw) & (ck[None, :] < ct[:, None] + hw))
        bands.append(jnp.asarray(band, BF16))
        cnt_c = np.minimum(ct + hw, GRID_W) - np.maximum(ct - hw, 0)
        inv_cols.append(np.broadcast_to((1.0 / cnt_c)[:, None], (POOL_SUB, 128)))
    return bands, jnp.asarray(np.stack(inv_cols), F32)


def _merge_kernel(yf_ref, yb_ref, z_ref, up_ref, upp_ref, upn_ref, xln_ref, ng_ref, pw_ref, ps_ref, wo_ref,
                  g1_ref, l1g_ref, l1b_ref, band0_ref, band1_ref, band2_ref, band3_ref, invc_ref,
                  out_ref, pw_scr, wo_scr, *, tm, rows_total):
    i = pl.program_id(0)
    n = pl.num_programs(0)

    @pl.when(i == 0)
    def _():
        pw_scr[...] = pw_ref[...].astype(BF16)
        wo_scr[...] = wo_ref[...].astype(BF16)

    y = yf_ref[...].astype(F32) + yb_ref[...].astype(F32)
    g = y * z_ref[...].astype(F32)
    yn = (g * lax.rsqrt(jnp.mean(g * g, axis=-1, keepdims=True) + LN_EPS) * ng_ref[...]).astype(BF16)

    c = POOL_GROUP_DIM
    band_refs = (band0_ref, band1_ref, band2_ref, band3_ref)
    keep_p = (i > 0).astype(BF16)
    keep_n = (i < n - 1).astype(BF16)
    sub_row = lax.broadcasted_iota(jnp.int32, (POOL_SUB, 128), 0) // GRID_W
    nsub = tm // POOL_SUB
    sums = []
    for gi, w in enumerate(POOL_WINDOWS):
        hw = w // 2
        cols = slice(gi * c, (gi + 1) * c)
        ext = jnp.concatenate([upp_ref[:, cols] * keep_p, up_ref[:, cols], upn_ref[:, cols] * keep_n], axis=0)
        starts = [POOL_HALO + b * POOL_SUB - hw * GRID_W for b in range(nsub)]
        sums.append([_dot(band_refs[gi][...], ext[s:s + POOL_SUB + GRID_W * w]) for s in starts])
    diffs = []
    for gi, w in enumerate(POOL_WINDOWS):
        hw = w // 2
        cols = slice(gi * c, (gi + 1) * c)
        parts = []
        for b in range(nsub):
            row = sub_row + (i * tm + b * POOL_SUB) // GRID_W
            cnt_r = jnp.minimum(row + hw, rows_total) - jnp.maximum(row - hw, 0)
            inv = invc_ref[gi] / cnt_r.astype(F32)
            u = up_ref[b * POOL_SUB:(b + 1) * POOL_SUB, cols].astype(F32)
            parts.append((sums[gi][b] * jnp.concatenate([inv, inv], axis=1) - u).astype(BF16))
        diffs.append(jnp.concatenate(parts, axis=0))
    p = [(_dot(diffs[gi], pw_scr[gi]) * ps_ref[:, gi * c:(gi + 1) * c]).astype(BF16)
         for gi in range(len(POOL_WINDOWS))]
    lhs = jnp.concatenate([yn] + p, axis=1)
    hr = tm // 2
    for r in range(0, tm, hr):
        mix = _dot(lhs[r:r + hr], wo_scr[...])
        out_ref[r:r + hr, :] = _layer_norm(xln_ref[r:r + hr, :] + g1_ref[...] * mix,
                                           l1g_ref[...], l1b_ref[...])


def _merge_call(yf, yb, z, up, xln, norm_g, pool_w, pool_scale, w_out, g1, l1g, l1b, *, tm):
    L = xln.shape[0]
    nt = L // tm
    hb = tm // POOL_HALO
    const = lambda i: (0, 0)
    const3 = lambda i: (0, 0, 0)
    row = lambda i: (i, 0)
    vec = pl.BlockSpec((1, D_MODEL), const)
    bands, inv_cols = _pool_constants()
    return pl.pallas_call(
        functools.partial(_merge_kernel, tm=tm, rows_total=L // GRID_W),
        grid=(nt,),
        in_specs=[pl.BlockSpec((tm, D_SSD), row), pl.BlockSpec((tm, D_SSD), row),
                  pl.BlockSpec((tm, D_SSD), row),
                  pl.BlockSpec((tm, D_POOL), row),
                  pl.BlockSpec((POOL_HALO, D_POOL), lambda i: (jnp.maximum(i * hb - 1, 0), 0)),
                  pl.BlockSpec((POOL_HALO, D_POOL),
                               lambda i: (jnp.minimum((i + 1) * hb, L // POOL_HALO - 1), 0)),
                  pl.BlockSpec((tm, D_MODEL), row),
                  vec,
                  pl.BlockSpec((None,) + pool_w.shape[1:], lambda i: (0, 0, 0, 0), pipeline_mode=pl.Buffered(1)),
                  vec,
                  pl.BlockSpec((None,) + w_out.shape[1:], const3, pipeline_mode=pl.Buffered(1)),
                  vec, vec, vec]
                 + [pl.BlockSpec(bm.shape, const) for bm in bands]
                 + [pl.BlockSpec(inv_cols.shape, const3)],
        out_specs=pl.BlockSpec((tm, D_MODEL), row),
        out_shape=jax.ShapeDtypeStruct((L, D_MODEL), F32),
        scratch_shapes=[pltpu.VMEM(pool_w.shape[1:], BF16), pltpu.VMEM(w_out.shape[1:], BF16)],
        compiler_params=pltpu.CompilerParams(dimension_semantics=("arbitrary",),
                                             vmem_limit_bytes=VMEM_LIMIT),
        name="merge",
    )(yf, yb, z, up, up, up, xln, norm_g, pool_w, pool_scale, w_out, g1, l1g, l1b, *bands, inv_cols)


def _ffn_kernel(x_ref, sh_ref, sc_ref, g2_ref, wg_hbm, wu_hbm, wd_hbm, lg_ref, lb_ref, out_ref,
                wg_ref, wu_ref, wd_ref, stage_wide, stage_tall, sem):
    @pl.when(pl.program_id(0) == 0)
    def _():
        _round_weights([(wg_hbm, wg_ref, stage_wide, 0), (wu_hbm, wu_ref, stage_wide, 0),
                        (wd_hbm, wd_ref, stage_tall, 2)], sem)

    x = x_ref[...]
    h = (x * (1.0 + sc_ref[...]) + sh_ref[...]).astype(BF16)
    half = D_FF // 2
    ffn = None
    for s in (0, half):
        gate = _dot(h, wg_ref[:, s:s + half])
        upv = _dot(h, wu_ref[:, s:s + half])
        part = _dot((_silu(gate) * upv).astype(BF16), wd_ref[s:s + half, :])
        ffn = part if ffn is None else ffn + part
    out_ref[...] = _layer_norm(DEEPNORM_ALPHA * x + g2_ref[...] * ffn, lg_ref[...], lb_ref[...])


def _ffn_call(x, sh, sc, g2, wg, wu, wd, lg, lb, *, tm):
    L = x.shape[0]
    const = lambda i: (0, 0)
    row = lambda i: (i, 0)
    vec = pl.BlockSpec((1, D_MODEL), const)
    hbm = pl.BlockSpec(memory_space=pl.ANY)
    return pl.pallas_call(
        _ffn_kernel,
        grid=(L // tm,),
        in_specs=[pl.BlockSpec((tm, D_MODEL), row), vec, vec, vec, hbm, hbm, hbm, vec, vec],
        out_specs=pl.BlockSpec((tm, D_MODEL), row),
        out_shape=jax.ShapeDtypeStruct((L, D_MODEL), F32),
        scratch_shapes=[pltpu.VMEM(wg.shape[1:], BF16), pltpu.VMEM(wu.shape[1:], BF16),
                        pltpu.VMEM(wd.shape[1:], BF16),
                        pltpu.VMEM((2, D_MODEL // FFN_STAGE_CHUNKS, D_FF), F32),
                        pltpu.VMEM((2, D_FF // FFN_STAGE_CHUNKS, D_MODEL), F32),
                        pltpu.SemaphoreType.DMA((4,))],
        compiler_params=pltpu.CompilerParams(dimension_semantics=("arbitrary",),
                                             vmem_limit_bytes=VMEM_LIMIT),
        name="ffn",
    )(x, sh, sc, g2, wg, wu, wd, lg, lb)


def kernel(x, c, ctx, c_ctx, emb_ln_g, emb_ln_b, w_ada, b_ada, in_proj, conv_w, conv_b, dt_bias, a_log,
           d_skip, ssd_norm_g, pool_w, pool_scale, w_out, ln1_g, ln1_b, w_gate, w_up, w_down, ln2_g, ln2_b):
    assert x.shape[0] == 1 and w_ada.shape[0] == DEPTH == 1
    xl, xc = x[0], ctx[0]
    rowv = lambda v: v.reshape(1, -1)
    elg, elb = rowv(emb_ln_g), rowv(emb_ln_b)

    mod = _mod_call(jnp.stack([c[0], c_ctx], axis=1), w_ada, rowv(b_ada[0]))
    sh1, sc1, g1, sh2, sc2, g2 = [mod[0:1, k * D_MODEL:(k + 1) * D_MODEL] for k in range(6)]
    sh1c, sc1c = mod[1:2, 0:D_MODEL], mod[1:2, D_MODEL:2 * D_MODEL]

    conv_args = (conv_w, rowv(conv_b[0]), dt_bias[0].reshape(N_DH, 1), a_log[0].reshape(N_DH, 1))

    dskip = rowv(jnp.repeat(d_skip[0], SSD_HEAD_DIM))
    h_zero = jnp.zeros((D_STATE, D_SSD), F32)

    _, xbc_c, _, _, rp_c, cs_c, wz, wxd, wp = _inproj_call(xc, elg, elb, sh1c, sc1c, jnp.swapaxes(in_proj, 1, 2),
                                                           *conv_args, tm=xc.shape[0])
    _, _, hf_ctx, hb_ctx = _ssd_call(xbc_c, rp_c, cs_c, dskip, h_zero, h_zero)

    z, xbc, up, xln, rp, cs = _inproj_call(xl, elg, elb, sh1, sc1, (wz, wxd, wp), *conv_args, tm=INPROJ_TILE)
    yf, yb, _, _ = _ssd_call(xbc, rp, cs, dskip, hf_ctx, hb_ctx)
    x1 = _merge_call(yf, yb, z, up, xln, rowv(ssd_norm_g[0]), pool_w, rowv(pool_scale[0]),
                     w_out, g1, rowv(ln1_g[0]), rowv(ln1_b[0]), tm=MERGE_TILE)
    x2 = _ffn_call(x1, sh2, sc2, g2, w_gate, w_up, w_down, rowv(ln2_g[0]), rowv(ln2_b[0]), tm=FFN_TILE)
    return x2[None]
```
